```python
import math
import jax, jax.numpy as jnp
from jax import lax
import numpy as np

D_MODEL = 2048
BATCH = 4
SEQ = 8192
DEPTH = 1
DEC_BATCH = 32
DEC_SEQ = 64
PAST_LEN = 2048

CHUNK = 64
D_ATTN = D_MODEL // 2
D_HG = D_MODEL - D_ATTN
HEAD_DIM = 128
N_HEADS = D_ATTN // HEAD_DIM
N_KV = 2
GROUP = N_HEADS // N_KV
WINDOW = 128
N_WIN_CHUNKS = WINDOW // CHUNK
NUM_BUCKETS = 32
MAX_DISTANCE = 128
HG_DK = 128
HG_DV = 128
HG_HEADS = D_HG // HG_DV
HG_BLOCK = 16
D_FF = 5632
PLE_DIM = 256
EPS = 1e-6
NEG_INF = -1e30
IN_SPLITS = (D_ATTN, N_KV * HEAD_DIM, N_KV * HEAD_DIM, HG_HEADS * HG_DK, HG_HEADS * HG_DK, HG_HEADS * HG_DV, HG_HEADS * HG_DV)
D_IN = sum(IN_SPLITS)
IN_OFFSETS = tuple(int(o) for o in np.cumsum(IN_SPLITS)[:-1])

kernel_name = 'hybrid_swa_hgrn2_streaming_step'


def rms_norm(x, gain):
    xf = x.astype(jnp.float32)
    y = xf * lax.rsqrt(jnp.mean(xf * xf, axis=-1, keepdims=True) + EPS)
    return (y * gain.astype(jnp.float32)).astype(x.dtype)


def swiglu_half_step(x, pre, post, w_gate, w_up, w_down):
    h = rms_norm(x, pre)
    y = (jax.nn.silu(h @ w_gate) * (h @ w_up)) @ w_down
    return x + 0.5 * rms_norm(y, post)


def t5_bucket(rel):
    half = NUM_BUCKETS // 2
    max_exact = half // 2
    n = jnp.abs(rel)
    nf = jnp.maximum(n, 1).astype(jnp.float32)
    large = max_exact + (jnp.log(nf / max_exact) / math.log(MAX_DISTANCE / max_exact) * (half - max_exact)).astype(jnp.int32)
    large = jnp.minimum(large, half - 1)
    return jnp.where(rel > 0, half, 0) + jnp.where(n < max_exact, n, large)


def rel_bias(table, n_q, n_k, offset):
    rel = jnp.arange(n_k)[None, :] - offset - jnp.arange(n_q)[:, None]
    return jnp.transpose(table[t5_bucket(rel)], (2, 0, 1)).astype(jnp.float32)


def attend(q, k, v, bias, sinks, key_mask=None):
    b, n, lq = q.shape[:3]
    lk = k.shape[2]
    qg = q.reshape(b, n, lq, N_KV, GROUP, HEAD_DIM)
    s = jnp.einsum('bnqkgd,bnskd->bnkgqs', qg, k).astype(jnp.float32) * (HEAD_DIM ** -0.5)
    s = s + bias.reshape(N_KV, GROUP, lq, lk)
    if key_mask is not None:
        s = jnp.where(key_mask[None, :, None, None, None, :], s, NEG_INF)
    sink = sinks.astype(jnp.float32).reshape(N_KV, GROUP, 1, 1)
    m = jnp.maximum(jnp.max(s, axis=-1, keepdims=True), sink)
    e = jnp.exp(s - m)
    p = e / (jnp.sum(e, axis=-1, keepdims=True) + jnp.exp(sink - m))
    o = jnp.einsum('bnkgqs,bnskd->bnqkgd', p.astype(v.dtype), v)
    return o.reshape(b, n, lq, N_HEADS * HEAD_DIM)


def swa_prompt(q, k, v, table, sinks):
    b, t = q.shape[:2]
    nc = t // CHUNK
    lk = (N_WIN_CHUNKS + 1) * CHUNK
    qb = q.reshape(b, nc, CHUNK, N_HEADS, HEAD_DIM)

    def band(a):
        ap = jnp.pad(a, ((0, 0), (WINDOW, 0), (0, 0), (0, 0))).reshape(b, nc + N_WIN_CHUNKS, CHUNK, N_KV, HEAD_DIM)
        return jnp.concatenate([ap[:, j:j + nc] for j in range(N_WIN_CHUNKS + 1)], axis=2)

    key_pos = jnp.arange(nc)[:, None] * CHUNK - WINDOW + jnp.arange(lk)[None, :]
    o = attend(qb, band(k), band(v), rel_bias(table, CHUNK, lk, WINDOW), sinks, key_pos >= 0)
    return o.reshape(b, t, N_HEADS * HEAD_DIM)


def swa_sample(q, k, v, k_cache, v_cache, table, sinks):
    b, t = q.shape[:2]
    wc = k_cache.shape[1]
    k_all = jnp.concatenate([k_cache.astype(k.dtype), k], axis=1)
    v_all = jnp.concatenate([v_cache.astype(v.dtype), v], axis=1)
    o = attend(q[:, None], k_all[:, None], v_all[:, None], rel_bias(table, t, wc + t, wc), sinks)
    return o.reshape(b, t, N_HEADS * HEAD_DIM), k_all[:, t:], v_all[:, t:]


def hgrn2_mixer(q, f_logit, i, g, lb, norm_gain, s0):
    b, t = q.shape[:2]
    f32 = jnp.float32
    lbf = lb.astype(f32)
    f = lbf + (1.0 - lbf) * jax.nn.sigmoid(f_logit.astype(f32))
    n_blk = -(-t // HG_BLOCK)
    pad = n_blk * HG_BLOCK - t

    def blocks(a, d):
        a = jnp.pad(a.astype(f32), ((0, 0), (0, pad), (0, 0))).reshape(b, n_blk, HG_BLOCK, HG_HEADS, d)
        return jnp.transpose(a, (1, 0, 3, 2, 4))

    qb = blocks(q, HG_DK)
    log_f = blocks(jnp.log(f), HG_DK)
    kb = blocks(1.0 - f, HG_DK)
    ib = blocks(i, HG_DV)
    cum = jnp.cumsum(log_f, axis=3)
    q_dec = qb * jnp.exp(cum)
    k_inv = kb * jnp.exp(-cum)
    k_end = kb * jnp.exp(cum[:, :, :, -1:, :] - cum)
    blk_decay = jnp.exp(cum[:, :, :, -1, :])
    causal = jnp.tril(jnp.ones((HG_BLOCK, HG_BLOCK), dtype=bool))
    a = jnp.where(causal, jnp.einsum('nbhtk,nbhsk->nbhts', q_dec, k_inv), 0.0)
    intra = jnp.einsum('nbhts,nbhsv->nbhtv', a, ib)

    def step(state, xs):
        q_d, k_e, iv, dec = xs
        out = jnp.einsum('bhtk,bhkv->bhtv', q_d, state)
        state = dec[..., None] * state + jnp.einsum('bhtk,bhtv->bhkv', k_e, iv)
        return state, out

    s_final, inter = lax.scan(step, s0.astype(f32), (q_dec, k_end, ib, blk_decay))
    o = jnp.transpose(intra + inter, (1, 0, 3, 2, 4)).reshape(b, n_blk * HG_BLOCK, HG_HEADS, HG_DV)[:, :t]
    o = rms_norm(o, norm_gain) * jax.nn.silu(g.astype(f32).reshape(b, t, HG_HEADS, HG_DV))
    return o.reshape(b, t, D_HG).astype(q.dtype), s_final.astype(s0.dtype)


def trunk_layer(x, p, lb, lp, table, s0, kv_cache):
    x = swiglu_half_step(x, lp['ffn1_pre'], lp['ffn1_post'], lp['ffn1_w_gate'], lp['ffn1_w_up'], lp['ffn1_w_down'])
    b, t = x.shape[:2]
    z = rms_norm(x, lp['mix_pre']) @ lp['w_in']
    qa, ka, va, qh, fh, ih, gh = jnp.split(z, IN_OFFSETS, axis=-1)
    qa = qa.reshape(b, t, N_HEADS, HEAD_DIM)
    ka = ka.reshape(b, t, N_KV, HEAD_DIM)
    va = va.reshape(b, t, N_KV, HEAD_DIM)
    if kv_cache is None:
        attn = swa_prompt(qa, ka, va, table, lp['attn_sinks'])
        keep = min(WINDOW, t)
        k_new, v_new = ka[:, t - keep:], va[:, t - keep:]
    else:
        attn, k_new, v_new = swa_sample(qa, ka, va, kv_cache[0], kv_cache[1], table, lp['attn_sinks'])
    rec, s_new = hgrn2_mixer(qh, fh, ih, gh, lb, lp['hgrn_norm'], s0)
    mix = jnp.concatenate([attn, rec], axis=-1) @ lp['w_out']
    x = x + rms_norm(mix, lp['mix_post'])
    x = swiglu_half_step(x, lp['ffn2_pre'], lp['ffn2_post'], lp['ffn2_w_gate'], lp['ffn2_w_up'], lp['ffn2_w_down'])
    h = rms_norm(x, lp['ple_pre'])
    y = jax.nn.sigmoid(h @ lp['w_ple_gate']) * (p @ lp['w_ple_proj'])
    x = x + rms_norm(y, lp['ple_post'])
    return x, k_new, v_new, s_new


def setup_inputs(seed: int = 0) -> dict:
    key = jax.random.key(seed)
    ks = jax.random.split(key, 32)
    f32 = jnp.float32

    def nrm(k, shape, scale):
        return jax.random.normal(k, shape, f32) * scale

    def gain(k, shape):
        return 1.0 + 0.05 * jax.random.normal(k, shape, f32)

    wc = min(WINDOW, PAST_LEN)
    return {
        'x_prompt': nrm(ks[0], (BATCH, SEQ, D_MODEL), 1.0),
        'x_sample': nrm(ks[1], (DEC_BATCH, DEC_SEQ, D_MODEL), 1.0),
        'cache_attn_k': nrm(ks[2], (DEPTH, DEC_BATCH, wc, N_KV, HEAD_DIM), 1.0),
        'cache_attn_v': nrm(ks[3], (DEPTH, DEC_BATCH, wc, N_KV, HEAD_DIM), 1.0),
        'state_hgrn': nrm(ks[4], (DEPTH, DEC_BATCH, HG_HEADS, HG_DK, HG_DV), 0.1),
        'p_prompt': nrm(ks[5], (DEPTH, BATCH, SEQ, PLE_DIM), 1.0),
        'p_sample': nrm(ks[6], (DEPTH, DEC_BATCH, DEC_SEQ, PLE_DIM), 1.0),
        'rel_bias_table': nrm(ks[7], (NUM_BUCKETS, N_HEADS), 0.5),
        'ffn1_pre': gain(ks[8], (DEPTH, D_MODEL)),
        'ffn1_post': gain(ks[9], (DEPTH, D_MODEL)),
        'ffn1_w_gate': nrm(ks[10], (DEPTH, D_MODEL, D_FF), D_MODEL ** -0.5),
        'ffn1_w_up': nrm(ks[11], (DEPTH, D_MODEL, D_FF), D_MODEL ** -0.5),
        'ffn1_w_down': nrm(ks[12], (DEPTH, D_FF, D_MODEL), D_FF ** -0.5),
        'mix_pre': gain(ks[13], (DEPTH, D_MODEL)),
        'mix_post': gain(ks[14], (DEPTH, D_MODEL)),
        'w_in': nrm(ks[15], (DEPTH, D_MODEL, D_IN), D_MODEL ** -0.5),
        'w_out': nrm(ks[16], (DEPTH, D_MODEL, D_MODEL), D_MODEL ** -0.5),
        'attn_sinks': nrm(ks[17], (DEPTH, N_HEADS), 0.5),
        'hgrn_lb_logits': nrm(ks[18], (DEPTH + 1, D_HG), 0.5),
        'hgrn_norm': gain(ks[19], (DEPTH, HG_DV)),
        'ffn2_pre': gain(ks[20], (DEPTH, D_MODEL)),
        'ffn2_post': gain(ks[21], (DEPTH, D_MODEL)),
        'ffn2_w_gate': nrm(ks[22], (DEPTH, D_MODEL, D_FF), D_MODEL ** -0.5),
        'ffn2_w_up': nrm(ks[23], (DEPTH, D_MODEL, D_FF), D_MODEL ** -0.5),
        'ffn2_w_down': nrm(ks[24], (DEPTH, D_FF, D_MODEL), D_FF ** -0.5),
        'ple_pre': gain(ks[25], (DEPTH, D_MODEL)),
        'ple_post': gain(ks[26], (DEPTH, D_MODEL)),
        'w_ple_gate': nrm(ks[27], (DEPTH, D_MODEL, D_MODEL), D_MODEL ** -0.5),
        'w_ple_proj': nrm(ks[28], (DEPTH, PLE_DIM, D_MODEL), PLE_DIM ** -0.5),
    }


def reference(x_prompt, x_sample, cache_attn_k, cache_attn_v, state_hgrn, p_prompt, p_sample,
              rel_bias_table, ffn1_pre, ffn1_post, ffn1_w_gate, ffn1_w_up, ffn1_w_down,
              mix_pre, mix_post, w_in, w_out, attn_sinks, hgrn_lb_logits, hgrn_norm,
              ffn2_pre, ffn2_post, ffn2_w_gate, ffn2_w_up, ffn2_w_down,
              ple_pre, ple_post, w_ple_gate, w_ple_proj):
    lower_bounds = jnp.cumsum(jax.nn.softmax(hgrn_lb_logits.astype(jnp.float32), axis=0), axis=0)
    yp, ys = x_prompt, x_sample
    kp_l, vp_l, sp_l, ks_l, vs_l, ss_l = [], [], [], [], [], []
    for l in range(DEPTH):
        lp = {
            'ffn1_pre': ffn1_pre[l], 'ffn1_post': ffn1_post[l], 'ffn1_w_gate': ffn1_w_gate[l],
            'ffn1_w_up': ffn1_w_up[l], 'ffn1_w_down': ffn1_w_down[l],
            'mix_pre': mix_pre[l], 'mix_post': mix_post[l], 'w_in': w_in[l], 'w_out': w_out[l],
            'attn_sinks': attn_sinks[l], 'hgrn_norm': hgrn_norm[l],
            'ffn2_pre': ffn2_pre[l], 'ffn2_post': ffn2_post[l], 'ffn2_w_gate': ffn2_w_gate[l],
            'ffn2_w_up': ffn2_w_up[l], 'ffn2_w_down': ffn2_w_down[l],
            'ple_pre': ple_pre[l], 'ple_post': ple_post[l], 'w_ple_gate': w_ple_gate[l], 'w_ple_proj': w_ple_proj[l],
        }
        s0_prompt = jnp.zeros((yp.shape[0], HG_HEADS, HG_DK, HG_DV), state_hgrn.dtype)
        yp, kp, vp, sp = trunk_layer(yp, p_prompt[l], lower_bounds[l], lp, rel_bias_table, s0_prompt, None)
        ys, kn, vn, sn = trunk_layer(ys, p_sample[l], lower_bounds[l], lp, rel_bias_table, state_hgrn[l],
                                     (cache_attn_k[l], cache_attn_v[l]))
        kp_l.append(kp); vp_l.append(vp); sp_l.append(sp)
        ks_l.append(kn); vs_l.append(vn); ss_l.append(sn)
    return (yp, ys, jnp.stack(kp_l), jnp.stack(vp_l), jnp.stack(sp_l), jnp.stack(ks_l), jnp.stack(vs_l), jnp.stack(ss_l))
```

```python
import functools
import math

import jax
import jax.numpy as jnp
from jax import lax
from jax.experimental import pallas as pl
from jax.experimental.pallas import tpu as pltpu

F32 = jnp.float32
BF16 = jnp.bfloat16

D_MODEL = 2048
CHUNK = 64
D_ATTN = 1024
HEAD_DIM = 128
N_HEADS = 8
N_KV = 2
GROUP = N_HEADS // N_KV
WINDOW = 128
LK = WINDOW + CHUNK
NUM_BUCKETS = 32
MAX_DISTANCE = 128
HG_DK = 128
HG_DV = 128
HG_HEADS = 8
D_HG = HG_HEADS * HG_DV
HG_SUB = 16
HG_SUB_SHIFT = HG_SUB.bit_length() - 1
CHUNK_SHIFT = CHUNK.bit_length() - 1
D_FF = 5632
PLE_DIM = 256
EPS = 1e-6
NEG_INF = -1e30
D_IN = D_ATTN + 2 * N_KV * HEAD_DIM + 4 * D_HG

Z_QH, Z_FH, Z_IH, Z_GH = 1, 2, 3, 4
Z_K, Z_V = 20, 21

VMEM_LIMIT_BYTES = 60 * 1024 * 1024

FFN_TM = 1024
FFN_TF = 512
PROJ_TM = 1024
PROJ_TN = 512
ROW_TM = 512
NORM_ROWS = 256
ATTN_TQ = 256
HG_TT = 256


def _rms(x, gain):
    y = x * lax.rsqrt(jnp.mean(x * x, axis=-1, keepdims=True) + EPS)
    return y * gain


def _sigmoid(x):
    return 1.0 / (1.0 + jnp.exp(-x))


def _cparams(sem):
    return pltpu.CompilerParams(dimension_semantics=sem, vmem_limit_bytes=VMEM_LIMIT_BYTES)


def _ffn_kernel(x_ref, pre_ref, post_ref, wg_ref, wu_ref, wd_ref, o_ref, xn_ref):
    j = pl.program_id(1)
    tm = x_ref.shape[0]

    @pl.when(j == 0)
    def _():
        def body(r, c):
            rows = pl.ds(pl.multiple_of(r * NORM_ROWS, NORM_ROWS), NORM_ROWS)
            xn_ref[rows, :] = _rms(x_ref[rows, :], pre_ref[...]).astype(BF16)
            return c
        lax.fori_loop(0, tm // NORM_ROWS, body, 0)

    xn = xn_ref[...]
    g = jnp.dot(xn, wg_ref[...], preferred_element_type=F32)
    u = jnp.dot(xn, wu_ref[...], preferred_element_type=F32)
    h = (g * _sigmoid(g) * u).astype(BF16)
    part = jnp.dot(h, wd_ref[...], preferred_element_type=F32)

    @pl.when(j == 0)
    def _():
        o_ref[...] = part

    @pl.when(j > 0)
    def _():
        o_ref[...] += part

    @pl.when(j == pl.num_programs(1) - 1)
    def _():
        def body(r, c):
            rows = pl.ds(pl.multiple_of(r * NORM_ROWS, NORM_ROWS), NORM_ROWS)
            o_ref[rows, :] = x_ref[rows, :] + 0.5 * _rms(o_ref[rows, :], post_ref[...])
            return c
        lax.fori_loop(0, tm // NORM_ROWS, body, 0)


def _ffn(x, pre, post, wg, wu, wd):
    t = x.shape[0]
    tm = min(FFN_TM, t)
    grid = (t // tm, D_FF // FFN_TF)
    return pl.pallas_call(
        _ffn_kernel,
        grid=grid,
        in_specs=[
            pl.BlockSpec((tm, D_MODEL), lambda i, j: (i, 0)),
            pl.BlockSpec((1, D_MODEL), lambda i, j: (0, 0)),
            pl.BlockSpec((1, D_MODEL), lambda i, j: (0, 0)),
            pl.BlockSpec((D_MODEL, FFN_TF), lambda i, j: (0, j)),
            pl.BlockSpec((D_MODEL, FFN_TF), lambda i, j: (0, j)),
            pl.BlockSpec((FFN_TF, D_MODEL), lambda i, j: (j, 0)),
        ],
        out_specs=pl.BlockSpec((tm, D_MODEL), lambda i, j: (i, 0)),
        out_shape=jax.ShapeDtypeStruct((t, D_MODEL), F32),
        scratch_shapes=[pltpu.VMEM((tm, D_MODEL), BF16)],
        compiler_params=_cparams(("parallel", "arbitrary")),
        name="ffn_half_step",
    )(x, pre, post, wg, wu, wd)


def _inproj_kernel(x_ref, pre_ref, w_ref, z_ref, xn_ref):
    j = pl.program_id(1)
    tm = x_ref.shape[0]

    @pl.when(j == 0)
    def _():
        def body(r, c):
            rows = pl.ds(pl.multiple_of(r * NORM_ROWS, NORM_ROWS), NORM_ROWS)
            xn_ref[rows, :] = _rms(x_ref[rows, :], pre_ref[...]).astype(BF16)
            return c
        lax.fori_loop(0, tm // NORM_ROWS, body, 0)

    z_ref[...] = jnp.dot(xn_ref[...], w_ref[...], preferred_element_type=F32)


def _inproj(x, pre, w_in):
    t = x.shape[0]
    tm = min(PROJ_TM, t)
    grid = (t // tm, D_IN // PROJ_TN)
    return pl.pallas_call(
        _inproj_kernel,
        grid=grid,
        in_specs=[
            pl.BlockSpec((tm, D_MODEL), lambda i, j: (i, 0)),
            pl.BlockSpec((1, D_MODEL), lambda i, j: (0, 0)),
            pl.BlockSpec((D_MODEL, PROJ_TN), lambda i, j: (0, j)),
        ],
        out_specs=pl.BlockSpec((tm, PROJ_TN), lambda i, j: (i, j)),
        out_shape=jax.ShapeDtypeStruct((t, D_IN), F32),
        scratch_shapes=[pltpu.VMEM((tm, D_MODEL), BF16)],
        compiler_params=_cparams(("parallel", "arbitrary")),
        name="mixer_in_proj",
    )(x, pre, w_in)


def _attn_kernel(bucket_ref, table_ref, sinks_ref, q_ref, kc_ref, vc_ref, kp_ref, vp_ref,
                 o_ref, bias_ref, k_all, v_all, *, mask_history):
    b = pl.program_id(0)
    i = pl.program_id(1)
    tq = q_ref.shape[0]

    @pl.when((b == 0) & (i == 0))
    def _():
        bucket = bucket_ref[...]
        for h in range(N_HEADS):
            acc = jnp.zeros((CHUNK, LK), F32)
            for n in range(NUM_BUCKETS):
                acc = jnp.where(bucket == n, table_ref[n, h], acc)
            g, hh = divmod(h, GROUP)
            bias_ref[g, hh * CHUNK:(hh + 1) * CHUNK, :] = acc

    k_all[0:WINDOW, :] = kp_ref[...].astype(BF16)
    v_all[0:WINDOW, :] = vp_ref[...].astype(BF16)
    k_all[WINDOW:, :] = kc_ref[...].astype(BF16)
    v_all[WINDOW:, :] = vc_ref[...].astype(BF16)

    row = lax.broadcasted_iota(jnp.int32, (GROUP * CHUNK, 1), 0)
    col = lax.broadcasted_iota(jnp.int32, (GROUP * CHUNK, LK), 1)
    scale = HEAD_DIM ** -0.5

    for c in range(tq // CHUNK):
        for g in range(N_KV):
            heads = [g * GROUP + hh for hh in range(GROUP)]
            qs = jnp.concatenate(
                [q_ref[c * CHUNK:(c + 1) * CHUNK, h * HEAD_DIM:(h + 1) * HEAD_DIM] for h in heads],
                axis=0).astype(BF16)
            kk = k_all[c * CHUNK:c * CHUNK + LK, g * HEAD_DIM:(g + 1) * HEAD_DIM]
            vv = v_all[c * CHUNK:c * CHUNK + LK, g * HEAD_DIM:(g + 1) * HEAD_DIM]
            s = lax.dot_general(qs, kk, (((1,), (1,)), ((), ())), preferred_element_type=F32)
            s = s * scale + bias_ref[g]
            if mask_history and c * CHUNK < WINDOW:
                s = jnp.where(col + (i * tq + (c * CHUNK - WINDOW)) >= 0, s, NEG_INF)
            sink = jnp.zeros((GROUP * CHUNK, 1), F32)
            for hh, h in enumerate(heads):
                sink = jnp.where((row >> CHUNK_SHIFT) == hh, sinks_ref[h], sink)
            m = jnp.maximum(jnp.max(s, axis=-1, keepdims=True), sink)
            e = jnp.exp(s - m)
            p = e / (jnp.sum(e, axis=-1, keepdims=True) + jnp.exp(sink - m))
            o = jnp.dot(p.astype(BF16), vv, preferred_element_type=F32)
            for hh, h in enumerate(heads):
                o_ref[c * CHUNK:(c + 1) * CHUNK, h * HEAD_DIM:(h + 1) * HEAD_DIM] = (
                    o[hh * CHUNK:(hh + 1) * CHUNK, :].astype(BF16))


def _attention(bucket, table, sinks, z, k_prev, v_prev, *, n_seq, seq_len, prev_from_z):
    tq = min(ATTN_TQ, seq_len)
    nt = seq_len // tq
    if prev_from_z:
        per_seq, per_tile = seq_len // WINDOW, tq // WINDOW

        def kp_map(b, i):
            return (jnp.maximum(b * per_seq + i * per_tile - 1, 0), Z_K)

        def vp_map(b, i):
            return (jnp.maximum(b * per_seq + i * per_tile - 1, 0), Z_V)
    else:
        def kp_map(b, i):
            return (b, 0)
        vp_map = kp_map
    kv_w = N_KV * HEAD_DIM
    smem = pl.BlockSpec(memory_space=pltpu.SMEM)
    return pl.pallas_call(
        functools.partial(_attn_kernel, mask_history=prev_from_z),
        grid=(n_seq, nt),
        in_specs=[
            pl.BlockSpec((CHUNK, LK), lambda b, i: (0, 0)),
            smem,
            smem,
            pl.BlockSpec((tq, D_ATTN), lambda b, i: (b * nt + i, 0)),
            pl.BlockSpec((tq, kv_w), lambda b, i: (b * nt + i, Z_K)),
            pl.BlockSpec((tq, kv_w), lambda b, i: (b * nt + i, Z_V)),
            pl.BlockSpec((WINDOW, kv_w), kp_map),
            pl.BlockSpec((WINDOW, kv_w), vp_map),
        ],
        out_specs=pl.BlockSpec((tq, D_ATTN), lambda b, i: (b * nt + i, 0)),
        out_shape=jax.ShapeDtypeStruct((n_seq * seq_len, D_ATTN), BF16),
        scratch_shapes=[
            pltpu.VMEM((N_KV, GROUP * CHUNK, LK), F32),
            pltpu.VMEM((WINDOW + tq, kv_w), BF16),
            pltpu.VMEM((WINDOW + tq, kv_w), BF16),
        ],
        compiler_params=_cparams(("arbitrary", "arbitrary")),
        name="swa_attention",
    )(bucket, table, sinks, z, z, z, k_prev, v_prev)


def _hgrn_kernel(lbl_ref, gain_ref, s0_ref, q_ref, f_ref, i_ref, g_ref, rec_ref, sout_ref, st_ref,
                 *, layer):
    ti = pl.program_id(1)
    tt = q_ref.shape[0]
    nsub = CHUNK // HG_SUB

    @pl.when(ti == 0)
    def _():
        for h in range(HG_HEADS):
            st_ref[h] = s0_ref[0, h].T

    logits = lbl_ref[...]
    ex = jnp.exp(logits - jnp.max(logits, axis=0, keepdims=True))
    lb = jnp.sum(ex[:layer + 1], axis=0, keepdims=True) / jnp.sum(ex, axis=0, keepdims=True)

    r_i = lax.broadcasted_iota(jnp.int32, (2 * CHUNK, CHUNK), 0)
    c_i = lax.broadcasted_iota(jnp.int32, (2 * CHUNK, CHUNK), 1)
    sub_r = (r_i & (CHUNK - 1)) >> HG_SUB_SHIFT
    sub_c = c_i >> HG_SUB_SHIFT
    within = jnp.where((sub_r == sub_c) & (c_i <= r_i), 1.0, 0.0)
    before = jnp.where(sub_c < sub_r, 1.0, 0.0)
    csum = jnp.where(r_i < CHUNK, within, before).astype(BF16)
    t_i = lax.broadcasted_iota(jnp.int32, (CHUNK, CHUNK), 0)
    s_i = lax.broadcasted_iota(jnp.int32, (CHUNK, CHUNK), 1)
    causal = s_i <= t_i

    def chunk_body(c, carry):
        rows = pl.ds(pl.multiple_of(c * CHUNK, CHUNK), CHUNK)
        f = lb + (1.0 - lb) * _sigmoid(f_ref[rows, :])
        lf = jnp.log(f)
        k = 1.0 - f
        hi = lf.astype(BF16)
        r1 = lf - hi.astype(F32)
        mid = r1.astype(BF16)
        lo = (r1 - mid.astype(F32)).astype(BF16)
        cr = (jnp.dot(csum, hi, preferred_element_type=F32)
              + jnp.dot(csum, mid, preferred_element_type=F32)
              + jnp.dot(csum, lo, preferred_element_type=F32))
        cc = cr[:CHUNK]
        rr = cr[CHUNK:]
        bb = cc + rr
        q = q_ref[rows, :]
        iv = i_ref[rows, :].astype(BF16)
        qd = (q * jnp.exp(cc)).astype(BF16)
        qe = (q * jnp.exp(bb)).astype(BF16)
        b_end = bb[CHUNK - 1:CHUNK, :]
        ke = (k * jnp.exp(b_end - bb)).astype(BF16)
        dec = jnp.exp(b_end)
        gate = g_ref[rows, :]
        gate = gate * _sigmoid(gate)
        for h in range(HG_HEADS):
            sl = slice(h * HG_DK, (h + 1) * HG_DK)
            st = st_ref[h]
            parts = []
            for j in range(nsub):
                kj = (k[:, sl] * jnp.exp(rr[j * HG_SUB:j * HG_SUB + 1, sl] - bb[:, sl])).astype(BF16)
                parts.append(lax.dot_general(qd[j * HG_SUB:(j + 1) * HG_SUB, sl], kj,
                                             (((1,), (1,)), ((), ())), preferred_element_type=F32))
            a = jnp.where(causal, jnp.concatenate(parts, axis=0), 0.0).astype(BF16)
            o = (lax.dot_general(qe[:, sl], st.astype(BF16), (((1,), (1,)), ((), ())),
                                 preferred_element_type=F32)
                 + jnp.dot(a, iv[:, sl], preferred_element_type=F32))
            upd = lax.dot_general(iv[:, sl], ke[:, sl], (((0,), (0,)), ((), ())),
                                  preferred_element_type=F32)
            st_ref[h] = st * dec[:, sl] + upd
            rec_ref[rows, sl] = (_rms(o, gain_ref[...]) * gate[:, sl]).astype(BF16)
        return carry

    lax.fori_loop(0, tt // CHUNK, chunk_body, 0)

    @pl.when(ti == pl.num_programs(1) - 1)
    def _():
        for h in range(HG_HEADS):
            sout_ref[0, h] = st_ref[h].T


def _hgrn(lb_logits, gain, s0, z, *, n_seq, seq_len, layer):
    tt = min(HG_TT, seq_len)
    nt = seq_len // tt
    n_layers = lb_logits.shape[0]

    def zspec(col):
        return pl.BlockSpec((tt, D_HG), lambda b, i: (b * nt + i, col))

    state_spec = pl.BlockSpec((1, HG_HEADS, HG_DK, HG_DV), lambda b, i: (b, 0, 0, 0))
    return pl.pallas_call(
        functools.partial(_hgrn_kernel, layer=layer),
        grid=(n_seq, nt),
        in_specs=[
            pl.BlockSpec((n_layers, D_HG), lambda b, i: (0, 0)),
            pl.BlockSpec((1, HG_DV), lambda b, i: (0, 0)),
            state_spec,
            zspec(Z_QH), zspec(Z_FH), zspec(Z_IH), zspec(Z_GH),
        ],
        out_specs=[
            pl.BlockSpec((tt, D_HG), lambda b, i: (b * nt + i, 0)),
            state_spec,
        ],
        out_shape=[
            jax.ShapeDtypeStruct((n_seq * seq_len, D_HG), BF16),
            jax.ShapeDtypeStruct((n_seq, HG_HEADS, HG_DK, HG_DV), F32),
        ],
        scratch_shapes=[pltpu.VMEM((HG_HEADS, HG_DV, HG_DK), F32)],
        compiler_params=_cparams(("arbitrary", "arbitrary")),
        name="hgrn2_mixer",
    )(lb_logits, gain, s0, z, z, z, z)


def _outproj_kernel(x_ref, a_ref, r_ref, w_ref, post_ref, o_ref):
    mix = (jnp.dot(a_ref[...], w_ref[0:D_ATTN, :], preferred_element_type=F32)
           + jnp.dot(r_ref[...], w_ref[D_ATTN:, :], preferred_element_type=F32))
    o_ref[...] = x_ref[...] + _rms(mix, post_ref[...])


def _outproj(x, attn, rec, w_out, post):
    t = x.shape[0]
    tm = min(ROW_TM, t)
    return pl.pallas_call(
        _outproj_kernel,
        grid=(t // tm,),
        in_specs=[
            pl.BlockSpec((tm, D_MODEL), lambda i: (i, 0)),
            pl.BlockSpec((tm, D_ATTN), lambda i: (i, 0)),
            pl.BlockSpec((tm, D_HG), lambda i: (i, 0)),
            pl.BlockSpec((D_MODEL, D_MODEL), lambda i: (0, 0)),
            pl.BlockSpec((1, D_MODEL), lambda i: (0, 0)),
        ],
        out_specs=pl.BlockSpec((tm, D_MODEL), lambda i: (i, 0)),
        out_shape=jax.ShapeDtypeStruct((t, D_MODEL), F32),
        compiler_params=_cparams(("parallel",)),
        name="mixer_out_proj",
    )(x, attn, rec, w_out, post)


def _ple_kernel(x_ref, p_ref, pre_ref, post_ref, wg_ref, wp_ref, o_ref):
    x = x_ref[...]
    h = _rms(x, pre_ref[...]).astype(BF16)
    gate = _sigmoid(jnp.dot(h, wg_ref[...], preferred_element_type=F32))
    proj = jnp.dot(p_ref[...].astype(BF16), wp_ref[...], preferred_element_type=F32)
    o_ref[...] = x + _rms(gate * proj, post_ref[...])


def _ple(x, p, pre, post, w_gate, w_proj):
    t = x.shape[0]
    tm = min(ROW_TM, t)
    return pl.pallas_call(
        _ple_kernel,
        grid=(t // tm,),
        in_specs=[
            pl.BlockSpec((tm, D_MODEL), lambda i: (i, 0)),
            pl.BlockSpec((tm, PLE_DIM), lambda i: (i, 0)),
            pl.BlockSpec((1, D_MODEL), lambda i: (0, 0)),
            pl.BlockSpec((1, D_MODEL), lambda i: (0, 0)),
            pl.BlockSpec((D_MODEL, D_MODEL), lambda i: (0, 0)),
            pl.BlockSpec((PLE_DIM, D_MODEL), lambda i: (0, 0)),
        ],
        out_specs=pl.BlockSpec((tm, D_MODEL), lambda i: (i, 0)),
        out_shape=jax.ShapeDtypeStruct((t, D_MODEL), F32),
        compiler_params=_cparams(("parallel",)),
        name="ple_embed",
    )(x, p, pre, post, w_gate, w_proj)


def _t5_bucket(rel):
    half = NUM_BUCKETS // 2
    max_exact = half // 2
    n = jnp.abs(rel)
    nf = jnp.maximum(n, 1).astype(jnp.float32)
    large = max_exact + (jnp.log(nf / max_exact) / math.log(MAX_DISTANCE / max_exact)
                         * (half - max_exact)).astype(jnp.int32)
    large = jnp.minimum(large, half - 1)
    return jnp.where(rel > 0, half, 0) + jnp.where(n < max_exact, n, large)


def _layer(x, p, w, table, bucket, layer, *, n_seq, seq_len, s0, kv_cache):
    row = lambda v: v.reshape(1, -1)
    x = _ffn(x, row(w['ffn1_pre']), row(w['ffn1_post']), w['ffn1_w_gate'], w['ffn1_w_up'], w['ffn1_w_down'])
    z = _inproj(x, row(w['mix_pre']), w['w_in'])
    if kv_cache is None:
        k_prev = v_prev = z
    else:
        k_prev = kv_cache[0].reshape(n_seq * WINDOW, N_KV * HEAD_DIM)
        v_prev = kv_cache[1].reshape(n_seq * WINDOW, N_KV * HEAD_DIM)
    attn = _attention(bucket, table, w['attn_sinks'], z, k_prev, v_prev,
                      n_seq=n_seq, seq_len=seq_len, prev_from_z=kv_cache is None)
    rec, s_new = _hgrn(w['hgrn_lb_logits'], row(w['hgrn_norm']), s0, z,
                       n_seq=n_seq, seq_len=seq_len, layer=layer)
    x = _outproj(x, attn, rec, w['w_out'], row(w['mix_post']))
    x = _ffn(x, row(w['ffn2_pre']), row(w['ffn2_post']), w['ffn2_w_gate'], w['ffn2_w_up'], w['ffn2_w_down'])
    x = _ple(x, p, row(w['ple_pre']), row(w['ple_post']), w['w_ple_gate'], w['w_ple_proj'])
    return x, z, s_new


def kernel(x_prompt, x_sample, cache_attn_k, cache_attn_v, state_hgrn, p_prompt, p_sample,
           rel_bias_table, ffn1_pre, ffn1_post, ffn1_w_gate, ffn1_w_up, ffn1_w_down,
           mix_pre, mix_post, w_in, w_out, attn_sinks, hgrn_lb_logits, hgrn_norm,
           ffn2_pre, ffn2_post, ffn2_w_gate, ffn2_w_up, ffn2_w_down,
           ple_pre, ple_post, w_ple_gate, w_ple_proj):
    depth = w_in.shape[0]
    bp, sp, _ = x_prompt.shape
    bs, ss, _ = x_sample.shape
    wc = cache_attn_k.shape[2]
    assert wc == WINDOW and ss == CHUNK and sp % ATTN_TQ == 0 and sp % HG_TT == 0

    rel = jnp.arange(LK)[None, :] - WINDOW - jnp.arange(CHUNK)[:, None]
    bucket = _t5_bucket(rel).astype(jnp.int32)

    kv_w = N_KV * HEAD_DIM
    k_lo, v_lo = D_ATTN, D_ATTN + kv_w
    yp = x_prompt.reshape(bp * sp, D_MODEL)
    ys = x_sample.reshape(bs * ss, D_MODEL)
    outs = [[] for _ in range(6)]
    for l in range(depth):
        w_in_l = w_in[l]
        w = {
            'ffn1_pre': ffn1_pre[l], 'ffn1_post': ffn1_post[l],
            'ffn1_w_gate': ffn1_w_gate[l].astype(BF16), 'ffn1_w_up': ffn1_w_up[l].astype(BF16),
            'ffn1_w_down': ffn1_w_down[l].astype(BF16),
            'mix_pre': mix_pre[l], 'mix_post': mix_post[l],
            'w_in': jnp.concatenate([w_in_l[:, :k_lo], w_in_l[:, v_lo + kv_w:], w_in_l[:, k_lo:v_lo + kv_w]],
                                    axis=1).astype(BF16),
            'w_out': w_out[l].astype(BF16), 'attn_sinks': attn_sinks[l],
            'hgrn_lb_logits': hgrn_lb_logits, 'hgrn_norm': hgrn_norm[l],
            'ffn2_pre': ffn2_pre[l], 'ffn2_post': ffn2_post[l],
            'ffn2_w_gate': ffn2_w_gate[l].astype(BF16), 'ffn2_w_up': ffn2_w_up[l].astype(BF16),
            'ffn2_w_down': ffn2_w_down[l].astype(BF16),
            'ple_pre': ple_pre[l], 'ple_post': ple_post[l],
            'w_ple_gate': w_ple_gate[l].astype(BF16), 'w_ple_proj': w_ple_proj[l].astype(BF16),
        }
        s0_prompt = jnp.zeros((bp, HG_HEADS, HG_DK, HG_DV), state_hgrn.dtype)
        yp, zp, st_p = _layer(yp, p_prompt[l].reshape(bp * sp, PLE_DIM), w, rel_bias_table, bucket, l,
                              n_seq=bp, seq_len=sp, s0=s0_prompt, kv_cache=None)
        ys, zs, st_s = _layer(ys, p_sample[l].reshape(bs * ss, PLE_DIM), w, rel_bias_table, bucket, l,
                              n_seq=bs, seq_len=ss, s0=state_hgrn[l],
                              kv_cache=(cache_attn_k[l], cache_attn_v[l]))
        keep = min(WINDOW, sp)
        zp3 = zp.reshape(bp, sp, D_IN)
        kz, vz = Z_K * kv_w, Z_V * kv_w
        outs[0].append(zp3[:, sp - keep:, kz:kz + kv_w].reshape(bp, keep, N_KV, HEAD_DIM))
        outs[1].append(zp3[:, sp - keep:, vz:vz + kv_w].reshape(bp, keep, N_KV, HEAD_DIM))
        outs[2].append(st_p)
        zs3 = zs.reshape(bs, ss, D_IN)
        k_new = zs3[:, :, kz:kz + kv_w].reshape(bs, ss, N_KV, HEAD_DIM)
        v_new = zs3[:, :, vz:vz + kv_w].reshape(bs, ss, N_KV, HEAD_DIM)
        outs[3].append(jnp.concatenate([cache_attn_k[l], k_new], axis=1)[:, ss:])
        outs[4].append(jnp.concatenate([cache_attn_v[l], v_new], axis=1)[:, ss:])
        outs[5].append(st_s)
    return (yp.reshape(bp, sp, D_MODEL), ys.reshape(bs, ss, D_MODEL),
            jnp.stack(outs[0]), jnp.stack(outs[1]), jnp.stack(outs[2]),
            jnp.stack(outs[3]), jnp.stack(outs[4]), jnp.stack(outs[5]))
```

```python
import functools
import math

import jax
import jax.numpy as jnp
import numpy as np
from jax import lax
from jax.experimental import pallas as pl
from jax.experimental.pallas import tpu as pltpu

F32 = jnp.float32
BF16 = jnp.bfloat16

D_MODEL = 2048
CHUNK = 64
D_ATTN = 1024
HEAD_DIM = 128
N_HEADS = 8
N_KV = 2
GROUP = N_HEADS // N_KV
WINDOW = 128
LK = WINDOW + CHUNK
NUM_BUCKETS = 32
MAX_DISTANCE = 128
HG_DK = 128
HG_DV = 128
HG_HEADS = 8
D_HG = HG_HEADS * HG_DV
HG_SUB = 16
HG_SUB_SHIFT = HG_SUB.bit_length() - 1
CHUNK_SHIFT = CHUNK.bit_length() - 1
D_FF = 5632
PLE_DIM = 256
EPS = 1e-6
NEG_INF = -1e30
D_IN = D_ATTN + 2 * N_KV * HEAD_DIM + 4 * D_HG

Z_QH, Z_FH, Z_IH, Z_GH = 1, 2, 3, 4
Z_K, Z_V = 20, 21

VMEM_LIMIT_BYTES = 60 * 1024 * 1024

FFN_TM = 1024
FFN_TF = 512
PROJ_TM = 1024
PROJ_TN = 512
ROW_TM = 512
SUB_ROWS = 256
NORM_ROWS = 256
ATTN_TQ = 256
HG_TT = 256


def _rms(x, gain):
    y = x * lax.rsqrt(jnp.mean(x * x, axis=-1, keepdims=True) + EPS)
    return y * gain


def _sigmoid(x):
    return 1.0 / (1.0 + jnp.exp(-x))


def _cparams(sem):
    return pltpu.CompilerParams(dimension_semantics=sem, vmem_limit_bytes=VMEM_LIMIT_BYTES)


def _ffn_kernel(x_ref, pre_ref, post_ref, wg_ref, wu_ref, wd_ref, o_ref, xn_ref, ms_ref):
    j = pl.program_id(1)
    tm = x_ref.shape[0]

    @pl.when(j == 0)
    def _():
        def body(r, c):
            rows = pl.ds(pl.multiple_of(r * NORM_ROWS, NORM_ROWS), NORM_ROWS)
            xn_ref[rows, :] = _rms(x_ref[rows, :], pre_ref[...]).astype(BF16)
            o_ref[rows, :] = jnp.zeros((NORM_ROWS, D_MODEL), F32)
            return c
        lax.fori_loop(0, tm // NORM_ROWS, body, 0)

    xn = xn_ref[...]
    g = jnp.dot(xn, wg_ref[...], preferred_element_type=F32)
    u = jnp.dot(xn, wu_ref[...], preferred_element_type=F32)
    h = (g * _sigmoid(g) * u).astype(BF16)
    o_ref[...] += jnp.dot(h, wd_ref[...], preferred_element_type=F32)

    @pl.when(j == pl.num_programs(1) - 1)
    def _():
        def sumsq(r, c):
            rows = pl.ds(pl.multiple_of(r * NORM_ROWS, NORM_ROWS), NORM_ROWS)
            y = o_ref[rows, :]
            ms_ref[rows, :] = jnp.mean(y * y, axis=-1, keepdims=True)
            return c
        lax.fori_loop(0, tm // NORM_ROWS, sumsq, 0)

        def apply(r, c):
            rows = pl.ds(pl.multiple_of(r * NORM_ROWS, NORM_ROWS), NORM_ROWS)
            y = o_ref[rows, :] * lax.rsqrt(ms_ref[rows, :] + EPS)
            o_ref[rows, :] = x_ref[rows, :] + 0.5 * (y * post_ref[...])
            return c
        lax.fori_loop(0, tm // NORM_ROWS, apply, 0)


def _ffn(x, pre, post, wg, wu, wd):
    t = x.shape[0]
    tm = min(FFN_TM, t)
    grid = (t // tm, D_FF // FFN_TF)
    return pl.pallas_call(
        _ffn_kernel,
        grid=grid,
        in_specs=[
            pl.BlockSpec((tm, D_MODEL), lambda i, j: (i, 0)),
            pl.BlockSpec((1, D_MODEL), lambda i, j: (0, 0)),
            pl.BlockSpec((1, D_MODEL), lambda i, j: (0, 0)),
            pl.BlockSpec((D_MODEL, FFN_TF), lambda i, j: (0, j)),
            pl.BlockSpec((D_MODEL, FFN_TF), lambda i, j: (0, j)),
            pl.BlockSpec((FFN_TF, D_MODEL), lambda i, j: (j, 0)),
        ],
        out_specs=pl.BlockSpec((tm, D_MODEL), lambda i, j: (i, 0)),
        out_shape=jax.ShapeDtypeStruct((t, D_MODEL), F32),
        scratch_shapes=[pltpu.VMEM((tm, D_MODEL), BF16), pltpu.VMEM((tm, 1), F32)],
        compiler_params=_cparams(("parallel", "arbitrary")),
        name="ffn_half_step",
    )(x, pre, post, wg, wu, wd)


def _inproj_kernel(x_ref, pre_ref, w_ref, z_ref, xn_ref):
    j = pl.program_id(1)
    tm = x_ref.shape[0]

    @pl.when(j == 0)
    def _():
        def body(r, c):
            rows = pl.ds(pl.multiple_of(r * NORM_ROWS, NORM_ROWS), NORM_ROWS)
            xn_ref[rows, :] = _rms(x_ref[rows, :], pre_ref[...]).astype(BF16)
            return c
        lax.fori_loop(0, tm // NORM_ROWS, body, 0)

    z_ref[...] = jnp.dot(xn_ref[...], w_ref[...], preferred_element_type=F32)


def _inproj(x, pre, w_in):
    t = x.shape[0]
    tm = min(PROJ_TM, t)
    grid = (t // tm, D_IN // PROJ_TN)
    return pl.pallas_call(
        _inproj_kernel,
        grid=grid,
        in_specs=[
            pl.BlockSpec((tm, D_MODEL), lambda i, j: (i, 0)),
            pl.BlockSpec((1, D_MODEL), lambda i, j: (0, 0)),
            pl.BlockSpec((D_MODEL, PROJ_TN), lambda i, j: (0, j)),
        ],
        out_specs=pl.BlockSpec((tm, PROJ_TN), lambda i, j: (i, j)),
        out_shape=jax.ShapeDtypeStruct((t, D_IN), F32),
        scratch_shapes=[pltpu.VMEM((tm, D_MODEL), BF16)],
        compiler_params=_cparams(("parallel", "arbitrary")),
        name="mixer_in_proj",
    )(x, pre, w_in)


def _attn_kernel(bucket_ref, table_ref, sinks_ref, q_ref, kc_ref, vc_ref, kp_ref, vp_ref,
                 o_ref, bias_ref, k_all, v_all, *, mask_history):
    b = pl.program_id(0)
    i = pl.program_id(1)
    tq = q_ref.shape[0]

    @pl.when((b == 0) & (i == 0))
    def _():
        bucket = bucket_ref[...]
        for h in range(N_HEADS):
            acc = jnp.zeros((CHUNK, LK), F32)
            for n in range(NUM_BUCKETS):
                acc = jnp.where(bucket == n, table_ref[n, h], acc)
            g, hh = divmod(h, GROUP)
            bias_ref[g, hh * CHUNK:(hh + 1) * CHUNK, :] = acc

    k_all[0:WINDOW, :] = kp_ref[...].astype(BF16)
    v_all[0:WINDOW, :] = vp_ref[...].astype(BF16)
    k_all[WINDOW:, :] = kc_ref[...].astype(BF16)
    v_all[WINDOW:, :] = vc_ref[...].astype(BF16)

    row = lax.broadcasted_iota(jnp.int32, (GROUP * CHUNK, 1), 0)
    col = lax.broadcasted_iota(jnp.int32, (GROUP * CHUNK, LK), 1)
    scale = HEAD_DIM ** -0.5

    for c in range(tq // CHUNK):
        for g in range(N_KV):
            heads = [g * GROUP + hh for hh in range(GROUP)]
            qs = jnp.concatenate(
                [q_ref[c * CHUNK:(c + 1) * CHUNK, h * HEAD_DIM:(h + 1) * HEAD_DIM] for h in heads],
                axis=0).astype(BF16)
            kk = k_all[c * CHUNK:c * CHUNK + LK, g * HEAD_DIM:(g + 1) * HEAD_DIM]
            vv = v_all[c * CHUNK:c * CHUNK + LK, g * HEAD_DIM:(g + 1) * HEAD_DIM]
            s = lax.dot_general(qs, kk, (((1,), (1,)), ((), ())), preferred_element_type=F32)
            s = s * scale + bias_ref[g]
            if mask_history and c * CHUNK < WINDOW:
                s = jnp.where(col + (i * tq + (c * CHUNK - WINDOW)) >= 0, s, NEG_INF)
            sink = jnp.zeros((GROUP * CHUNK, 1), F32)
            for hh, h in enumerate(heads):
                sink = jnp.where((row >> CHUNK_SHIFT) == hh, sinks_ref[h], sink)
            m = jnp.maximum(jnp.max(s, axis=-1, keepdims=True), sink)
            e = jnp.exp(s - m)
            p = e / (jnp.sum(e, axis=-1, keepdims=True) + jnp.exp(sink - m))
            o = jnp.dot(p.astype(BF16), vv, preferred_element_type=F32)
            for hh, h in enumerate(heads):
                o_ref[c * CHUNK:(c + 1) * CHUNK, h * HEAD_DIM:(h + 1) * HEAD_DIM] = (
                    o[hh * CHUNK:(hh + 1) * CHUNK, :].astype(BF16))


def _attention(bucket, table, sinks, z, k_prev, v_prev, *, n_seq, seq_len, prev_from_z):
    tq = min(ATTN_TQ, seq_len)
    nt = seq_len // tq
    if prev_from_z:
        per_seq, per_tile = seq_len // WINDOW, tq // WINDOW

        def kp_map(b, i):
            return (jnp.maximum(b * per_seq + i * per_tile - 1, 0), Z_K)

        def vp_map(b, i):
            return (jnp.maximum(b * per_seq + i * per_tile - 1, 0), Z_V)
    else:
        def kp_map(b, i):
            return (b, 0)
        vp_map = kp_map
    kv_w = N_KV * HEAD_DIM
    smem = pl.BlockSpec(memory_space=pltpu.SMEM)
    return pl.pallas_call(
        functools.partial(_attn_kernel, mask_history=prev_from_z),
        grid=(n_seq, nt),
        in_specs=[
            pl.BlockSpec((CHUNK, LK), lambda b, i: (0, 0)),
            smem,
            smem,
            pl.BlockSpec((tq, D_ATTN), lambda b, i: (b * nt + i, 0)),
            pl.BlockSpec((tq, kv_w), lambda b, i: (b * nt + i, Z_K)),
            pl.BlockSpec((tq, kv_w), lambda b, i: (b * nt + i, Z_V)),
            pl.BlockSpec((WINDOW, kv_w), kp_map),
            pl.BlockSpec((WINDOW, kv_w), vp_map),
        ],
        out_specs=pl.BlockSpec((tq, D_ATTN), lambda b, i: (b * nt + i, 0)),
        out_shape=jax.ShapeDtypeStruct((n_seq * seq_len, D_ATTN), BF16),
        scratch_shapes=[
            pltpu.VMEM((N_KV, GROUP * CHUNK, LK), F32),
            pltpu.VMEM((WINDOW + tq, kv_w), BF16),
            pltpu.VMEM((WINDOW + tq, kv_w), BF16),
        ],
        compiler_params=_cparams(("arbitrary", "arbitrary")),
        name="swa_attention",
    )(bucket, table, sinks, z, z, z, k_prev, v_prev)


def _hgrn_kernel(sel_ref, lbl_ref, gain_ref, s0_ref, q_ref, f_ref, i_ref, g_ref, rec_ref, sout_ref,
                 st_ref, *, layer):
    ti = pl.program_id(1)
    tt = q_ref.shape[0]
    nsub = CHUNK // HG_SUB

    @pl.when(ti == 0)
    def _():
        for h in range(HG_HEADS):
            st_ref[h] = s0_ref[0, h].T

    logits = lbl_ref[...]
    ex = jnp.exp(logits - jnp.max(logits, axis=0, keepdims=True))
    lb = jnp.sum(ex[:layer + 1], axis=0, keepdims=True) / jnp.sum(ex, axis=0, keepdims=True)

    sel = sel_ref[...]
    t_i = lax.broadcasted_iota(jnp.int32, (CHUNK, CHUNK), 0)
    s_i = lax.broadcasted_iota(jnp.int32, (CHUNK, CHUNK), 1)
    causal = s_i <= t_i
    zeros_sub = jnp.zeros((HG_SUB, HG_DK), BF16)

    def sub(a, j):
        return a[j * HG_SUB:(j + 1) * HG_SUB]

    def chunk_body(c, carry):
        rows = pl.ds(pl.multiple_of(c * CHUNK, CHUNK), CHUNK)
        f = lb + (1.0 - lb) * _sigmoid(f_ref[rows, :])
        lf = jnp.log(f)
        k = 1.0 - f
        hi = lf.astype(BF16)
        r1 = lf - hi.astype(F32)
        mid = r1.astype(BF16)
        lo = (r1 - mid.astype(F32)).astype(BF16)
        cr = (jnp.dot(sel, hi, preferred_element_type=F32)
              + jnp.dot(sel, mid, preferred_element_type=F32)
              + jnp.dot(sel, lo, preferred_element_type=F32))
        cc = cr[:CHUNK]
        ev = jnp.exp(cr[CHUNK:])
        qd = q_ref[rows, :] * jnp.exp(cc)
        ki = k * jnp.exp(-cc)
        iv = i_ref[rows, :].astype(BF16)
        gate = g_ref[rows, :]
        gate = gate * _sigmoid(gate)
        s1, s2, s3, s4 = HG_SUB, 2 * HG_SUB, 3 * HG_SUB, 4 * HG_SUB
        head_cols = [slice(h * HG_DK, (h + 1) * HG_DK) for h in range(HG_HEADS)]

        def dk(sl, lo_row, hi_row):
            n = _HG_RANGES.index((lo_row, hi_row))
            return ev[n:n + 1, sl]

        scores, inter, upds = [], [], []
        for sl in head_cols:
            qd_h, ki_h = qd[:, sl], ki[:, sl]
            qd_b = qd_h.astype(BF16)
            ki_b = [sub(ki_h, j).astype(BF16) for j in range(nsub)]
            kend = [sub(ki_h, 0) * dk(sl, 0, s1), sub(ki_h, 1) * dk(sl, s1, s2),
                    sub(ki_h, 2) * dk(sl, s2, s3), sub(ki_h, 3) * dk(sl, s3, s4)]
            kend_b = [kend[j].astype(BF16) for j in range(nsub - 1)]
            kmat = [
                [ki_b[0], zeros_sub, zeros_sub, zeros_sub],
                [kend_b[0], ki_b[1], zeros_sub, zeros_sub],
                [(kend[0] * dk(sl, s1, s2)).astype(BF16), kend_b[1], ki_b[2], zeros_sub],
                [(kend[0] * dk(sl, s1, s3)).astype(BF16), (kend[1] * dk(sl, s2, s3)).astype(BF16),
                 kend_b[2], ki_b[3]],
            ]
            scores.append([lax.dot_general(sub(qd_b, j), jnp.concatenate(kmat[j], axis=0),
                                           (((1,), (1,)), ((), ())), preferred_element_type=F32)
                           for j in range(nsub)])
            qe = jnp.concatenate([sub(qd_h, 0), sub(qd_h, 1) * dk(sl, 0, s1), sub(qd_h, 2) * dk(sl, 0, s2),
                                  sub(qd_h, 3) * dk(sl, 0, s3)], axis=0).astype(BF16)
            ke = jnp.concatenate([kend[0] * dk(sl, s1, s4), kend[1] * dk(sl, s2, s4),
                                  kend[2] * dk(sl, s3, s4), kend[3]], axis=0).astype(BF16)
            h = sl.start // HG_DK
            inter.append(lax.dot_general(qe, st_ref[h].astype(BF16), (((1,), (1,)), ((), ())),
                                         preferred_element_type=F32))
            upds.append(lax.dot_general(iv[:, sl], ke, (((0,), (0,)), ((), ())),
                                        preferred_element_type=F32))
        outs = []
        for h, sl in enumerate(head_cols):
            a = jnp.where(causal, jnp.concatenate(scores[h], axis=0), 0.0).astype(BF16)
            outs.append(inter[h] + jnp.dot(a, iv[:, sl], preferred_element_type=F32))
            st_ref[h] = st_ref[h] * dk(sl, 0, s4) + upds[h]
        for h, sl in enumerate(head_cols):
            rec_ref[rows, sl] = (_rms(outs[h], gain_ref[...]) * gate[:, sl]).astype(BF16)
        return carry

    lax.fori_loop(0, tt // CHUNK, chunk_body, 0, unroll=2)

    @pl.when(ti == pl.num_programs(1) - 1)
    def _():
        for h in range(HG_HEADS):
            sout_ref[0, h] = st_ref[h].T


_HG_RANGES = [(0, 16), (0, 32), (0, 48), (0, 64), (16, 32), (32, 48), (48, 64), (16, 48), (16, 64), (32, 64)]
_HG_RANGE_ROWS = 16


def _hgrn_selector():
    t = np.arange(CHUNK)
    within = (t[:, None] // HG_SUB == t[None, :] // HG_SUB) & (t[None, :] <= t[:, None])
    ranges = np.zeros((_HG_RANGE_ROWS, CHUNK), bool)
    for n, (lo, hi) in enumerate(_HG_RANGES):
        ranges[n, lo:hi] = True
    return jnp.asarray(np.concatenate([within, ranges], axis=0), BF16)


def _hgrn(lb_logits, gain, s0, z, *, n_seq, seq_len, layer):
    tt = min(HG_TT, seq_len)
    nt = seq_len // tt
    n_layers = lb_logits.shape[0]
    sel = _hgrn_selector()

    def zspec(col):
        return pl.BlockSpec((tt, D_HG), lambda b, i: (b * nt + i, col))

    state_spec = pl.BlockSpec((1, HG_HEADS, HG_DK, HG_DV), lambda b, i: (b, 0, 0, 0))
    return pl.pallas_call(
        functools.partial(_hgrn_kernel, layer=layer),
        grid=(n_seq, nt),
        in_specs=[
            pl.BlockSpec(sel.shape, lambda b, i: (0, 0)),
            pl.BlockSpec((n_layers, D_HG), lambda b, i: (0, 0)),
            pl.BlockSpec((1, HG_DV), lambda b, i: (0, 0)),
            state_spec,
            zspec(Z_QH), zspec(Z_FH), zspec(Z_IH), zspec(Z_GH),
        ],
        out_specs=[
            pl.BlockSpec((tt, D_HG), lambda b, i: (b * nt + i, 0)),
            state_spec,
        ],
        out_shape=[
            jax.ShapeDtypeStruct((n_seq * seq_len, D_HG), BF16),
            jax.ShapeDtypeStruct((n_seq, HG_HEADS, HG_DK, HG_DV), F32),
        ],
        scratch_shapes=[pltpu.VMEM((HG_HEADS, HG_DV, HG_DK), F32)],
        compiler_params=_cparams(("arbitrary", "arbitrary")),
        name="hgrn2_mixer",
    )(sel, lb_logits, gain, s0, z, z, z, z)


def _outproj_kernel(x_ref, a_ref, r_ref, w_ref, post_ref, o_ref):
    for r in range(x_ref.shape[0] // SUB_ROWS):
        rows = slice(r * SUB_ROWS, (r + 1) * SUB_ROWS)
        mix = (jnp.dot(a_ref[rows, :], w_ref[0:D_ATTN, :], preferred_element_type=F32)
               + jnp.dot(r_ref[rows, :], w_ref[D_ATTN:, :], preferred_element_type=F32))
        o_ref[rows, :] = x_ref[rows, :] + _rms(mix, post_ref[...])


def _outproj(x, attn, rec, w_out, post):
    t = x.shape[0]
    tm = min(ROW_TM, t)
    return pl.pallas_call(
        _outproj_kernel,
        grid=(t // tm,),
        in_specs=[
            pl.BlockSpec((tm, D_MODEL), lambda i: (i, 0)),
            pl.BlockSpec((tm, D_ATTN), lambda i: (i, 0)),
            pl.BlockSpec((tm, D_HG), lambda i: (i, 0)),
            pl.BlockSpec((D_MODEL, D_MODEL), lambda i: (0, 0)),
            pl.BlockSpec((1, D_MODEL), lambda i: (0, 0)),
        ],
        out_specs=pl.BlockSpec((tm, D_MODEL), lambda i: (i, 0)),
        out_shape=jax.ShapeDtypeStruct((t, D_MODEL), F32),
        compiler_params=_cparams(("parallel",)),
        name="mixer_out_proj",
    )(x, attn, rec, w_out, post)


def _ple_kernel(x_ref, p_ref, pre_ref, post_ref, wg_ref, wp_ref, o_ref):
    for r in range(x_ref.shape[0] // SUB_ROWS):
        rows = slice(r * SUB_ROWS, (r + 1) * SUB_ROWS)
        h = _rms(x_ref[rows, :], pre_ref[...]).astype(BF16)
        gate = _sigmoid(jnp.dot(h, wg_ref[...], preferred_element_type=F32))
        proj = jnp.dot(p_ref[rows, :].astype(BF16), wp_ref[...], preferred_element_type=F32)
        o_ref[rows, :] = x_ref[rows, :] + _rms(gate * proj, post_ref[...])


def _ple(x, p, pre, post, w_gate, w_proj):
    t = x.shape[0]
    tm = min(ROW_TM, t)
    return pl.pallas_call(
        _ple_kernel,
        grid=(t // tm,),
        in_specs=[
            pl.BlockSpec((tm, D_MODEL), lambda i: (i, 0)),
            pl.BlockSpec((tm, PLE_DIM), lambda i: (i, 0)),
            pl.BlockSpec((1, D_MODEL), lambda i: (0, 0)),
            pl.BlockSpec((1, D_MODEL), lambda i: (0, 0)),
            pl.BlockSpec((D_MODEL, D_MODEL), lambda i: (0, 0)),
            pl.BlockSpec((PLE_DIM, D_MODEL), lambda i: (0, 0)),
        ],
        out_specs=pl.BlockSpec((tm, D_MODEL), lambda i: (i, 0)),
        out_shape=jax.ShapeDtypeStruct((t, D_MODEL), F32),
        compiler_params=_cparams(("parallel",)),
        name="ple_embed",
    )(x, p, pre, post, w_gate, w_proj)


def _t5_bucket(rel):
    half = NUM_BUCKETS // 2
    max_exact = half // 2
    n = jnp.abs(rel)
    nf = jnp.maximum(n, 1).astype(jnp.float32)
    large = max_exact + (jnp.log(nf / max_exact) / math.log(MAX_DISTANCE / max_exact)
                         * (half - max_exact)).astype(jnp.int32)
    large = jnp.minimum(large, half - 1)
    return jnp.where(rel > 0, half, 0) + jnp.where(n < max_exact, n, large)


def _layer(x, p, w, table, bucket, layer, *, n_seq, seq_len, s0, kv_cache):
    row = lambda v: v.reshape(1, -1)
    x = _ffn(x, row(w['ffn1_pre']), row(w['ffn1_post']), w['ffn1_w_gate'], w['ffn1_w_up'], w['ffn1_w_down'])
    z = _inproj(x, row(w['mix_pre']), w['w_in'])
    if kv_cache is None:
        k_prev = v_prev = z
    else:
        k_prev = kv_cache[0].reshape(n_seq * WINDOW, N_KV * HEAD_DIM)
        v_prev = kv_cache[1].reshape(n_seq * WINDOW, N_KV * HEAD_DIM)
    attn = _attention(bucket, table, w['attn_sinks'], z, k_prev, v_prev,
                      n_seq=n_seq, seq_len=seq_len, prev_from_z=kv_cache is None)
    rec, s_new = _hgrn(w['hgrn_lb_logits'], row(w['hgrn_norm']), s0, z,
                       n_seq=n_seq, seq_len=seq_len, layer=layer)
    x = _outproj(x, attn, rec, w['w_out'], row(w['mix_post']))
    x = _ffn(x, row(w['ffn2_pre']), row(w['ffn2_post']), w['ffn2_w_gate'], w['ffn2_w_up'], w['ffn2_w_down'])
    x = _ple(x, p, row(w['ple_pre']), row(w['ple_post']), w['w_ple_gate'], w['w_ple_proj'])
    return x, z, s_new


def kernel(x_prompt, x_sample, cache_attn_k, cache_attn_v, state_hgrn, p_prompt, p_sample,
           rel_bias_table, ffn1_pre, ffn1_post, ffn1_w_gate, ffn1_w_up, ffn1_w_down,
           mix_pre, mix_post, w_in, w_out, attn_sinks, hgrn_lb_logits, hgrn_norm,
           ffn2_pre, ffn2_post, ffn2_w_gate, ffn2_w_up, ffn2_w_down,
           ple_pre, ple_post, w_ple_gate, w_ple_proj):
    depth = w_in.shape[0]
    bp, sp, _ = x_prompt.shape
    bs, ss, _ = x_sample.shape
    wc = cache_attn_k.shape[2]
    assert wc == WINDOW and ss == CHUNK and sp % ATTN_TQ == 0 and sp % HG_TT == 0

    rel = jnp.arange(LK)[None, :] - WINDOW - jnp.arange(CHUNK)[:, None]
    bucket = _t5_bucket(rel).astype(jnp.int32)

    kv_w = N_KV * HEAD_DIM
    k_lo, v_lo = D_ATTN, D_ATTN + kv_w
    yp = x_prompt.reshape(bp * sp, D_MODEL)
    ys = x_sample.reshape(bs * ss, D_MODEL)
    outs = [[] for _ in range(6)]
    for l in range(depth):
        w_in_l = w_in[l]
        w = {
            'ffn1_pre': ffn1_pre[l], 'ffn1_post': ffn1_post[l],
            'ffn1_w_gate': ffn1_w_gate[l].astype(BF16), 'ffn1_w_up': ffn1_w_up[l].astype(BF16),
            'ffn1_w_down': ffn1_w_down[l].astype(BF16),
            'mix_pre': mix_pre[l], 'mix_post': mix_post[l],
            'w_in': jnp.concatenate([w_in_l[:, :k_lo], w_in_l[:, v_lo + kv_w:], w_in_l[:, k_lo:v_lo + kv_w]],
                                    axis=1).astype(BF16),
            'w_out': w_out[l].astype(BF16), 'attn_sinks': attn_sinks[l],
            'hgrn_lb_logits': hgrn_lb_logits, 'hgrn_norm': hgrn_norm[l],
            'ffn2_pre': ffn2_pre[l], 'ffn2_post': ffn2_post[l],
            'ffn2_w_gate': ffn2_w_gate[l].astype(BF16), 'ffn2_w_up': ffn2_w_up[l].astype(BF16),
            'ffn2_w_down': ffn2_w_down[l].astype(BF16),
            'ple_pre': ple_pre[l], 'ple_post': ple_post[l],
            'w_ple_gate': w_ple_gate[l].astype(BF16), 'w_ple_proj': w_ple_proj[l].astype(BF16),
        }
        s0_prompt = jnp.zeros((bp, HG_HEADS, HG_DK, HG_DV), state_hgrn.dtype)
        yp, zp, st_p = _layer(yp, p_prompt[l].reshape(bp * sp, PLE_DIM), w, rel_bias_table, bucket, l,
                              n_seq=bp, seq_len=sp, s0=s0_prompt, kv_cache=None)
        ys, zs, st_s = _layer(ys, p_sample[l].reshape(bs * ss, PLE_DIM), w, rel_bias_table, bucket, l,
                              n_seq=bs, seq_len=ss, s0=state_hgrn[l],
                              kv_cache=(cache_attn_k[l], cache_attn_v[l]))
        keep = min(WINDOW, sp)
        zp3 = zp.reshape(bp, sp, D_IN)
        kz, vz = Z_K * kv_w, Z_V * kv_w
        outs[0].append(zp3[:, sp - keep:, kz:kz + kv_w].reshape(bp, keep, N_KV, HEAD_DIM))
        outs[1].append(zp3[:, sp - keep:, vz:vz + kv_w].reshape(bp, keep, N_KV, HEAD_DIM))
        outs[2].append(st_p)
        zs3 = zs.reshape(bs, ss, D_IN)
        k_new = zs3[:, :, kz:kz + kv_w].reshape(bs, ss, N_KV, HEAD_DIM)
        v_new = zs3[:, :, vz:vz + kv_w].reshape(bs, ss, N_KV, HEAD_DIM)
        outs[3].append(jnp.concatenate([cache_attn_k[l], k_new], axis=1)[:, ss:])
        outs[4].append(jnp.concatenate([cache_attn_v[l], v_new], axis=1)[:, ss:])
        outs[5].append(st_s)
    return (yp.reshape(bp, sp, D_MODEL), ys.reshape(bs, ss, D_MODEL),
            jnp.stack(outs[0]), jnp.stack(outs[1]), jnp.stack(outs[2]),
            jnp.stack(outs[3]), jnp.stack(outs[4]), jnp.stack(outs[5]))
```

```python
import functools
import math

import jax
import jax.numpy as jnp
import numpy as np
from jax import lax
from jax.experimental import pallas as pl
from jax.experimental.pallas import tpu as pltpu

F32 = jnp.float32
BF16 = jnp.bfloat16

D_MODEL = 2048
CHUNK = 64
D_ATTN = 1024
HEAD_DIM = 128
N_HEADS = 8
N_KV = 2
GROUP = N_HEADS // N_KV
WINDOW = 128
LK = WINDOW + CHUNK
NUM_BUCKETS = 32
MAX_DISTANCE = 128
HG_DK = 128
HG_DV = 128
HG_HEADS = 8
D_HG = HG_HEADS * HG_DV
HG_SUB = 16
CHUNK_SHIFT = CHUNK.bit_length() - 1
D_FF = 5632
PLE_DIM = 256
EPS = 1e-6
NEG_INF = -1e30
D_IN = D_ATTN + 2 * N_KV * HEAD_DIM + 4 * D_HG

Z_QH, Z_FH, Z_IH, Z_GH = 1, 2, 3, 4
Z_K, Z_V = 20, 21

VMEM_LIMIT_BYTES = 60 * 1024 * 1024

FFN_TM = 1024
FFN_TF = 512
PROJ_TM = 1024
PROJ_TN = 512
ROW_TM = 512
SUB_ROWS = 256
NORM_ROWS = 256
MIX_UNIT = 256


def _rms(x, gain):
    y = x * lax.rsqrt(jnp.mean(x * x, axis=-1, keepdims=True) + EPS)
    return y * gain


def _sigmoid(x):
    return 1.0 / (1.0 + jnp.exp(-x))


def _cparams(sem):
    return pltpu.CompilerParams(dimension_semantics=sem, vmem_limit_bytes=VMEM_LIMIT_BYTES)


def _ffn_kernel(x_ref, pre_ref, post_ref, wg_ref, wu_ref, wd_ref, o_ref, xn_ref, ms_ref):
    j = pl.program_id(1)
    tm = x_ref.shape[0]

    @pl.when(j == 0)
    def _():
        def body(r, c):
            rows = pl.ds(pl.multiple_of(r * NORM_ROWS, NORM_ROWS), NORM_ROWS)
            xn_ref[rows, :] = _rms(x_ref[rows, :], pre_ref[...]).astype(BF16)
            o_ref[rows, :] = jnp.zeros((NORM_ROWS, D_MODEL), F32)
            return c
        lax.fori_loop(0, tm // NORM_ROWS, body, 0)

    xn = xn_ref[...]
    g = jnp.dot(xn, wg_ref[...], preferred_element_type=F32)
    u = jnp.dot(xn, wu_ref[...], preferred_element_type=F32)
    h = (g * _sigmoid(g) * u).astype(BF16)
    o_ref[...] += jnp.dot(h, wd_ref[...], preferred_element_type=F32)

    @pl.when(j == pl.num_programs(1) - 1)
    def _():
        def sumsq(r, c):
            rows = pl.ds(pl.multiple_of(r * NORM_ROWS, NORM_ROWS), NORM_ROWS)
            y = o_ref[rows, :]
            ms_ref[rows, :] = jnp.mean(y * y, axis=-1, keepdims=True)
            return c
        lax.fori_loop(0, tm // NORM_ROWS, sumsq, 0)

        def apply(r, c):
            rows = pl.ds(pl.multiple_of(r * NORM_ROWS, NORM_ROWS), NORM_ROWS)
            y = o_ref[rows, :] * lax.rsqrt(ms_ref[rows, :] + EPS)
            o_ref[rows, :] = x_ref[rows, :] + 0.5 * (y * post_ref[...])
            return c
        lax.fori_loop(0, tm // NORM_ROWS, apply, 0)


def _ffn(x, pre, post, wg, wu, wd):
    t = x.shape[0]
    tm = min(FFN_TM, t)
    grid = (t // tm, D_FF // FFN_TF)
    return pl.pallas_call(
        _ffn_kernel,
        grid=grid,
        in_specs=[
            pl.BlockSpec((tm, D_MODEL), lambda i, j: (i, 0)),
            pl.BlockSpec((1, D_MODEL), lambda i, j: (0, 0)),
            pl.BlockSpec((1, D_MODEL), lambda i, j: (0, 0)),
            pl.BlockSpec((D_MODEL, FFN_TF), lambda i, j: (0, j)),
            pl.BlockSpec((D_MODEL, FFN_TF), lambda i, j: (0, j)),
            pl.BlockSpec((FFN_TF, D_MODEL), lambda i, j: (j, 0)),
        ],
        out_specs=pl.BlockSpec((tm, D_MODEL), lambda i, j: (i, 0)),
        out_shape=jax.ShapeDtypeStruct((t, D_MODEL), F32),
        scratch_shapes=[pltpu.VMEM((tm, D_MODEL), BF16), pltpu.VMEM((tm, 1), F32)],
        compiler_params=_cparams(("parallel", "arbitrary")),
        name="ffn_half_step",
    )(x, pre, post, wg, wu, wd)


def _inproj_kernel(x_ref, pre_ref, w_ref, z_ref, xn_ref):
    j = pl.program_id(1)
    tm = x_ref.shape[0]

    @pl.when(j == 0)
    def _():
        def body(r, c):
            rows = pl.ds(pl.multiple_of(r * NORM_ROWS, NORM_ROWS), NORM_ROWS)
            xn_ref[rows, :] = _rms(x_ref[rows, :], pre_ref[...]).astype(BF16)
            return c
        lax.fori_loop(0, tm // NORM_ROWS, body, 0)

    z_ref[...] = jnp.dot(xn_ref[...], w_ref[...], preferred_element_type=F32)


def _inproj(x, pre, w_in):
    t = x.shape[0]
    tm = min(PROJ_TM, t)
    grid = (t // tm, D_IN // PROJ_TN)
    return pl.pallas_call(
        _inproj_kernel,
        grid=grid,
        in_specs=[
            pl.BlockSpec((tm, D_MODEL), lambda i, j: (i, 0)),
            pl.BlockSpec((1, D_MODEL), lambda i, j: (0, 0)),
            pl.BlockSpec((D_MODEL, PROJ_TN), lambda i, j: (0, j)),
        ],
        out_specs=pl.BlockSpec((tm, PROJ_TN), lambda i, j: (i, j)),
        out_shape=jax.ShapeDtypeStruct((t, D_IN), F32),
        scratch_shapes=[pltpu.VMEM((tm, D_MODEL), BF16)],
        compiler_params=_cparams(("parallel", "arbitrary")),
        name="mixer_in_proj",
    )(x, pre, w_in)


def _fill_bias(bucket_ref, table_ref, bias_ref):
    bucket = bucket_ref[...]
    for h in range(N_HEADS):
        acc = jnp.zeros((CHUNK, LK), F32)
        for n in range(NUM_BUCKETS):
            acc = jnp.where(bucket == n, table_ref[n, h], acc)
        g, hh = divmod(h, GROUP)
        bias_ref[g, hh * CHUNK:(hh + 1) * CHUNK, :] = acc


def _sink_column(sinks_ref, g):
    row = lax.broadcasted_iota(jnp.int32, (GROUP * CHUNK, 1), 0)
    sink = jnp.zeros((GROUP * CHUNK, 1), F32)
    for hh in range(GROUP):
        sink = jnp.where((row >> CHUNK_SHIFT) == hh, sinks_ref[g * GROUP + hh], sink)
    return sink


def _attend_chunk(q_of_head, kk, vv, bias_g, sink, first_key_pos):
    qs = jnp.concatenate([q_of_head(hh) for hh in range(GROUP)], axis=0).astype(BF16)
    s = lax.dot_general(qs, kk, (((1,), (1,)), ((), ())), preferred_element_type=F32)
    s = s * (HEAD_DIM ** -0.5) + bias_g
    if first_key_pos is not None:
        col = lax.broadcasted_iota(jnp.int32, (GROUP * CHUNK, LK), 1)
        s = jnp.where(col + first_key_pos >= 0, s, NEG_INF)
    m = jnp.maximum(jnp.max(s, axis=-1, keepdims=True), sink)
    e = jnp.exp(s - m)
    p = e / (jnp.sum(e, axis=-1, keepdims=True) + jnp.exp(sink - m))
    return jnp.dot(p.astype(BF16), vv, preferred_element_type=F32)


def _attend_tile(q_ref, o_ref, k_all, v_all, bias_ref, sinks_ref, pos0):
    sinks = [_sink_column(sinks_ref, g) for g in range(N_KV)]
    for c in range(q_ref.shape[0] // CHUNK):
        rows = slice(c * CHUNK, (c + 1) * CHUNK)
        masked = pos0 is not None and c * CHUNK < WINDOW
        for g in range(N_KV):
            cols = slice(g * HEAD_DIM, (g + 1) * HEAD_DIM)

            def q_of_head(hh, g=g, rows=rows):
                h = g * GROUP + hh
                return q_ref[rows, h * HEAD_DIM:(h + 1) * HEAD_DIM]

            o = _attend_chunk(q_of_head, k_all[c * CHUNK:c * CHUNK + LK, cols],
                              v_all[c * CHUNK:c * CHUNK + LK, cols], bias_ref[g], sinks[g],
                              pos0 + (c * CHUNK - WINDOW) if masked else None)
            for hh in range(GROUP):
                h = g * GROUP + hh
                o_ref[rows, h * HEAD_DIM:(h + 1) * HEAD_DIM] = o[hh * CHUNK:(hh + 1) * CHUNK, :].astype(BF16)


def _attn_kernel(bucket_ref, table_ref, sinks_ref, q_ref, kc_ref, vc_ref, kp_ref, vp_ref,
                 o_ref, bias_ref, k_all, v_all):
    @pl.when(pl.program_id(0) == 0)
    def _():
        _fill_bias(bucket_ref, table_ref, bias_ref)

    k_all[0:WINDOW, :] = kp_ref[...].astype(BF16)
    v_all[0:WINDOW, :] = vp_ref[...].astype(BF16)
    k_all[WINDOW:, :] = kc_ref[...].astype(BF16)
    v_all[WINDOW:, :] = vc_ref[...].astype(BF16)
    _attend_tile(q_ref, o_ref, k_all, v_all, bias_ref, sinks_ref, None)


def _attention_cached(bucket, table, sinks, z, k_prev, v_prev, *, n_seq):
    kv_w = N_KV * HEAD_DIM
    smem = pl.BlockSpec(memory_space=pltpu.SMEM)
    return pl.pallas_call(
        _attn_kernel,
        grid=(n_seq,),
        in_specs=[
            pl.BlockSpec((CHUNK, LK), lambda b: (0, 0)),
            smem,
            smem,
            pl.BlockSpec((CHUNK, D_ATTN), lambda b: (b, 0)),
            pl.BlockSpec((CHUNK, kv_w), lambda b: (b, Z_K)),
            pl.BlockSpec((CHUNK, kv_w), lambda b: (b, Z_V)),
            pl.BlockSpec((WINDOW, kv_w), lambda b: (b, 0)),
            pl.BlockSpec((WINDOW, kv_w), lambda b: (b, 0)),
        ],
        out_specs=pl.BlockSpec((CHUNK, D_ATTN), lambda b: (b, 0)),
        out_shape=jax.ShapeDtypeStruct((n_seq * CHUNK, D_ATTN), BF16),
        scratch_shapes=[
            pltpu.VMEM((N_KV, GROUP * CHUNK, LK), F32),
            pltpu.VMEM((WINDOW + CHUNK, kv_w), BF16),
            pltpu.VMEM((WINDOW + CHUNK, kv_w), BF16),
        ],
        compiler_params=_cparams(("arbitrary",)),
        name="swa_attention",
    )(bucket, table, sinks, z, z, z, k_prev, v_prev)


_HG_RANGES = [(0, 16), (0, 32), (0, 48), (0, 64), (16, 32), (32, 48), (48, 64), (16, 48), (16, 64), (32, 64)]
_HG_RANGE_ROWS = 16


def _hgrn_selector():
    t = np.arange(CHUNK)
    within = (t[:, None] // HG_SUB == t[None, :] // HG_SUB) & (t[None, :] <= t[:, None])
    ranges = np.zeros((_HG_RANGE_ROWS, CHUNK), bool)
    for n, (lo, hi) in enumerate(_HG_RANGES):
        ranges[n, lo:hi] = True
    return jnp.asarray(np.concatenate([within, ranges], axis=0), BF16)


def _lower_bound(lbl_ref, layer):
    logits = lbl_ref[...]
    ex = jnp.exp(logits - jnp.max(logits, axis=0, keepdims=True))
    return jnp.sum(ex[:layer + 1], axis=0, keepdims=True) / jnp.sum(ex, axis=0, keepdims=True)


def _hgrn_chunk(q, fl, iv, gate, sel, lb, gain, st_ref):
    nsub = CHUNK // HG_SUB
    s1, s2, s3, s4 = HG_SUB, 2 * HG_SUB, 3 * HG_SUB, 4 * HG_SUB
    t_i = lax.broadcasted_iota(jnp.int32, (CHUNK, CHUNK), 0)
    s_i = lax.broadcasted_iota(jnp.int32, (CHUNK, CHUNK), 1)
    causal = s_i <= t_i
    zeros_sub = jnp.zeros((HG_SUB, HG_DK), BF16)

    def sub(a, j):
        return a[j * HG_SUB:(j + 1) * HG_SUB]

    f = lb + (1.0 - lb) * _sigmoid(fl)
    lf = jnp.log(f)
    k = 1.0 - f
    hi = lf.astype(BF16)
    r1 = lf - hi.astype(F32)
    mid = r1.astype(BF16)
    lo = (r1 - mid.astype(F32)).astype(BF16)
    cr = (jnp.dot(sel, hi, preferred_element_type=F32)
          + jnp.dot(sel, mid, preferred_element_type=F32)
          + jnp.dot(sel, lo, preferred_element_type=F32))
    cc = cr[:CHUNK]
    ev = jnp.exp(cr[CHUNK:])
    qd = q * jnp.exp(cc)
    ki = k * jnp.exp(-cc)
    iv = iv.astype(BF16)
    gate = gate * _sigmoid(gate)
    head_cols = [slice(h * HG_DK, (h + 1) * HG_DK) for h in range(HG_HEADS)]

    def dk(sl, lo_row, hi_row):
        n = _HG_RANGES.index((lo_row, hi_row))
        return ev[n:n + 1, sl]

    scores, inter, upds = [], [], []
    for h, sl in enumerate(head_cols):
        qd_h, ki_h = qd[:, sl], ki[:, sl]
        qd_b = qd_h.astype(BF16)
        ki_b = [sub(ki_h, j).astype(BF16) for j in range(nsub)]
        kend = [sub(ki_h, 0) * dk(sl, 0, s1), sub(ki_h, 1) * dk(sl, s1, s2),
                sub(ki_h, 2) * dk(sl, s2, s3), sub(ki_h, 3) * dk(sl, s3, s4)]
        kend_b = [kend[j].astype(BF16) for j in range(nsub - 1)]
        kmat = [
            [ki_b[0], zeros_sub, zeros_sub, zeros_sub],
            [kend_b[0], ki_b[1], zeros_sub, zeros_sub],
            [(kend[0] * dk(sl, s1, s2)).astype(BF16), kend_b[1], ki_b[2], zeros_sub],
            [(kend[0] * dk(sl, s1, s3)).astype(BF16), (kend[1] * dk(sl, s2, s3)).astype(BF16),
             kend_b[2], ki_b[3]],
        ]
        scores.append([lax.dot_general(sub(qd_b, j), jnp.concatenate(kmat[j], axis=0),
                                       (((1,), (1,)), ((), ())), preferred_element_type=F32)
                       for j in range(nsub)])
        qe = jnp.concatenate([sub(qd_h, 0), sub(qd_h, 1) * dk(sl, 0, s1), sub(qd_h, 2) * dk(sl, 0, s2),
                              sub(qd_h, 3) * dk(sl, 0, s3)], axis=0).astype(BF16)
        ke = jnp.concatenate([kend[0] * dk(sl, s1, s4), kend[1] * dk(sl, s2, s4),
                              kend[2] * dk(sl, s3, s4), kend[3]], axis=0).astype(BF16)
        inter.append(lax.dot_general(qe, st_ref[h].astype(BF16), (((1,), (1,)), ((), ())),
                                     preferred_element_type=F32))
        upds.append(lax.dot_general(iv[:, sl], ke, (((0,), (0,)), ((), ())),
                                    preferred_element_type=F32))
    outs = []
    for h, sl in enumerate(head_cols):
        a = jnp.where(causal, jnp.concatenate(scores[h], axis=0), 0.0).astype(BF16)
        outs.append(inter[h] + jnp.dot(a, iv[:, sl], preferred_element_type=F32))
        st_ref[h] = st_ref[h] * dk(sl, 0, s4) + upds[h]
    return [(_rms(outs[h], gain) * gate[:, sl]).astype(BF16) for h, sl in enumerate(head_cols)]


def _hgrn_kernel(sel_ref, lbl_ref, gain_ref, s0_ref, q_ref, f_ref, i_ref, g_ref, rec_ref, sout_ref,
                 st_ref, *, layer):
    for h in range(HG_HEADS):
        st_ref[h] = s0_ref[0, h].T
    outs = _hgrn_chunk(q_ref[...], f_ref[...], i_ref[...], g_ref[...], sel_ref[...],
                       _lower_bound(lbl_ref, layer), gain_ref[...], st_ref)
    for h in range(HG_HEADS):
        rec_ref[:, h * HG_DV:(h + 1) * HG_DV] = outs[h]
        sout_ref[0, h] = st_ref[h].T


def _hgrn_single_chunk(lb_logits, gain, s0, z, *, n_seq, layer):
    n_layers = lb_logits.shape[0]
    sel = _hgrn_selector()

    def zspec(col):
        return pl.BlockSpec((CHUNK, D_HG), lambda b: (b, col))

    state_spec = pl.BlockSpec((1, HG_HEADS, HG_DK, HG_DV), lambda b: (b, 0, 0, 0))
    return pl.pallas_call(
        functools.partial(_hgrn_kernel, layer=layer),
        grid=(n_seq,),
        in_specs=[
            pl.BlockSpec(sel.shape, lambda b: (0, 0)),
            pl.BlockSpec((n_layers, D_HG), lambda b: (0, 0)),
            pl.BlockSpec((1, HG_DV), lambda b: (0, 0)),
            state_spec,
            zspec(Z_QH), zspec(Z_FH), zspec(Z_IH), zspec(Z_GH),
        ],
        out_specs=[pl.BlockSpec((CHUNK, D_HG), lambda b: (b, 0)), state_spec],
        out_shape=[
            jax.ShapeDtypeStruct((n_seq * CHUNK, D_HG), BF16),
            jax.ShapeDtypeStruct((n_seq, HG_HEADS, HG_DK, HG_DV), F32),
        ],
        scratch_shapes=[pltpu.VMEM((HG_HEADS, HG_DV, HG_DK), F32)],
        compiler_params=_cparams(("arbitrary",)),
        name="hgrn2_mixer",
    )(sel, lb_logits, gain, s0, z, z, z, z)


def _mixer_kernel(bucket_ref, table_ref, sinks_ref, sel_ref, lbl_ref, gain_ref, pre_ref, x_ref, w_ref,
                  attn_ref, rec_ref, kv_ref, sout_ref,
                  bias_ref, z_even, z_odd, k_all, v_all, st_ref, *, layer, units_per_seq):
    s = pl.program_id(0)
    unit = MIX_UNIT
    kv_w = N_KV * HEAD_DIM
    u_in_seq = (s + (units_per_seq - 1)) % units_per_seq

    @pl.when(s == 0)
    def _():
        _fill_bias(bucket_ref, table_ref, bias_ref)
        z_odd[...] = jnp.zeros(z_odd.shape, F32)
        k_all[0:WINDOW, :] = jnp.zeros((WINDOW, kv_w), BF16)
        v_all[0:WINDOW, :] = jnp.zeros((WINDOW, kv_w), BF16)

    @pl.when((s == 0) | (u_in_seq == 0))
    def _():
        st_ref[...] = jnp.zeros(st_ref.shape, F32)

    def step(z_write, z_read):
        xn = _rms(x_ref[...], pre_ref[...]).astype(BF16)
        z_write[...] = jnp.dot(xn, w_ref[...], preferred_element_type=F32)
        k_cols = slice(Z_K * kv_w, (Z_K + 1) * kv_w)
        v_cols = slice(Z_V * kv_w, (Z_V + 1) * kv_w)
        k_all[WINDOW:, :] = z_read[:, k_cols].astype(BF16)
        v_all[WINDOW:, :] = z_read[:, v_cols].astype(BF16)
        kv_ref[:, 0:kv_w] = z_read[:, k_cols]
        kv_ref[:, kv_w:] = z_read[:, v_cols]
        _attend_tile(z_read, attn_ref, k_all, v_all, bias_ref, sinks_ref, u_in_seq * unit)
        k_all[0:WINDOW, :] = k_all[unit:unit + WINDOW, :]
        v_all[0:WINDOW, :] = v_all[unit:unit + WINDOW, :]
        lb = _lower_bound(lbl_ref, layer)
        for c in range(unit // CHUNK):
            rows = slice(c * CHUNK, (c + 1) * CHUNK)

            def zcols(col):
                return z_read[rows, col * D_HG:(col + 1) * D_HG]

            outs = _hgrn_chunk(zcols(Z_QH), zcols(Z_FH), zcols(Z_IH), zcols(Z_GH), sel_ref[...],
                               lb, gain_ref[...], st_ref)
            for h in range(HG_HEADS):
                rec_ref[rows, h * HG_DV:(h + 1) * HG_DV] = outs[h]

    @pl.when(s % 2 == 0)
    def _():
        step(z_even, z_odd)

    @pl.when(s % 2 == 1)
    def _():
        step(z_odd, z_even)

    @pl.when((s > 0) & (u_in_seq == units_per_seq - 1))
    def _():
        for h in range(HG_HEADS):
            sout_ref[0, h] = st_ref[h].T


def _mixer_front(x, pre, w_in, bucket, table, sinks, lb_logits, gain, *, n_seq, seq_len, layer):
    unit = MIX_UNIT
    ups = seq_len // unit
    n_units = n_seq * ups
    n_layers = lb_logits.shape[0]
    kv_w = N_KV * HEAD_DIM
    sel = _hgrn_selector()
    smem = pl.BlockSpec(memory_space=pltpu.SMEM)

    def const(shape):
        return pl.BlockSpec(shape, lambda s: (0,) * len(shape))

    def prev_unit(s):
        return (jnp.maximum(s - 1, 0), 0)

    return pl.pallas_call(
        functools.partial(_mixer_kernel, layer=layer, units_per_seq=ups),
        grid=(n_units + 1,),
        in_specs=[
            const((CHUNK, LK)), smem, smem, const(sel.shape), const((n_layers, D_HG)),
            const((1, HG_DV)), const((1, D_MODEL)),
            pl.BlockSpec((unit, D_MODEL), lambda s: (jnp.minimum(s, n_units - 1), 0)),
            pl.BlockSpec((D_MODEL, D_IN), lambda s: (0, 0), pipeline_mode=pl.Buffered(1)),
        ],
        out_specs=[
            pl.BlockSpec((unit, D_ATTN), prev_unit),
            pl.BlockSpec((unit, D_HG), prev_unit),
            pl.BlockSpec((unit, 2 * kv_w), prev_unit),
            pl.BlockSpec((1, HG_HEADS, HG_DK, HG_DV), lambda s: (jnp.maximum(s - 1, 0) // ups, 0, 0, 0)),
        ],
        out_shape=[
            jax.ShapeDtypeStruct((n_units * unit, D_ATTN), BF16),
            jax.ShapeDtypeStruct((n_units * unit, D_HG), BF16),
            jax.ShapeDtypeStruct((n_units * unit, 2 * kv_w), F32),
            jax.ShapeDtypeStruct((n_seq, HG_HEADS, HG_DK, HG_DV), F32),
        ],
        scratch_shapes=[
            pltpu.VMEM((N_KV, GROUP * CHUNK, LK), F32),
            pltpu.VMEM((unit, D_IN), F32),
            pltpu.VMEM((unit, D_IN), F32),
            pltpu.VMEM((WINDOW + unit, kv_w), BF16),
            pltpu.VMEM((WINDOW + unit, kv_w), BF16),
            pltpu.VMEM((HG_HEADS, HG_DV, HG_DK), F32),
        ],
        compiler_params=_cparams(("arbitrary",)),
        name="mixer_front",
    )(bucket, table, sinks, sel, lb_logits, gain, pre, x, w_in)


def _outproj_kernel(x_ref, a_ref, r_ref, w_ref, post_ref, o_ref):
    for r in range(x_ref.shape[0] // SUB_ROWS):
        rows = slice(r * SUB_ROWS, (r + 1) * SUB_ROWS)
        mix = (jnp.dot(a_ref[rows, :], w_ref[0:D_ATTN, :], preferred_element_type=F32)
               + jnp.dot(r_ref[rows, :], w_ref[D_ATTN:, :], preferred_element_type=F32))
        o_ref[rows, :] = x_ref[rows, :] + _rms(mix, post_ref[...])


def _outproj(x, attn, rec, w_out, post):
    t = x.shape[0]
    tm = min(ROW_TM, t)
    return pl.pallas_call(
        _outproj_kernel,
        grid=(t // tm,),
        in_specs=[
            pl.BlockSpec((tm, D_MODEL), lambda i: (i, 0)),
            pl.BlockSpec((tm, D_ATTN), lambda i: (i, 0)),
            pl.BlockSpec((tm, D_HG), lambda i: (i, 0)),
            pl.BlockSpec((D_MODEL, D_MODEL), lambda i: (0, 0)),
            pl.BlockSpec((1, D_MODEL), lambda i: (0, 0)),
        ],
        out_specs=pl.BlockSpec((tm, D_MODEL), lambda i: (i, 0)),
        out_shape=jax.ShapeDtypeStruct((t, D_MODEL), F32),
        compiler_params=_cparams(("parallel",)),
        name="mixer_out_proj",
    )(x, attn, rec, w_out, post)


def _ple_kernel(x_ref, p_ref, pre_ref, post_ref, wg_ref, wp_ref, o_ref):
    for r in range(x_ref.shape[0] // SUB_ROWS):
        rows = slice(r * SUB_ROWS, (r + 1) * SUB_ROWS)
        h = _rms(x_ref[rows, :], pre_ref[...]).astype(BF16)
        gate = _sigmoid(jnp.dot(h, wg_ref[...], preferred_element_type=F32))
        proj = jnp.dot(p_ref[rows, :].astype(BF16), wp_ref[...], preferred_element_type=F32)
        o_ref[rows, :] = x_ref[rows, :] + _rms(gate * proj, post_ref[...])


def _ple(x, p, pre, post, w_gate, w_proj):
    t = x.shape[0]
    tm = min(ROW_TM, t)
    return pl.pallas_call(
        _ple_kernel,
        grid=(t // tm,),
        in_specs=[
            pl.BlockSpec((tm, D_MODEL), lambda i: (i, 0)),
            pl.BlockSpec((tm, PLE_DIM), lambda i: (i, 0)),
            pl.BlockSpec((1, D_MODEL), lambda i: (0, 0)),
            pl.BlockSpec((1, D_MODEL), lambda i: (0, 0)),
            pl.BlockSpec((D_MODEL, D_MODEL), lambda i: (0, 0)),
            pl.BlockSpec((PLE_DIM, D_MODEL), lambda i: (0, 0)),
        ],
        out_specs=pl.BlockSpec((tm, D_MODEL), lambda i: (i, 0)),
        out_shape=jax.ShapeDtypeStruct((t, D_MODEL), F32),
        compiler_params=_cparams(("parallel",)),
        name="ple_embed",
    )(x, p, pre, post, w_gate, w_proj)


def _t5_bucket(rel):
    half = NUM_BUCKETS // 2
    max_exact = half // 2
    n = jnp.abs(rel)
    nf = jnp.maximum(n, 1).astype(jnp.float32)
    large = max_exact + (jnp.log(nf / max_exact) / math.log(MAX_DISTANCE / max_exact)
                         * (half - max_exact)).astype(jnp.int32)
    large = jnp.minimum(large, half - 1)
    return jnp.where(rel > 0, half, 0) + jnp.where(n < max_exact, n, large)


def _ffn1(x, w):
    row = lambda v: v.reshape(1, -1)
    return _ffn(x, row(w['ffn1_pre']), row(w['ffn1_post']), w['ffn1_w_gate'], w['ffn1_w_up'], w['ffn1_w_down'])


def _back_half(x, p, attn, rec, w):
    row = lambda v: v.reshape(1, -1)
    x = _outproj(x, attn, rec, w['w_out'], row(w['mix_post']))
    x = _ffn(x, row(w['ffn2_pre']), row(w['ffn2_post']), w['ffn2_w_gate'], w['ffn2_w_up'], w['ffn2_w_down'])
    return _ple(x, p, row(w['ple_pre']), row(w['ple_post']), w['w_ple_gate'], w['w_ple_proj'])


def _prompt_layer(x, p, w, table, bucket, layer, *, n_seq, seq_len):
    x = _ffn1(x, w)
    attn, rec, kv, s_new = _mixer_front(
        x, w['mix_pre'].reshape(1, -1), w['w_in'], bucket, table, w['attn_sinks'], w['hgrn_lb_logits'],
        w['hgrn_norm'].reshape(1, -1), n_seq=n_seq, seq_len=seq_len, layer=layer)
    return _back_half(x, p, attn, rec, w), kv, s_new


def _sample_layer(x, p, w, table, bucket, layer, *, n_seq, s0, k_cache, v_cache):
    kv_w = N_KV * HEAD_DIM
    x = _ffn1(x, w)
    z = _inproj(x, w['mix_pre'].reshape(1, -1), w['w_in'])
    attn = _attention_cached(bucket, table, w['attn_sinks'], z, k_cache.reshape(n_seq * WINDOW, kv_w),
                             v_cache.reshape(n_seq * WINDOW, kv_w), n_seq=n_seq)
    rec, s_new = _hgrn_single_chunk(w['hgrn_lb_logits'], w['hgrn_norm'].reshape(1, -1), s0, z,
                                    n_seq=n_seq, layer=layer)
    return _back_half(x, p, attn, rec, w), z[:, Z_K * kv_w:], s_new


def kernel(x_prompt, x_sample, cache_attn_k, cache_attn_v, state_hgrn, p_prompt, p_sample,
           rel_bias_table, ffn1_pre, ffn1_post, ffn1_w_gate, ffn1_w_up, ffn1_w_down,
           mix_pre, mix_post, w_in, w_out, attn_sinks, hgrn_lb_logits, hgrn_norm,
           ffn2_pre, ffn2_post, ffn2_w_gate, ffn2_w_up, ffn2_w_down,
           ple_pre, ple_post, w_ple_gate, w_ple_proj):
    depth = w_in.shape[0]
    bp, sp, _ = x_prompt.shape
    bs, ss, _ = x_sample.shape
    wc = cache_attn_k.shape[2]
    assert wc == WINDOW and ss == CHUNK and sp % MIX_UNIT == 0 and sp >= WINDOW

    rel = jnp.arange(LK)[None, :] - WINDOW - jnp.arange(CHUNK)[:, None]
    bucket = _t5_bucket(rel).astype(jnp.int32)

    kv_w = N_KV * HEAD_DIM
    k_lo, v_lo = D_ATTN, D_ATTN + kv_w
    yp = x_prompt.reshape(bp * sp, D_MODEL)
    ys = x_sample.reshape(bs * ss, D_MODEL)
    outs = [[] for _ in range(6)]
    for l in range(depth):
        w_in_l = w_in[l]
        w = {
            'ffn1_pre': ffn1_pre[l], 'ffn1_post': ffn1_post[l],
            'ffn1_w_gate': ffn1_w_gate[l].astype(BF16), 'ffn1_w_up': ffn1_w_up[l].astype(BF16),
            'ffn1_w_down': ffn1_w_down[l].astype(BF16),
            'mix_pre': mix_pre[l], 'mix_post': mix_post[l],
            'w_in': jnp.concatenate([w_in_l[:, :k_lo], w_in_l[:, v_lo + kv_w:], w_in_l[:, k_lo:v_lo + kv_w]],
                                    axis=1).astype(BF16),
            'w_out': w_out[l].astype(BF16), 'attn_sinks': attn_sinks[l],
            'hgrn_lb_logits': hgrn_lb_logits, 'hgrn_norm': hgrn_norm[l],
            'ffn2_pre': ffn2_pre[l], 'ffn2_post': ffn2_post[l],
            'ffn2_w_gate': ffn2_w_gate[l].astype(BF16), 'ffn2_w_up': ffn2_w_up[l].astype(BF16),
            'ffn2_w_down': ffn2_w_down[l].astype(BF16),
            'ple_pre': ple_pre[l], 'ple_post': ple_post[l],
            'w_ple_gate': w_ple_gate[l].astype(BF16), 'w_ple_proj': w_ple_proj[l].astype(BF16),
        }
        yp, kv_p, st_p = _prompt_layer(yp, p_prompt[l].reshape(bp * sp, PLE_DIM), w, rel_bias_table, bucket, l,
                                       n_seq=bp, seq_len=sp)
        ys, kv_s, st_s = _sample_layer(ys, p_sample[l].reshape(bs * ss, PLE_DIM), w, rel_bias_table, bucket, l,
                                       n_seq=bs, s0=state_hgrn[l], k_cache=cache_attn_k[l],
                                       v_cache=cache_attn_v[l])
        kv_p = kv_p.reshape(bp, sp, 2, N_KV, HEAD_DIM)[:, sp - WINDOW:]
        outs[0].append(kv_p[:, :, 0])
        outs[1].append(kv_p[:, :, 1])
        outs[2].append(st_p)
        kv_s = kv_s.reshape(bs, ss, 2, N_KV, HEAD_DIM)
        outs[3].append(jnp.concatenate([cache_attn_k[l], kv_s[:, :, 0]], axis=1)[:, ss:])
        outs[4].append(jnp.concatenate([cache_attn_v[l], kv_s[:, :, 1]], axis=1)[:, ss:])
        outs[5].append(st_s)
    return (yp.reshape(bp, sp, D_MODEL), ys.reshape(bs, ss, D_MODEL),
            jnp.stack(outs[0]), jnp.stack(outs[1]), jnp.stack(outs[2]),
            jnp.stack(outs[3]), jnp.stack(outs[4]), jnp.stack(outs[5]))
```

```python
import functools
import math

import jax
import jax.numpy as jnp
import numpy as np
from jax import lax
from jax.experimental import pallas as pl
from jax.experimental.pallas import tpu as pltpu

F32 = jnp.float32
BF16 = jnp.bfloat16

D_MODEL = 2048
CHUNK = 64
D_ATTN = 1024
HEAD_DIM = 128
N_HEADS = 8
N_KV = 2
GROUP = N_HEADS // N_KV
WINDOW = 128
LK = WINDOW + CHUNK
NUM_BUCKETS = 32
MAX_DISTANCE = 128
HG_DK = 128
HG_DV = 128
HG_HEADS = 8
D_HG = HG_HEADS * HG_DV
HG_SUB = 16
CHUNK_SHIFT = CHUNK.bit_length() - 1
D_FF = 5632
PLE_DIM = 256
EPS = 1e-6
NEG_INF = -1e30
D_IN = D_ATTN + 2 * N_KV * HEAD_DIM + 4 * D_HG

Z_QH, Z_FH, Z_IH, Z_GH = 1, 2, 3, 4
Z_K, Z_V = 20, 21

VMEM_LIMIT_BYTES = 60 * 1024 * 1024

FFN_TM = 1024
FFN_TF = 512
PROJ_TM = 1024
PROJ_TN = 512
ROW_TM = 1024
SUB_ROWS = 256
NORM_ROWS = 256
MIX_UNIT = 256


def _rms(x, gain):
    y = x * lax.rsqrt(jnp.mean(x * x, axis=-1, keepdims=True) + EPS)
    return y * gain


def _sigmoid(x):
    return 1.0 / (1.0 + jnp.exp(-x))


def _cparams(sem):
    return pltpu.CompilerParams(dimension_semantics=sem, vmem_limit_bytes=VMEM_LIMIT_BYTES)


def _pipelined(n, before, matmul, after):
    if before is not None:
        before(0)
    for r in range(n):
        if before is not None and r + 1 < n:
            before(r + 1)
        matmul(r)
        if after is not None and r > 0:
            after(r - 1)
    if after is not None:
        after(n - 1)


def _ffn_kernel(x_ref, pre_ref, post_ref, wg_ref, wu_ref, wd_ref, o_ref, xn_ref):
    j = pl.program_id(1)
    last = pl.num_programs(1) - 1
    nsub = x_ref.shape[0] // SUB_ROWS
    sub = [slice(r * SUB_ROWS, (r + 1) * SUB_ROWS) for r in range(nsub)]

    def swiglu(rows):
        xn = xn_ref[rows, :]
        g = jnp.dot(xn, wg_ref[...], preferred_element_type=F32)
        u = jnp.dot(xn, wu_ref[...], preferred_element_type=F32)
        h = (g * _sigmoid(g) * u).astype(BF16)
        return jnp.dot(h, wd_ref[...], preferred_element_type=F32)

    @pl.when(j == 0)
    def _():
        def norm(r):
            xn_ref[sub[r], :] = _rms(x_ref[sub[r], :], pre_ref[...]).astype(BF16)

        def first(r):
            o_ref[sub[r], :] = swiglu(sub[r])

        _pipelined(nsub, norm, first, None)

    @pl.when((j > 0) & (j < last))
    def _():
        o_ref[...] += swiglu(slice(None))

    @pl.when(j == last)
    def _():
        ys = [None] * nsub

        def final(r):
            ys[r] = o_ref[sub[r], :] + swiglu(sub[r])

        def finish(r):
            o_ref[sub[r], :] = x_ref[sub[r], :] + 0.5 * _rms(ys[r], post_ref[...])

        _pipelined(nsub, None, final, finish)


def _ffn(x, pre, post, wg, wu, wd):
    t = x.shape[0]
    tm = min(FFN_TM, t)
    grid = (t // tm, D_FF // FFN_TF)
    assert grid[1] >= 3
    return pl.pallas_call(
        _ffn_kernel,
        grid=grid,
        in_specs=[
            pl.BlockSpec((tm, D_MODEL), lambda i, j: (i, 0)),
            pl.BlockSpec((1, D_MODEL), lambda i, j: (0, 0)),
            pl.BlockSpec((1, D_MODEL), lambda i, j: (0, 0)),
            pl.BlockSpec((D_MODEL, FFN_TF), lambda i, j: (0, j)),
            pl.BlockSpec((D_MODEL, FFN_TF), lambda i, j: (0, j)),
            pl.BlockSpec((FFN_TF, D_MODEL), lambda i, j: (j, 0)),
        ],
        out_specs=pl.BlockSpec((tm, D_MODEL), lambda i, j: (i, 0)),
        out_shape=jax.ShapeDtypeStruct((t, D_MODEL), F32),
        scratch_shapes=[pltpu.VMEM((tm, D_MODEL), BF16)],
        compiler_params=_cparams(("parallel", "arbitrary")),
        name="ffn_half_step",
    )(x, pre, post, wg, wu, wd)


def _inproj_kernel(x_ref, pre_ref, w_ref, z_ref, xn_ref):
    j = pl.program_id(1)
    tm = x_ref.shape[0]

    @pl.when(j == 0)
    def _():
        def body(r, c):
            rows = pl.ds(pl.multiple_of(r * NORM_ROWS, NORM_ROWS), NORM_ROWS)
            xn_ref[rows, :] = _rms(x_ref[rows, :], pre_ref[...]).astype(BF16)
            return c
        lax.fori_loop(0, tm // NORM_ROWS, body, 0)

    z_ref[...] = jnp.dot(xn_ref[...], w_ref[...], preferred_element_type=F32)


def _inproj(x, pre, w_in):
    t = x.shape[0]
    tm = min(PROJ_TM, t)
    grid = (t // tm, D_IN // PROJ_TN)
    return pl.pallas_call(
        _inproj_kernel,
        grid=grid,
        in_specs=[
            pl.BlockSpec((tm, D_MODEL), lambda i, j: (i, 0)),
            pl.BlockSpec((1, D_MODEL), lambda i, j: (0, 0)),
            pl.BlockSpec((D_MODEL, PROJ_TN), lambda i, j: (0, j)),
        ],
        out_specs=pl.BlockSpec((tm, PROJ_TN), lambda i, j: (i, j)),
        out_shape=jax.ShapeDtypeStruct((t, D_IN), F32),
        scratch_shapes=[pltpu.VMEM((tm, D_MODEL), BF16)],
        compiler_params=_cparams(("parallel", "arbitrary")),
        name="mixer_in_proj",
    )(x, pre, w_in)


def _fill_bias(bucket_ref, table_ref, bias_ref):
    bucket = bucket_ref[...]
    for h in range(N_HEADS):
        acc = jnp.zeros((CHUNK, LK), F32)
        for n in range(NUM_BUCKETS):
            acc = jnp.where(bucket == n, table_ref[n, h], acc)
        g, hh = divmod(h, GROUP)
        bias_ref[g, hh * CHUNK:(hh + 1) * CHUNK, :] = acc


def _sink_column(sinks_ref, g):
    row = lax.broadcasted_iota(jnp.int32, (GROUP * CHUNK, 1), 0)
    sink = jnp.zeros((GROUP * CHUNK, 1), F32)
    for hh in range(GROUP):
        sink = jnp.where((row >> CHUNK_SHIFT) == hh, sinks_ref[g * GROUP + hh], sink)
    return sink


def _attend_chunk(q_of_head, kk, vv, bias_g, sink, first_key_pos):
    qs = jnp.concatenate([q_of_head(hh) for hh in range(GROUP)], axis=0).astype(BF16)
    s = lax.dot_general(qs, kk, (((1,), (1,)), ((), ())), preferred_element_type=F32)
    s = s * (HEAD_DIM ** -0.5) + bias_g
    if first_key_pos is not None:
        col = lax.broadcasted_iota(jnp.int32, (GROUP * CHUNK, LK), 1)
        s = jnp.where(col + first_key_pos >= 0, s, NEG_INF)
    m = jnp.maximum(jnp.max(s, axis=-1, keepdims=True), sink)
    e = jnp.exp(s - m)
    p = e / (jnp.sum(e, axis=-1, keepdims=True) + jnp.exp(sink - m))
    return jnp.dot(p.astype(BF16), vv, preferred_element_type=F32)


def _attend_tile(q_ref, o_ref, k_all, v_all, bias_ref, sinks_ref, pos0):
    sinks = [_sink_column(sinks_ref, g) for g in range(N_KV)]
    for c in range(q_ref.shape[0] // CHUNK):
        rows = slice(c * CHUNK, (c + 1) * CHUNK)
        masked = pos0 is not None and c * CHUNK < WINDOW
        for g in range(N_KV):
            cols = slice(g * HEAD_DIM, (g + 1) * HEAD_DIM)

            def q_of_head(hh, g=g, rows=rows):
                h = g * GROUP + hh
                return q_ref[rows, h * HEAD_DIM:(h + 1) * HEAD_DIM]

            o = _attend_chunk(q_of_head, k_all[c * CHUNK:c * CHUNK + LK, cols],
                              v_all[c * CHUNK:c * CHUNK + LK, cols], bias_ref[g], sinks[g],
                              pos0 + (c * CHUNK - WINDOW) if masked else None)
            for hh in range(GROUP):
                h = g * GROUP + hh
                o_ref[rows, h * HEAD_DIM:(h + 1) * HEAD_DIM] = o[hh * CHUNK:(hh + 1) * CHUNK, :].astype(BF16)


def _attn_kernel(bucket_ref, table_ref, sinks_ref, q_ref, kc_ref, vc_ref, kp_ref, vp_ref,
                 o_ref, bias_ref, k_all, v_all):
    @pl.when(pl.program_id(0) == 0)
    def _():
        _fill_bias(bucket_ref, table_ref, bias_ref)

    k_all[0:WINDOW, :] = kp_ref[...].astype(BF16)
    v_all[0:WINDOW, :] = vp_ref[...].astype(BF16)
    k_all[WINDOW:, :] = kc_ref[...].astype(BF16)
    v_all[WINDOW:, :] = vc_ref[...].astype(BF16)
    _attend_tile(q_ref, o_ref, k_all, v_all, bias_ref, sinks_ref, None)


def _attention_cached(bucket, table, sinks, z, k_prev, v_prev, *, n_seq):
    kv_w = N_KV * HEAD_DIM
    smem = pl.BlockSpec(memory_space=pltpu.SMEM)
    return pl.pallas_call(
        _attn_kernel,
        grid=(n_seq,),
        in_specs=[
            pl.BlockSpec((CHUNK, LK), lambda b: (0, 0)),
            smem,
            smem,
            pl.BlockSpec((CHUNK, D_ATTN), lambda b: (b, 0)),
            pl.BlockSpec((CHUNK, kv_w), lambda b: (b, Z_K)),
            pl.BlockSpec((CHUNK, kv_w), lambda b: (b, Z_V)),
            pl.BlockSpec((WINDOW, kv_w), lambda b: (b, 0)),
            pl.BlockSpec((WINDOW, kv_w), lambda b: (b, 0)),
        ],
        out_specs=pl.BlockSpec((CHUNK, D_ATTN), lambda b: (b, 0)),
        out_shape=jax.ShapeDtypeStruct((n_seq * CHUNK, D_ATTN), BF16),
        scratch_shapes=[
            pltpu.VMEM((N_KV, GROUP * CHUNK, LK), F32),
            pltpu.VMEM((WINDOW + CHUNK, kv_w), BF16),
            pltpu.VMEM((WINDOW + CHUNK, kv_w), BF16),
        ],
        compiler_params=_cparams(("arbitrary",)),
        name="swa_attention",
    )(bucket, table, sinks, z, z, z, k_prev, v_prev)


_HG_RANGES = [(0, 16), (0, 32), (0, 48), (0, 64), (16, 32), (32, 48), (48, 64), (16, 48), (16, 64), (32, 64)]
_HG_RANGE_ROWS = 16


def _hgrn_selector():
    t = np.arange(CHUNK)
    within = (t[:, None] // HG_SUB == t[None, :] // HG_SUB) & (t[None, :] <= t[:, None])
    ranges = np.zeros((_HG_RANGE_ROWS, CHUNK), bool)
    for n, (lo, hi) in enumerate(_HG_RANGES):
        ranges[n, lo:hi] = True
    return jnp.asarray(np.concatenate([within, ranges], axis=0), BF16)


def _lower_bound(lbl_ref, layer):
    logits = lbl_ref[...]
    ex = jnp.exp(logits - jnp.max(logits, axis=0, keepdims=True))
    return jnp.sum(ex[:layer + 1], axis=0, keepdims=True) / jnp.sum(ex, axis=0, keepdims=True)


def _hgrn_chunk(q, fl, iv, gate, sel, lb, gain, st_ref):
    nsub = CHUNK // HG_SUB
    s1, s2, s3, s4 = HG_SUB, 2 * HG_SUB, 3 * HG_SUB, 4 * HG_SUB
    t_i = lax.broadcasted_iota(jnp.int32, (CHUNK, CHUNK), 0)
    s_i = lax.broadcasted_iota(jnp.int32, (CHUNK, CHUNK), 1)
    causal = s_i <= t_i
    zeros_sub = jnp.zeros((HG_SUB, HG_DK), BF16)

    def sub(a, j):
        return a[j * HG_SUB:(j + 1) * HG_SUB]

    f = lb + (1.0 - lb) * _sigmoid(fl)
    lf = jnp.log(f)
    k = 1.0 - f
    hi = lf.astype(BF16)
    r1 = lf - hi.astype(F32)
    mid = r1.astype(BF16)
    lo = (r1 - mid.astype(F32)).astype(BF16)
    cr = (jnp.dot(sel, hi, preferred_element_type=F32)
          + jnp.dot(sel, mid, preferred_element_type=F32)
          + jnp.dot(sel, lo, preferred_element_type=F32))
    cc = cr[:CHUNK]
    ev = jnp.exp(cr[CHUNK:])
    qd = q * jnp.exp(cc)
    ki = k * jnp.exp(-cc)
    iv = iv.astype(BF16)
    gate = gate * _sigmoid(gate)
    head_cols = [slice(h * HG_DK, (h + 1) * HG_DK) for h in range(HG_HEADS)]

    def dk(sl, lo_row, hi_row):
        n = _HG_RANGES.index((lo_row, hi_row))
        return ev[n:n + 1, sl]

    scores, inter, upds = [], [], []
    for h, sl in enumerate(head_cols):
        qd_h, ki_h = qd[:, sl], ki[:, sl]
        qd_b = qd_h.astype(BF16)
        ki_b = [sub(ki_h, j).astype(BF16) for j in range(nsub)]
        kend = [sub(ki_h, 0) * dk(sl, 0, s1), sub(ki_h, 1) * dk(sl, s1, s2),
                sub(ki_h, 2) * dk(sl, s2, s3), sub(ki_h, 3) * dk(sl, s3, s4)]
        kend_b = [kend[j].astype(BF16) for j in range(nsub - 1)]
        kmat = [
            [ki_b[0], zeros_sub, zeros_sub, zeros_sub],
            [kend_b[0], ki_b[1], zeros_sub, zeros_sub],
            [(kend[0] * dk(sl, s1, s2)).astype(BF16), kend_b[1], ki_b[2], zeros_sub],
            [(kend[0] * dk(sl, s1, s3)).astype(BF16), (kend[1] * dk(sl, s2, s3)).astype(BF16),
             kend_b[2], ki_b[3]],
        ]
        scores.append([lax.dot_general(sub(qd_b, j), jnp.concatenate(kmat[j], axis=0),
                                       (((1,), (1,)), ((), ())), preferred_element_type=F32)
                       for j in range(nsub)])
        qe = jnp.concatenate([sub(qd_h, 0), sub(qd_h, 1) * dk(sl, 0, s1), sub(qd_h, 2) * dk(sl, 0, s2),
                              sub(qd_h, 3) * dk(sl, 0, s3)], axis=0).astype(BF16)
        ke = jnp.concatenate([kend[0] * dk(sl, s1, s4), kend[1] * dk(sl, s2, s4),
                              kend[2] * dk(sl, s3, s4), kend[3]], axis=0).astype(BF16)
        inter.append(lax.dot_general(qe, st_ref[h].astype(BF16), (((1,), (1,)), ((), ())),
                                     preferred_element_type=F32))
        upds.append(lax.dot_general(iv[:, sl], ke, (((0,), (0,)), ((), ())),
                                    preferred_element_type=F32))
    outs = []
    for h, sl in enumerate(head_cols):
        a = jnp.where(causal, jnp.concatenate(scores[h], axis=0), 0.0).astype(BF16)
        outs.append(inter[h] + jnp.dot(a, iv[:, sl], preferred_element_type=F32))
        st_ref[h] = st_ref[h] * dk(sl, 0, s4) + upds[h]
    return [(_rms(outs[h], gain) * gate[:, sl]).astype(BF16) for h, sl in enumerate(head_cols)]


def _hgrn_kernel(sel_ref, lbl_ref, gain_ref, s0_ref, q_ref, f_ref, i_ref, g_ref, rec_ref, sout_ref,
                 st_ref, *, layer):
    for h in range(HG_HEADS):
        st_ref[h] = s0_ref[0, h].T
    outs = _hgrn_chunk(q_ref[...], f_ref[...], i_ref[...], g_ref[...], sel_ref[...],
                       _lower_bound(lbl_ref, layer), gain_ref[...], st_ref)
    for h in range(HG_HEADS):
        rec_ref[:, h * HG_DV:(h + 1) * HG_DV] = outs[h]
        sout_ref[0, h] = st_ref[h].T


def _hgrn_single_chunk(lb_logits, gain, s0, z, *, n_seq, layer):
    n_layers = lb_logits.shape[0]
    sel = _hgrn_selector()

    def zspec(col):
        return pl.BlockSpec((CHUNK, D_HG), lambda b: (b, col))

    state_spec = pl.BlockSpec((1, HG_HEADS, HG_DK, HG_DV), lambda b: (b, 0, 0, 0))
    return pl.pallas_call(
        functools.partial(_hgrn_kernel, layer=layer),
        grid=(n_seq,),
        in_specs=[
            pl.BlockSpec(sel.shape, lambda b: (0, 0)),
            pl.BlockSpec((n_layers, D_HG), lambda b: (0, 0)),
            pl.BlockSpec((1, HG_DV), lambda b: (0, 0)),
            state_spec,
            zspec(Z_QH), zspec(Z_FH), zspec(Z_IH), zspec(Z_GH),
        ],
        out_specs=[pl.BlockSpec((CHUNK, D_HG), lambda b: (b, 0)), state_spec],
        out_shape=[
            jax.ShapeDtypeStruct((n_seq * CHUNK, D_HG), BF16),
            jax.ShapeDtypeStruct((n_seq, HG_HEADS, HG_DK, HG_DV), F32),
        ],
        scratch_shapes=[pltpu.VMEM((HG_HEADS, HG_DV, HG_DK), F32)],
        compiler_params=_cparams(("arbitrary",)),
        name="hgrn2_mixer",
    )(sel, lb_logits, gain, s0, z, z, z, z)


def _mixer_kernel(bucket_ref, table_ref, sinks_ref, sel_ref, lbl_ref, gain_ref, pre_ref, x_ref, w_ref,
                  attn_ref, rec_ref, kv_ref, sout_ref,
                  bias_ref, z_even, z_odd, k_all, v_all, st_ref, *, layer, units_per_seq):
    s = pl.program_id(0)
    unit = MIX_UNIT
    kv_w = N_KV * HEAD_DIM
    u_in_seq = (s + (units_per_seq - 1)) % units_per_seq

    @pl.when(s == 0)
    def _():
        _fill_bias(bucket_ref, table_ref, bias_ref)
        z_odd[...] = jnp.zeros(z_odd.shape, F32)
        k_all[0:WINDOW, :] = jnp.zeros((WINDOW, kv_w), BF16)
        v_all[0:WINDOW, :] = jnp.zeros((WINDOW, kv_w), BF16)

    @pl.when((s == 0) | (u_in_seq == 0))
    def _():
        st_ref[...] = jnp.zeros(st_ref.shape, F32)

    def step(z_write, z_read):
        xn = _rms(x_ref[...], pre_ref[...]).astype(BF16)
        z_write[...] = jnp.dot(xn, w_ref[...], preferred_element_type=F32)
        k_cols = slice(Z_K * kv_w, (Z_K + 1) * kv_w)
        v_cols = slice(Z_V * kv_w, (Z_V + 1) * kv_w)
        k_all[WINDOW:, :] = z_read[:, k_cols].astype(BF16)
        v_all[WINDOW:, :] = z_read[:, v_cols].astype(BF16)
        kv_ref[:, 0:kv_w] = z_read[:, k_cols]
        kv_ref[:, kv_w:] = z_read[:, v_cols]
        _attend_tile(z_read, attn_ref, k_all, v_all, bias_ref, sinks_ref, u_in_seq * unit)
        k_all[0:WINDOW, :] = k_all[unit:unit + WINDOW, :]
        v_all[0:WINDOW, :] = v_all[unit:unit + WINDOW, :]
        lb = _lower_bound(lbl_ref, layer)
        for c in range(unit // CHUNK):
            rows = slice(c * CHUNK, (c + 1) * CHUNK)

            def zcols(col):
                return z_read[rows, col * D_HG:(col + 1) * D_HG]

            outs = _hgrn_chunk(zcols(Z_QH), zcols(Z_FH), zcols(Z_IH), zcols(Z_GH), sel_ref[...],
                               lb, gain_ref[...], st_ref)
            for h in range(HG_HEADS):
                rec_ref[rows, h * HG_DV:(h + 1) * HG_DV] = outs[h]

    @pl.when(s % 2 == 0)
    def _():
        step(z_even, z_odd)

    @pl.when(s % 2 == 1)
    def _():
        step(z_odd, z_even)

    @pl.when((s > 0) & (u_in_seq == units_per_seq - 1))
    def _():
        for h in range(HG_HEADS):
            sout_ref[0, h] = st_ref[h].T


def _mixer_front(x, pre, w_in, bucket, table, sinks, lb_logits, gain, *, n_seq, seq_len, layer):
    unit = MIX_UNIT
    ups = seq_len // unit
    n_units = n_seq * ups
    n_layers = lb_logits.shape[0]
    kv_w = N_KV * HEAD_DIM
    sel = _hgrn_selector()
    smem = pl.BlockSpec(memory_space=pltpu.SMEM)

    def const(shape):
        return pl.BlockSpec(shape, lambda s: (0,) * len(shape))

    def prev_unit(s):
        return (jnp.maximum(s - 1, 0), 0)

    return pl.pallas_call(
        functools.partial(_mixer_kernel, layer=layer, units_per_seq=ups),
        grid=(n_units + 1,),
        in_specs=[
            const((CHUNK, LK)), smem, smem, const(sel.shape), const((n_layers, D_HG)),
            const((1, HG_DV)), const((1, D_MODEL)),
            pl.BlockSpec((unit, D_MODEL), lambda s: (jnp.minimum(s, n_units - 1), 0)),
            pl.BlockSpec((D_MODEL, D_IN), lambda s: (0, 0), pipeline_mode=pl.Buffered(1)),
        ],
        out_specs=[
            pl.BlockSpec((unit, D_ATTN), prev_unit),
            pl.BlockSpec((unit, D_HG), prev_unit),
            pl.BlockSpec((unit, 2 * kv_w), prev_unit),
            pl.BlockSpec((1, HG_HEADS, HG_DK, HG_DV), lambda s: (jnp.maximum(s - 1, 0) // ups, 0, 0, 0)),
        ],
        out_shape=[
            jax.ShapeDtypeStruct((n_units * unit, D_ATTN), BF16),
            jax.ShapeDtypeStruct((n_units * unit, D_HG), BF16),
            jax.ShapeDtypeStruct((n_units * unit, 2 * kv_w), F32),
            jax.ShapeDtypeStruct((n_seq, HG_HEADS, HG_DK, HG_DV), F32),
        ],
        scratch_shapes=[
            pltpu.VMEM((N_KV, GROUP * CHUNK, LK), F32),
            pltpu.VMEM((unit, D_IN), F32),
            pltpu.VMEM((unit, D_IN), F32),
            pltpu.VMEM((WINDOW + unit, kv_w), BF16),
            pltpu.VMEM((WINDOW + unit, kv_w), BF16),
            pltpu.VMEM((HG_HEADS, HG_DV, HG_DK), F32),
        ],
        compiler_params=_cparams(("arbitrary",)),
        name="mixer_front",
    )(bucket, table, sinks, sel, lb_logits, gain, pre, x, w_in)


def _outproj_kernel(x_ref, a_ref, r_ref, w_ref, post_ref, o_ref):
    nsub = x_ref.shape[0] // SUB_ROWS
    sub = [slice(r * SUB_ROWS, (r + 1) * SUB_ROWS) for r in range(nsub)]
    mix = [None] * nsub

    def project(r):
        mix[r] = (jnp.dot(a_ref[sub[r], :], w_ref[0:D_ATTN, :], preferred_element_type=F32)
                  + jnp.dot(r_ref[sub[r], :], w_ref[D_ATTN:, :], preferred_element_type=F32))

    def finish(r):
        o_ref[sub[r], :] = x_ref[sub[r], :] + _rms(mix[r], post_ref[...])

    _pipelined(nsub, None, project, finish)


def _outproj(x, attn, rec, w_out, post):
    t = x.shape[0]
    tm = min(ROW_TM, t)
    return pl.pallas_call(
        _outproj_kernel,
        grid=(t // tm,),
        in_specs=[
            pl.BlockSpec((tm, D_MODEL), lambda i: (i, 0)),
            pl.BlockSpec((tm, D_ATTN), lambda i: (i, 0)),
            pl.BlockSpec((tm, D_HG), lambda i: (i, 0)),
            pl.BlockSpec((D_MODEL, D_MODEL), lambda i: (0, 0), pipeline_mode=pl.Buffered(1)),
            pl.BlockSpec((1, D_MODEL), lambda i: (0, 0)),
        ],
        out_specs=pl.BlockSpec((tm, D_MODEL), lambda i: (i, 0)),
        out_shape=jax.ShapeDtypeStruct((t, D_MODEL), F32),
        compiler_params=_cparams(("parallel",)),
        name="mixer_out_proj",
    )(x, attn, rec, w_out, post)


def _ple_kernel(x_ref, p_ref, pre_ref, post_ref, wg_ref, wp_ref, o_ref):
    nsub = x_ref.shape[0] // SUB_ROWS
    sub = [slice(r * SUB_ROWS, (r + 1) * SUB_ROWS) for r in range(nsub)]
    hs, gates, projs = [None] * nsub, [None] * nsub, [None] * nsub

    def norm(r):
        hs[r] = _rms(x_ref[sub[r], :], pre_ref[...]).astype(BF16)

    def project(r):
        gates[r] = jnp.dot(hs[r], wg_ref[...], preferred_element_type=F32)
        projs[r] = jnp.dot(p_ref[sub[r], :].astype(BF16), wp_ref[...], preferred_element_type=F32)

    def finish(r):
        o_ref[sub[r], :] = x_ref[sub[r], :] + _rms(_sigmoid(gates[r]) * projs[r], post_ref[...])

    _pipelined(nsub, norm, project, finish)


def _ple(x, p, pre, post, w_gate, w_proj):
    t = x.shape[0]
    tm = min(ROW_TM, t)
    return pl.pallas_call(
        _ple_kernel,
        grid=(t // tm,),
        in_specs=[
            pl.BlockSpec((tm, D_MODEL), lambda i: (i, 0)),
            pl.BlockSpec((tm, PLE_DIM), lambda i: (i, 0)),
            pl.BlockSpec((1, D_MODEL), lambda i: (0, 0)),
            pl.BlockSpec((1, D_MODEL), lambda i: (0, 0)),
            pl.BlockSpec((D_MODEL, D_MODEL), lambda i: (0, 0), pipeline_mode=pl.Buffered(1)),
            pl.BlockSpec((PLE_DIM, D_MODEL), lambda i: (0, 0)),
        ],
        out_specs=pl.BlockSpec((tm, D_MODEL), lambda i: (i, 0)),
        out_shape=jax.ShapeDtypeStruct((t, D_MODEL), F32),
        compiler_params=_cparams(("parallel",)),
        name="ple_embed",
    )(x, p, pre, post, w_gate, w_proj)


def _t5_bucket(rel):
    half = NUM_BUCKETS // 2
    max_exact = half // 2
    n = jnp.abs(rel)
    nf = jnp.maximum(n, 1).astype(jnp.float32)
    large = max_exact + (jnp.log(nf / max_exact) / math.log(MAX_DISTANCE / max_exact)
                         * (half - max_exact)).astype(jnp.int32)
    large = jnp.minimum(large, half - 1)
    return jnp.where(rel > 0, half, 0) + jnp.where(n < max_exact, n, large)


def _ffn1(x, w):
    row = lambda v: v.reshape(1, -1)
    return _ffn(x, row(w['ffn1_pre']), row(w['ffn1_post']), w['ffn1_w_gate'], w['ffn1_w_up'], w['ffn1_w_down'])


def _back_half(x, p, attn, rec, w):
    row = lambda v: v.reshape(1, -1)
    x = _outproj(x, attn, rec, w['w_out'], row(w['mix_post']))
    x = _ffn(x, row(w['ffn2_pre']), row(w['ffn2_post']), w['ffn2_w_gate'], w['ffn2_w_up'], w['ffn2_w_down'])
    return _ple(x, p, row(w['ple_pre']), row(w['ple_post']), w['w_ple_gate'], w['w_ple_proj'])


def _prompt_layer(x, p, w, table, bucket, layer, *, n_seq, seq_len):
    x = _ffn1(x, w)
    attn, rec, kv, s_new = _mixer_front(
        x, w['mix_pre'].reshape(1, -1), w['w_in'], bucket, table, w['attn_sinks'], w['hgrn_lb_logits'],
        w['hgrn_norm'].reshape(1, -1), n_seq=n_seq, seq_len=seq_len, layer=layer)
    return _back_half(x, p, attn, rec, w), kv, s_new


def _sample_layer(x, p, w, table, bucket, layer, *, n_seq, s0, k_cache, v_cache):
    kv_w = N_KV * HEAD_DIM
    x = _ffn1(x, w)
    z = _inproj(x, w['mix_pre'].reshape(1, -1), w['w_in'])
    attn = _attention_cached(bucket, table, w['attn_sinks'], z, k_cache.reshape(n_seq * WINDOW, kv_w),
                             v_cache.reshape(n_seq * WINDOW, kv_w), n_seq=n_seq)
    rec, s_new = _hgrn_single_chunk(w['hgrn_lb_logits'], w['hgrn_norm'].reshape(1, -1), s0, z,
                                    n_seq=n_seq, layer=layer)
    return _back_half(x, p, attn, rec, w), z[:, Z_K * kv_w:], s_new


def kernel(x_prompt, x_sample, cache_attn_k, cache_attn_v, state_hgrn, p_prompt, p_sample,
           rel_bias_table, ffn1_pre, ffn1_post, ffn1_w_gate, ffn1_w_up, ffn1_w_down,
           mix_pre, mix_post, w_in, w_out, attn_sinks, hgrn_lb_logits, hgrn_norm,
           ffn2_pre, ffn2_post, ffn2_w_gate, ffn2_w_up, ffn2_w_down,
           ple_pre, ple_post, w_ple_gate, w_ple_proj):
    depth = w_in.shape[0]
    bp, sp, _ = x_prompt.shape
    bs, ss, _ = x_sample.shape
    wc = cache_attn_k.shape[2]
    assert wc == WINDOW and ss == CHUNK and sp % MIX_UNIT == 0 and sp >= WINDOW

    rel = jnp.arange(LK)[None, :] - WINDOW - jnp.arange(CHUNK)[:, None]
    bucket = _t5_bucket(rel).astype(jnp.int32)

    kv_w = N_KV * HEAD_DIM
    k_lo, v_lo = D_ATTN, D_ATTN + kv_w
    yp = x_prompt.reshape(bp * sp, D_MODEL)
    ys = x_sample.reshape(bs * ss, D_MODEL)
    outs = [[] for _ in range(6)]
    for l in range(depth):
        w_in_l = w_in[l]
        w = {
            'ffn1_pre': ffn1_pre[l], 'ffn1_post': ffn1_post[l],
            'ffn1_w_gate': ffn1_w_gate[l].astype(BF16), 'ffn1_w_up': ffn1_w_up[l].astype(BF16),
            'ffn1_w_down': ffn1_w_down[l].astype(BF16),
            'mix_pre': mix_pre[l], 'mix_post': mix_post[l],
            'w_in': jnp.concatenate([w_in_l[:, :k_lo], w_in_l[:, v_lo + kv_w:], w_in_l[:, k_lo:v_lo + kv_w]],
                                    axis=1).astype(BF16),
            'w_out': w_out[l].astype(BF16), 'attn_sinks': attn_sinks[l],
            'hgrn_lb_logits': hgrn_lb_logits, 'hgrn_norm': hgrn_norm[l],
            'ffn2_pre': ffn2_pre[l], 'ffn2_post': ffn2_post[l],
            'ffn2_w_gate': ffn2_w_gate[l].astype(BF16), 'ffn2_w_up': ffn2_w_up[l].astype(BF16),
            'ffn2_w_down': ffn2_w_down[l].astype(BF16),
            'ple_pre': ple_pre[l], 'ple_post': ple_post[l],
            'w_ple_gate': w_ple_gate[l].astype(BF16), 'w_ple_proj': w_ple_proj[l].astype(BF16),
        }
        yp, kv_p, st_p = _prompt_layer(yp, p_prompt[l].reshape(bp * sp, PLE_DIM), w, rel_bias_table, bucket, l,
                                       n_seq=bp, seq_len=sp)
        ys, kv_s, st_s = _sample_layer(ys, p_sample[l].reshape(bs * ss, PLE_DIM), w, rel_bias_table, bucket, l,
                                       n_seq=bs, s0=state_hgrn[l], k_cache=cache_attn_k[l],
                                       v_cache=cache_attn_v[l])
        kv_p = kv_p.reshape(bp, sp, 2 * kv_w)[:, sp - WINDOW:].reshape(bp, WINDOW, 2, N_KV, HEAD_DIM)
        outs[0].append(kv_p[:, :, 0])
        outs[1].append(kv_p[:, :, 1])
        outs[2].append(st_p)
        kv_s = kv_s.reshape(bs, ss, 2, N_KV, HEAD_DIM)
        outs[3].append(jnp.concatenate([cache_attn_k[l], kv_s[:, :, 0]], axis=1)[:, ss:])
        outs[4].append(jnp.concatenate([cache_attn_v[l], kv_s[:, :, 1]], axis=1)[:, ss:])
        outs[5].append(st_s)
    return (yp.reshape(bp, sp, D_MODEL), ys.reshape(bs, ss, D_MODEL),
            jnp.stack(outs[0]), jnp.stack(outs[1]), jnp.stack(outs[2]),
            jnp.stack(outs[3]), jnp.stack(outs[4]), jnp.stack(outs[5]))
```

```python
import functools
import math

import jax
import jax.numpy as jnp
import numpy as np
from jax import lax
from jax.experimental import pallas as pl
from jax.experimental.pallas import tpu as pltpu

F32 = jnp.float32
BF16 = jnp.bfloat16

D_MODEL = 2048
CHUNK = 64
D_ATTN = 1024
HEAD_DIM = 128
N_HEADS = 8
N_KV = 2
GROUP = N_HEADS // N_KV
WINDOW = 128
LK = WINDOW + CHUNK
NUM_BUCKETS = 32
MAX_DISTANCE = 128
HG_DK = 128
HG_DV = 128
HG_HEADS = 8
D_HG = HG_HEADS * HG_DV
HG_SUB = 16
HG_SUB_SHIFT = HG_SUB.bit_length() - 1
CHUNK_SHIFT = CHUNK.bit_length() - 1
D_FF = 5632
PLE_DIM = 256
EPS = 1e-6
NEG_INF = -1e30
D_IN = D_ATTN + 2 * N_KV * HEAD_DIM + 4 * D_HG

Z_QH, Z_FH, Z_IH, Z_GH = 1, 2, 3, 4
Z_K, Z_V = 20, 21

VMEM_LIMIT_BYTES = 60 * 1024 * 1024

FFN_TM = 1024
FFN_TF = 512
PROJ_TM = 1024
PROJ_TN = 512
ROW_TM = 1024
SUB_ROWS = 256
NORM_ROWS = 256
MIX_UNIT = 256
MIX_PROJ_TN = 512


def _rms(x, gain):
    y = x * lax.rsqrt(jnp.mean(x * x, axis=-1, keepdims=True) + EPS)
    return y * gain


def _sigmoid(x):
    return 1.0 / (1.0 + jnp.exp(-x))


def _cparams(sem):
    return pltpu.CompilerParams(dimension_semantics=sem, vmem_limit_bytes=VMEM_LIMIT_BYTES)


def _pipelined(n, before, matmul, after):
    if before is not None:
        before(0)
    for r in range(n):
        if before is not None and r + 1 < n:
            before(r + 1)
        matmul(r)
        if after is not None and r > 0:
            after(r - 1)
    if after is not None:
        after(n - 1)


def _ffn_kernel(x_ref, pre_ref, post_ref, wg_ref, wu_ref, wd_ref, o_ref, xn_ref):
    j = pl.program_id(1)
    last = pl.num_programs(1) - 1
    nsub = x_ref.shape[0] // SUB_ROWS
    sub = [slice(r * SUB_ROWS, (r + 1) * SUB_ROWS) for r in range(nsub)]

    def swiglu(rows):
        xn = xn_ref[rows, :]
        g = jnp.dot(xn, wg_ref[...], preferred_element_type=F32)
        u = jnp.dot(xn, wu_ref[...], preferred_element_type=F32)
        h = (g * _sigmoid(g) * u).astype(BF16)
        return jnp.dot(h, wd_ref[...], preferred_element_type=F32)

    @pl.when(j == 0)
    def _():
        def norm(r):
            xn_ref[sub[r], :] = _rms(x_ref[sub[r], :], pre_ref[...]).astype(BF16)

        def first(r):
            o_ref[sub[r], :] = swiglu(sub[r])

        _pipelined(nsub, norm, first, None)

    @pl.when((j > 0) & (j < last))
    def _():
        o_ref[...] += swiglu(slice(None))

    @pl.when(j == last)
    def _():
        ys = [None] * nsub

        def final(r):
            ys[r] = o_ref[sub[r], :] + swiglu(sub[r])

        def finish(r):
            o_ref[sub[r], :] = x_ref[sub[r], :] + 0.5 * _rms(ys[r], post_ref[...])

        _pipelined(nsub, None, final, finish)


def _ffn(x, pre, post, wg, wu, wd):
    t = x.shape[0]
    tm = min(FFN_TM, t)
    grid = (t // tm, D_FF // FFN_TF)
    assert grid[1] >= 3
    return pl.pallas_call(
        _ffn_kernel,
        grid=grid,
        in_specs=[
            pl.BlockSpec((tm, D_MODEL), lambda i, j: (i, 0)),
            pl.BlockSpec((1, D_MODEL), lambda i, j: (0, 0)),
            pl.BlockSpec((1, D_MODEL), lambda i, j: (0, 0)),
            pl.BlockSpec((D_MODEL, FFN_TF), lambda i, j: (0, j)),
            pl.BlockSpec((D_MODEL, FFN_TF), lambda i, j: (0, j)),
            pl.BlockSpec((FFN_TF, D_MODEL), lambda i, j: (j, 0)),
        ],
        out_specs=pl.BlockSpec((tm, D_MODEL), lambda i, j: (i, 0)),
        out_shape=jax.ShapeDtypeStruct((t, D_MODEL), F32),
        scratch_shapes=[pltpu.VMEM((tm, D_MODEL), BF16)],
        compiler_params=_cparams(("parallel", "arbitrary")),
        name="ffn_half_step",
    )(x, pre, post, wg, wu, wd)


def _inproj_kernel(x_ref, pre_ref, w_ref, z_ref, xn_ref):
    j = pl.program_id(1)
    tm = x_ref.shape[0]

    @pl.when(j == 0)
    def _():
        def body(r, c):
            rows = pl.ds(pl.multiple_of(r * NORM_ROWS, NORM_ROWS), NORM_ROWS)
            xn_ref[rows, :] = _rms(x_ref[rows, :], pre_ref[...]).astype(BF16)
            return c
        lax.fori_loop(0, tm // NORM_ROWS, body, 0)

    z_ref[...] = jnp.dot(xn_ref[...], w_ref[...], preferred_element_type=F32)


def _inproj(x, pre, w_in):
    t = x.shape[0]
    tm = min(PROJ_TM, t)
    grid = (t // tm, D_IN // PROJ_TN)
    return pl.pallas_call(
        _inproj_kernel,
        grid=grid,
        in_specs=[
            pl.BlockSpec((tm, D_MODEL), lambda i, j: (i, 0)),
            pl.BlockSpec((1, D_MODEL), lambda i, j: (0, 0)),
            pl.BlockSpec((D_MODEL, PROJ_TN), lambda i, j: (0, j)),
        ],
        out_specs=pl.BlockSpec((tm, PROJ_TN), lambda i, j: (i, j)),
        out_shape=jax.ShapeDtypeStruct((t, D_IN), F32),
        scratch_shapes=[pltpu.VMEM((tm, D_MODEL), BF16)],
        compiler_params=_cparams(("parallel", "arbitrary")),
        name="mixer_in_proj",
    )(x, pre, w_in)


def _fill_bias(bucket_ref, table_ref, bias_ref):
    bucket = bucket_ref[...]
    for h in range(N_HEADS):
        acc = jnp.zeros((CHUNK, LK), F32)
        for n in range(NUM_BUCKETS):
            acc = jnp.where(bucket == n, table_ref[n, h], acc)
        g, hh = divmod(h, GROUP)
        bias_ref[g, hh * CHUNK:(hh + 1) * CHUNK, :] = acc


def _sink_column(sinks_ref, g):
    row = lax.broadcasted_iota(jnp.int32, (GROUP * CHUNK, 1), 0)
    sink = jnp.zeros((GROUP * CHUNK, 1), F32)
    for hh in range(GROUP):
        sink = jnp.where((row >> CHUNK_SHIFT) == hh, sinks_ref[g * GROUP + hh], sink)
    return sink


def _attend_chunk(q_of_head, kk, vv, bias_g, sink, first_key_pos):
    qs = jnp.concatenate([q_of_head(hh) for hh in range(GROUP)], axis=0).astype(BF16)
    s = lax.dot_general(qs, kk, (((1,), (1,)), ((), ())), preferred_element_type=F32)
    s = s * (HEAD_DIM ** -0.5) + bias_g
    if first_key_pos is not None:
        col = lax.broadcasted_iota(jnp.int32, (GROUP * CHUNK, LK), 1)
        s = jnp.where(col + first_key_pos >= 0, s, NEG_INF)
    m = jnp.maximum(jnp.max(s, axis=-1, keepdims=True), sink)
    e = jnp.exp(s - m)
    p = e / (jnp.sum(e, axis=-1, keepdims=True) + jnp.exp(sink - m))
    return jnp.dot(p.astype(BF16), vv, preferred_element_type=F32)


def _attend_stages(q_ref, o_ref, k_all, v_all, bias_ref, sinks_ref, pos0):
    def stage(c, g):
        rows = slice(c * CHUNK, (c + 1) * CHUNK)
        cols = slice(g * HEAD_DIM, (g + 1) * HEAD_DIM)
        masked = pos0 is not None and c * CHUNK < WINDOW

        def q_of_head(hh):
            h = g * GROUP + hh
            return q_ref[rows, h * HEAD_DIM:(h + 1) * HEAD_DIM]

        o = _attend_chunk(q_of_head, k_all[c * CHUNK:c * CHUNK + LK, cols],
                          v_all[c * CHUNK:c * CHUNK + LK, cols], bias_ref[g], _sink_column(sinks_ref, g),
                          pos0 + (c * CHUNK - WINDOW) if masked else None)
        for hh in range(GROUP):
            h = g * GROUP + hh
            o_ref[rows, h * HEAD_DIM:(h + 1) * HEAD_DIM] = o[hh * CHUNK:(hh + 1) * CHUNK, :].astype(BF16)

    return [functools.partial(stage, c, g) for c in range(q_ref.shape[0] // CHUNK) for g in range(N_KV)]


def _interleave(a, b):
    ia = ib = 0
    while ia < len(a) or ib < len(b):
        if ib >= len(b) or (ia < len(a) and ia * len(b) <= ib * len(a)):
            a[ia]()
            ia += 1
        else:
            b[ib]()
            ib += 1


def _attn_kernel(bucket_ref, table_ref, sinks_ref, q_ref, kc_ref, vc_ref, kp_ref, vp_ref,
                 o_ref, bias_ref, k_all, v_all):
    @pl.when(pl.program_id(0) == 0)
    def _():
        _fill_bias(bucket_ref, table_ref, bias_ref)

    k_all[0:WINDOW, :] = kp_ref[...].astype(BF16)
    v_all[0:WINDOW, :] = vp_ref[...].astype(BF16)
    k_all[WINDOW:, :] = kc_ref[...].astype(BF16)
    v_all[WINDOW:, :] = vc_ref[...].astype(BF16)
    for stage in _attend_stages(q_ref, o_ref, k_all, v_all, bias_ref, sinks_ref, None):
        stage()


def _attention_cached(bucket, table, sinks, z, k_prev, v_prev, *, n_seq):
    kv_w = N_KV * HEAD_DIM
    smem = pl.BlockSpec(memory_space=pltpu.SMEM)
    return pl.pallas_call(
        _attn_kernel,
        grid=(n_seq,),
        in_specs=[
            pl.BlockSpec((CHUNK, LK), lambda b: (0, 0)),
            smem,
            smem,
            pl.BlockSpec((CHUNK, D_ATTN), lambda b: (b, 0)),
            pl.BlockSpec((CHUNK, kv_w), lambda b: (b, Z_K)),
            pl.BlockSpec((CHUNK, kv_w), lambda b: (b, Z_V)),
            pl.BlockSpec((WINDOW, kv_w), lambda b: (b, 0)),
            pl.BlockSpec((WINDOW, kv_w), lambda b: (b, 0)),
        ],
        out_specs=pl.BlockSpec((CHUNK, D_ATTN), lambda b: (b, 0)),
        out_shape=jax.ShapeDtypeStruct((n_seq * CHUNK, D_ATTN), BF16),
        scratch_shapes=[
            pltpu.VMEM((N_KV, GROUP * CHUNK, LK), F32),
            pltpu.VMEM((WINDOW + CHUNK, kv_w), BF16),
            pltpu.VMEM((WINDOW + CHUNK, kv_w), BF16),
        ],
        compiler_params=_cparams(("arbitrary",)),
        name="swa_attention",
    )(bucket, table, sinks, z, z, z, k_prev, v_prev)


_HG_RANGES = [(0, 16), (0, 32), (0, 48), (0, 64), (16, 32), (32, 48), (48, 64), (16, 48), (16, 64), (32, 64)]
_HG_RANGE_ROWS = 16


def _hgrn_selector():
    t = np.arange(CHUNK)
    within = (t[:, None] // HG_SUB == t[None, :] // HG_SUB) & (t[None, :] <= t[:, None])
    ranges = np.zeros((_HG_RANGE_ROWS, CHUNK), bool)
    for n, (lo, hi) in enumerate(_HG_RANGES):
        ranges[n, lo:hi] = True
    once = np.concatenate([within, ranges], axis=0)
    return jnp.asarray(np.concatenate([once, once, once], axis=1), BF16)


def _lower_bound(lbl_ref, layer):
    logits = lbl_ref[...]
    ex = jnp.exp(logits - jnp.max(logits, axis=0, keepdims=True))
    return jnp.sum(ex[:layer + 1], axis=0, keepdims=True) / jnp.sum(ex, axis=0, keepdims=True)


def _hgrn_chunk(q, fl, iv, gate, sel, lb, gain, st_ref):
    nsub = CHUNK // HG_SUB
    s1, s2, s3, s4 = HG_SUB, 2 * HG_SUB, 3 * HG_SUB, 4 * HG_SUB
    t_i = lax.broadcasted_iota(jnp.int32, (CHUNK, nsub * CHUNK), 0)
    c_i = lax.broadcasted_iota(jnp.int32, (CHUNK, nsub * CHUNK), 1)
    keep = ((t_i >> HG_SUB_SHIFT) == (c_i >> CHUNK_SHIFT)) & ((c_i & (CHUNK - 1)) <= t_i)
    zeros_sub = jnp.zeros((HG_SUB, HG_DK), BF16)

    def sub(a, j):
        return a[j * HG_SUB:(j + 1) * HG_SUB]

    f = lb + (1.0 - lb) * _sigmoid(fl)
    lf = jnp.log(f)
    k = 1.0 - f
    hi = lf.astype(BF16)
    r1 = lf - hi.astype(F32)
    mid = r1.astype(BF16)
    lo = (r1 - mid.astype(F32)).astype(BF16)
    cr = jnp.dot(sel, jnp.concatenate([hi, mid, lo], axis=0), preferred_element_type=F32)
    cc = cr[:CHUNK]
    ev = jnp.exp(cr[CHUNK:])
    qd = q * jnp.exp(cc)
    ki = k * jnp.exp(-cc)
    iv = iv.astype(BF16)
    gate = gate * _sigmoid(gate)
    head_cols = [slice(h * HG_DK, (h + 1) * HG_DK) for h in range(HG_HEADS)]

    def dk(sl, lo_row, hi_row):
        n = _HG_RANGES.index((lo_row, hi_row))
        return ev[n:n + 1, sl]

    scores, inter, upds = [], [], []
    for h, sl in enumerate(head_cols):
        qd_h, ki_h = qd[:, sl], ki[:, sl]
        qd_b = qd_h.astype(BF16)
        ki_b = [sub(ki_h, j).astype(BF16) for j in range(nsub)]
        kend = [sub(ki_h, 0) * dk(sl, 0, s1), sub(ki_h, 1) * dk(sl, s1, s2),
                sub(ki_h, 2) * dk(sl, s2, s3), sub(ki_h, 3) * dk(sl, s3, s4)]
        kend_b = [kend[j].astype(BF16) for j in range(nsub - 1)]
        kmat = [
            [ki_b[0], zeros_sub, zeros_sub, zeros_sub],
            [kend_b[0], ki_b[1], zeros_sub, zeros_sub],
            [(kend[0] * dk(sl, s1, s2)).astype(BF16), kend_b[1], ki_b[2], zeros_sub],
            [(kend[0] * dk(sl, s1, s3)).astype(BF16), (kend[1] * dk(sl, s2, s3)).astype(BF16),
             kend_b[2], ki_b[3]],
        ]
        scores.append(lax.dot_general(qd_b, jnp.concatenate([p for row in kmat for p in row], axis=0),
                                      (((1,), (1,)), ((), ())), preferred_element_type=F32))
        qe = jnp.concatenate([sub(qd_h, 0), sub(qd_h, 1) * dk(sl, 0, s1), sub(qd_h, 2) * dk(sl, 0, s2),
                              sub(qd_h, 3) * dk(sl, 0, s3)], axis=0).astype(BF16)
        ke = jnp.concatenate([kend[0] * dk(sl, s1, s4), kend[1] * dk(sl, s2, s4),
                              kend[2] * dk(sl, s3, s4), kend[3]], axis=0).astype(BF16)
        inter.append(lax.dot_general(qe, st_ref[h].astype(BF16), (((1,), (1,)), ((), ())),
                                     preferred_element_type=F32))
        upds.append(lax.dot_general(iv[:, sl], ke, (((0,), (0,)), ((), ())),
                                    preferred_element_type=F32))
    outs = []
    for h, sl in enumerate(head_cols):
        a = jnp.where(keep, scores[h], 0.0).astype(BF16)
        outs.append(inter[h] + jnp.dot(a, jnp.concatenate([iv[:, sl]] * nsub, axis=0),
                                       preferred_element_type=F32))
        st_ref[h] = st_ref[h] * dk(sl, 0, s4) + upds[h]
    return [(_rms(outs[h], gain) * gate[:, sl]).astype(BF16) for h, sl in enumerate(head_cols)]


def _hgrn_kernel(sel_ref, lbl_ref, gain_ref, s0_ref, q_ref, f_ref, i_ref, g_ref, rec_ref, sout_ref,
                 st_ref, *, layer):
    for h in range(HG_HEADS):
        st_ref[h] = s0_ref[0, h].T
    outs = _hgrn_chunk(q_ref[...], f_ref[...], i_ref[...], g_ref[...], sel_ref[...],
                       _lower_bound(lbl_ref, layer), gain_ref[...], st_ref)
    for h in range(HG_HEADS):
        rec_ref[:, h * HG_DV:(h + 1) * HG_DV] = outs[h]
        sout_ref[0, h] = st_ref[h].T


def _hgrn_single_chunk(lb_logits, gain, s0, z, *, n_seq, layer):
    n_layers = lb_logits.shape[0]
    sel = _hgrn_selector()

    def zspec(col):
        return pl.BlockSpec((CHUNK, D_HG), lambda b: (b, col))

    state_spec = pl.BlockSpec((1, HG_HEADS, HG_DK, HG_DV), lambda b: (b, 0, 0, 0))
    return pl.pallas_call(
        functools.partial(_hgrn_kernel, layer=layer),
        grid=(n_seq,),
        in_specs=[
            pl.BlockSpec(sel.shape, lambda b: (0, 0)),
            pl.BlockSpec((n_layers, D_HG), lambda b: (0, 0)),
            pl.BlockSpec((1, HG_DV), lambda b: (0, 0)),
            state_spec,
            zspec(Z_QH), zspec(Z_FH), zspec(Z_IH), zspec(Z_GH),
        ],
        out_specs=[pl.BlockSpec((CHUNK, D_HG), lambda b: (b, 0)), state_spec],
        out_shape=[
            jax.ShapeDtypeStruct((n_seq * CHUNK, D_HG), BF16),
            jax.ShapeDtypeStruct((n_seq, HG_HEADS, HG_DK, HG_DV), F32),
        ],
        scratch_shapes=[pltpu.VMEM((HG_HEADS, HG_DV, HG_DK), F32)],
        compiler_params=_cparams(("arbitrary",)),
        name="hgrn2_mixer",
    )(sel, lb_logits, gain, s0, z, z, z, z)


def _mixer_kernel(bucket_ref, table_ref, sinks_ref, sel_ref, lbl_ref, gain_ref, pre_ref, x_ref, w_ref,
                  attn_ref, rec_ref, kv_ref, sout_ref,
                  bias_ref, z_even, z_odd, xn_ref, k_all, v_all, st_ref, *, layer, units_per_seq):
    s = pl.program_id(0)
    unit = MIX_UNIT
    kv_w = N_KV * HEAD_DIM
    u_in_seq = (s + (units_per_seq - 1)) % units_per_seq

    @pl.when(s == 0)
    def _():
        _fill_bias(bucket_ref, table_ref, bias_ref)
        z_odd[...] = jnp.zeros(z_odd.shape, F32)
        k_all[0:WINDOW, :] = jnp.zeros((WINDOW, kv_w), BF16)
        v_all[0:WINDOW, :] = jnp.zeros((WINDOW, kv_w), BF16)

    @pl.when((s == 0) | (u_in_seq == 0))
    def _():
        st_ref[...] = jnp.zeros(st_ref.shape, F32)

    def step(z_write, z_read):
        xn_ref[...] = _rms(x_ref[...], pre_ref[...]).astype(BF16)

        def project(c):
            cols = slice(c * MIX_PROJ_TN, (c + 1) * MIX_PROJ_TN)
            z_write[:, cols] = jnp.dot(xn_ref[...], w_ref[:, cols], preferred_element_type=F32)

        k_cols = slice(Z_K * kv_w, (Z_K + 1) * kv_w)
        v_cols = slice(Z_V * kv_w, (Z_V + 1) * kv_w)
        lb = _lower_bound(lbl_ref, layer)

        def load_kv():
            k_all[WINDOW:, :] = z_read[:, k_cols].astype(BF16)
            v_all[WINDOW:, :] = z_read[:, v_cols].astype(BF16)
            kv_ref[:, 0:kv_w] = z_read[:, k_cols]
            kv_ref[:, kv_w:] = z_read[:, v_cols]

        def keep_history():
            k_all[0:WINDOW, :] = k_all[unit:unit + WINDOW, :]
            v_all[0:WINDOW, :] = v_all[unit:unit + WINDOW, :]

        def recur(c):
            rows = slice(c * CHUNK, (c + 1) * CHUNK)

            def zcols(col):
                return z_read[rows, col * D_HG:(col + 1) * D_HG]

            outs = _hgrn_chunk(zcols(Z_QH), zcols(Z_FH), zcols(Z_IH), zcols(Z_GH), sel_ref[...],
                               lb, gain_ref[...], st_ref)
            for h in range(HG_HEADS):
                rec_ref[rows, h * HG_DV:(h + 1) * HG_DV] = outs[h]

        mixers = ([load_kv]
                  + _attend_stages(z_read, attn_ref, k_all, v_all, bias_ref, sinks_ref, u_in_seq * unit)
                  + [keep_history]
                  + [functools.partial(recur, c) for c in range(unit // CHUNK)])
        _interleave([functools.partial(project, c) for c in range(D_IN // MIX_PROJ_TN)], mixers)

    @pl.when(s % 2 == 0)
    def _():
        step(z_even, z_odd)

    @pl.when(s % 2 == 1)
    def _():
        step(z_odd, z_even)

    @pl.when((s > 0) & (u_in_seq == units_per_seq - 1))
    def _():
        for h in range(HG_HEADS):
            sout_ref[0, h] = st_ref[h].T


def _mixer_front(x, pre, w_in, bucket, table, sinks, lb_logits, gain, *, n_seq, seq_len, layer):
    unit = MIX_UNIT
    ups = seq_len // unit
    n_units = n_seq * ups
    n_layers = lb_logits.shape[0]
    kv_w = N_KV * HEAD_DIM
    sel = _hgrn_selector()
    smem = pl.BlockSpec(memory_space=pltpu.SMEM)

    def const(shape):
        return pl.BlockSpec(shape, lambda s: (0,) * len(shape))

    def prev_unit(s):
        return (jnp.maximum(s - 1, 0), 0)

    return pl.pallas_call(
        functools.partial(_mixer_kernel, layer=layer, units_per_seq=ups),
        grid=(n_units + 1,),
        in_specs=[
            const((CHUNK, LK)), smem, smem, const(sel.shape), const((n_layers, D_HG)),
            const((1, HG_DV)), const((1, D_MODEL)),
            pl.BlockSpec((unit, D_MODEL), lambda s: (jnp.minimum(s, n_units - 1), 0)),
            pl.BlockSpec((D_MODEL, D_IN), lambda s: (0, 0), pipeline_mode=pl.Buffered(1)),
        ],
        out_specs=[
            pl.BlockSpec((unit, D_ATTN), prev_unit),
            pl.BlockSpec((unit, D_HG), prev_unit),
            pl.BlockSpec((unit, 2 * kv_w), prev_unit),
            pl.BlockSpec((1, HG_HEADS, HG_DK, HG_DV), lambda s: (jnp.maximum(s - 1, 0) // ups, 0, 0, 0)),
        ],
        out_shape=[
            jax.ShapeDtypeStruct((n_units * unit, D_ATTN), BF16),
            jax.ShapeDtypeStruct((n_units * unit, D_HG), BF16),
            jax.ShapeDtypeStruct((n_units * unit, 2 * kv_w), F32),
            jax.ShapeDtypeStruct((n_seq, HG_HEADS, HG_DK, HG_DV), F32),
        ],
        scratch_shapes=[
            pltpu.VMEM((N_KV, GROUP * CHUNK, LK), F32),
            pltpu.VMEM((unit, D_IN), F32),
            pltpu.VMEM((unit, D_IN), F32),
            pltpu.VMEM((unit, D_MODEL), BF16),
            pltpu.VMEM((WINDOW + unit, kv_w), BF16),
            pltpu.VMEM((WINDOW + unit, kv_w), BF16),
            pltpu.VMEM((HG_HEADS, HG_DV, HG_DK), F32),
        ],
        compiler_params=_cparams(("arbitrary",)),
        name="mixer_front",
    )(bucket, table, sinks, sel, lb_logits, gain, pre, x, w_in)


def _outproj_kernel(x_ref, a_ref, r_ref, w_ref, post_ref, o_ref):
    nsub = x_ref.shape[0] // SUB_ROWS
    sub = [slice(r * SUB_ROWS, (r + 1) * SUB_ROWS) for r in range(nsub)]
    mix = [None] * nsub

    def project(r):
        mix[r] = (jnp.dot(a_ref[sub[r], :], w_ref[0:D_ATTN, :], preferred_element_type=F32)
                  + jnp.dot(r_ref[sub[r], :], w_ref[D_ATTN:, :], preferred_element_type=F32))

    def finish(r):
        o_ref[sub[r], :] = x_ref[sub[r], :] + _rms(mix[r], post_ref[...])

    _pipelined(nsub, None, project, finish)


def _outproj(x, attn, rec, w_out, post):
    t = x.shape[0]
    tm = min(ROW_TM, t)
    return pl.pallas_call(
        _outproj_kernel,
        grid=(t // tm,),
        in_specs=[
            pl.BlockSpec((tm, D_MODEL), lambda i: (i, 0)),
            pl.BlockSpec((tm, D_ATTN), lambda i: (i, 0)),
            pl.BlockSpec((tm, D_HG), lambda i: (i, 0)),
            pl.BlockSpec((D_MODEL, D_MODEL), lambda i: (0, 0), pipeline_mode=pl.Buffered(1)),
            pl.BlockSpec((1, D_MODEL), lambda i: (0, 0)),
        ],
        out_specs=pl.BlockSpec((tm, D_MODEL), lambda i: (i, 0)),
        out_shape=jax.ShapeDtypeStruct((t, D_MODEL), F32),
        compiler_params=_cparams(("parallel",)),
        name="mixer_out_proj",
    )(x, attn, rec, w_out, post)


def _ple_kernel(x_ref, p_ref, pre_ref, post_ref, wg_ref, wp_ref, o_ref):
    nsub = x_ref.shape[0] // SUB_ROWS
    sub = [slice(r * SUB_ROWS, (r + 1) * SUB_ROWS) for r in range(nsub)]
    hs, gates, projs = [None] * nsub, [None] * nsub, [None] * nsub

    def norm(r):
        hs[r] = _rms(x_ref[sub[r], :], pre_ref[...]).astype(BF16)

    def project(r):
        gates[r] = jnp.dot(hs[r], wg_ref[...], preferred_element_type=F32)
        projs[r] = jnp.dot(p_ref[sub[r], :].astype(BF16), wp_ref[...], preferred_element_type=F32)

    def finish(r):
        o_ref[sub[r], :] = x_ref[sub[r], :] + _rms(_sigmoid(gates[r]) * projs[r], post_ref[...])

    _pipelined(nsub, norm, project, finish)


def _ple(x, p, pre, post, w_gate, w_proj):
    t = x.shape[0]
    tm = min(ROW_TM, t)
    return pl.pallas_call(
        _ple_kernel,
        grid=(t // tm,),
        in_specs=[
            pl.BlockSpec((tm, D_MODEL), lambda i: (i, 0)),
            pl.BlockSpec((tm, PLE_DIM), lambda i: (i, 0)),
            pl.BlockSpec((1, D_MODEL), lambda i: (0, 0)),
            pl.BlockSpec((1, D_MODEL), lambda i: (0, 0)),
            pl.BlockSpec((D_MODEL, D_MODEL), lambda i: (0, 0), pipeline_mode=pl.Buffered(1)),
            pl.BlockSpec((PLE_DIM, D_MODEL), lambda i: (0, 0)),
        ],
        out_specs=pl.BlockSpec((tm, D_MODEL), lambda i: (i, 0)),
        out_shape=jax.ShapeDtypeStruct((t, D_MODEL), F32),
        compiler_params=_cparams(("parallel",)),
        name="ple_embed",
    )(x, p, pre, post, w_gate, w_proj)


def _t5_bucket(rel):
    half = NUM_BUCKETS // 2
    max_exact = half // 2
    n = jnp.abs(rel)
    nf = jnp.maximum(n, 1).astype(jnp.float32)
    large = max_exact + (jnp.log(nf / max_exact) / math.log(MAX_DISTANCE / max_exact)
                         * (half - max_exact)).astype(jnp.int32)
    large = jnp.minimum(large, half - 1)
    return jnp.where(rel > 0, half, 0) + jnp.where(n < max_exact, n, large)


def _ffn1(x, w):
    row = lambda v: v.reshape(1, -1)
    return _ffn(x, row(w['ffn1_pre']), row(w['ffn1_post']), w['ffn1_w_gate'], w['ffn1_w_up'], w['ffn1_w_down'])


def _back_half(x, p, attn, rec, w):
    row = lambda v: v.reshape(1, -1)
    x = _outproj(x, attn, rec, w['w_out'], row(w['mix_post']))
    x = _ffn(x, row(w['ffn2_pre']), row(w['ffn2_post']), w['ffn2_w_gate'], w['ffn2_w_up'], w['ffn2_w_down'])
    return _ple(x, p, row(w['ple_pre']), row(w['ple_post']), w['w_ple_gate'], w['w_ple_proj'])


def _prompt_layer(x, p, w, table, bucket, layer, *, n_seq, seq_len):
    x = _ffn1(x, w)
    attn, rec, kv, s_new = _mixer_front(
        x, w['mix_pre'].reshape(1, -1), w['w_in'], bucket, table, w['attn_sinks'], w['hgrn_lb_logits'],
        w['hgrn_norm'].reshape(1, -1), n_seq=n_seq, seq_len=seq_len, layer=layer)
    return _back_half(x, p, attn, rec, w), kv, s_new


def _sample_layer(x, p, w, table, bucket, layer, *, n_seq, s0, k_cache, v_cache):
    kv_w = N_KV * HEAD_DIM
    x = _ffn1(x, w)
    z = _inproj(x, w['mix_pre'].reshape(1, -1), w['w_in'])
    attn = _attention_cached(bucket, table, w['attn_sinks'], z, k_cache.reshape(n_seq * WINDOW, kv_w),
                             v_cache.reshape(n_seq * WINDOW, kv_w), n_seq=n_seq)
    rec, s_new = _hgrn_single_chunk(w['hgrn_lb_logits'], w['hgrn_norm'].reshape(1, -1), s0, z,
                                    n_seq=n_seq, layer=layer)
    return _back_half(x, p, attn, rec, w), z[:, Z_K * kv_w:], s_new


def kernel(x_prompt, x_sample, cache_attn_k, cache_attn_v, state_hgrn, p_prompt, p_sample,
           rel_bias_table, ffn1_pre, ffn1_post, ffn1_w_gate, ffn1_w_up, ffn1_w_down,
           mix_pre, mix_post, w_in, w_out, attn_sinks, hgrn_lb_logits, hgrn_norm,
           ffn2_pre, ffn2_post, ffn2_w_gate, ffn2_w_up, ffn2_w_down,
           ple_pre, ple_post, w_ple_gate, w_ple_proj):
    depth = w_in.shape[0]
    bp, sp, _ = x_prompt.shape
    bs, ss, _ = x_sample.shape
    wc = cache_attn_k.shape[2]
    assert wc == WINDOW and ss == CHUNK and sp % MIX_UNIT == 0 and sp >= WINDOW

    rel = jnp.arange(LK)[None, :] - WINDOW - jnp.arange(CHUNK)[:, None]
    bucket = _t5_bucket(rel).astype(jnp.int32)

    kv_w = N_KV * HEAD_DIM
    k_lo, v_lo = D_ATTN, D_ATTN + kv_w
    yp = x_prompt.reshape(bp * sp, D_MODEL)
    ys = x_sample.reshape(bs * ss, D_MODEL)
    outs = [[] for _ in range(6)]
    for l in range(depth):
        w_in_l = w_in[l]
        w = {
            'ffn1_pre': ffn1_pre[l], 'ffn1_post': ffn1_post[l],
            'ffn1_w_gate': ffn1_w_gate[l].astype(BF16), 'ffn1_w_up': ffn1_w_up[l].astype(BF16),
            'ffn1_w_down': ffn1_w_down[l].astype(BF16),
            'mix_pre': mix_pre[l], 'mix_post': mix_post[l],
            'w_in': jnp.concatenate([w_in_l[:, :k_lo], w_in_l[:, v_lo + kv_w:], w_in_l[:, k_lo:v_lo + kv_w]],
                                    axis=1).astype(BF16),
            'w_out': w_out[l].astype(BF16), 'attn_sinks': attn_sinks[l],
            'hgrn_lb_logits': hgrn_lb_logits, 'hgrn_norm': hgrn_norm[l],
            'ffn2_pre': ffn2_pre[l], 'ffn2_post': ffn2_post[l],
            'ffn2_w_gate': ffn2_w_gate[l].astype(BF16), 'ffn2_w_up': ffn2_w_up[l].astype(BF16),
            'ffn2_w_down': ffn2_w_down[l].astype(BF16),
            'ple_pre': ple_pre[l], 'ple_post': ple_post[l],
            'w_ple_gate': w_ple_gate[l].astype(BF16), 'w_ple_proj': w_ple_proj[l].astype(BF16),
        }
        yp, kv_p, st_p = _prompt_layer(yp, p_prompt[l].reshape(bp * sp, PLE_DIM), w, rel_bias_table, bucket, l,
                                       n_seq=bp, seq_len=sp)
        ys, kv_s, st_s = _sample_layer(ys, p_sample[l].reshape(bs * ss, PLE_DIM), w, rel_bias_table, bucket, l,
                                       n_seq=bs, s0=state_hgrn[l], k_cache=cache_attn_k[l],
                                       v_cache=cache_attn_v[l])
        kv_p = kv_p.reshape(bp, sp, 2 * kv_w)[:, sp - WINDOW:].reshape(bp, WINDOW, 2, N_KV, HEAD_DIM)
        outs[0].append(kv_p[:, :, 0])
        outs[1].append(kv_p[:, :, 1])
        outs[2].append(st_p)
        kv_s = kv_s.reshape(bs, ss, 2, N_KV, HEAD_DIM)
        outs[3].append(jnp.concatenate([cache_attn_k[l], kv_s[:, :, 0]], axis=1)[:, ss:])
        outs[4].append(jnp.concatenate([cache_attn_v[l], kv_s[:, :, 1]], axis=1)[:, ss:])
        outs[5].append(st_s)
    return (yp.reshape(bp, sp, D_MODEL), ys.reshape(bs, ss, D_MODEL),
            jnp.stack(outs[0]), jnp.stack(outs[1]), jnp.stack(outs[2]),
            jnp.stack(outs[3]), jnp.stack(outs[4]), jnp.stack(outs[5]))
```

```python
import functools
import math

import jax
import jax.numpy as jnp
import numpy as np
from jax import lax
from jax.experimental import pallas as pl
from jax.experimental.pallas import tpu as pltpu

F32 = jnp.float32
BF16 = jnp.bfloat16

D_MODEL = 2048
CHUNK = 64
D_ATTN = 1024
HEAD_DIM = 128
N_HEADS = 8
N_KV = 2
GROUP = N_HEADS // N_KV
WINDOW = 128
LK = WINDOW + CHUNK
NUM_BUCKETS = 32
MAX_DISTANCE = 128
HG_DK = 128
HG_DV = 128
HG_HEADS = 8
D_HG = HG_HEADS * HG_DV
HG_SUB = 16
HG_SUB_SHIFT = HG_SUB.bit_length() - 1
CHUNK_SHIFT = CHUNK.bit_length() - 1
D_FF = 5632
PLE_DIM = 256
EPS = 1e-6
NEG_INF = -1e30
D_IN = D_ATTN + 2 * N_KV * HEAD_DIM + 4 * D_HG

Z_QH, Z_FH, Z_IH, Z_GH = 1, 2, 3, 4
Z_K, Z_V = 20, 21

VMEM_LIMIT_BYTES = 60 * 1024 * 1024

FFN_TM = 1024
FFN_TF = 512
PROJ_TM = 1024
PROJ_TN = 512
ROW_TM = 1024
SUB_ROWS = 256
NORM_ROWS = 256
MIX_UNIT = 256
MIX_PROJ_TN = 512
MIX_COST_SMALL, MIX_COST_ATTEND, MIX_COST_RECUR_ISSUE, MIX_COST_RECUR_CONSUME = 1.0, 1.0, 0.5, 0.5


def _rms(x, gain):
    y = x * lax.rsqrt(jnp.mean(x * x, axis=-1, keepdims=True) + EPS)
    return y * gain


def _sigmoid(x):
    return 1.0 / (1.0 + jnp.exp(-x))


def _cparams(sem):
    return pltpu.CompilerParams(dimension_semantics=sem, vmem_limit_bytes=VMEM_LIMIT_BYTES)


def _pipelined(n, before, matmul, after):
    if before is not None:
        before(0)
    for r in range(n):
        if before is not None and r + 1 < n:
            before(r + 1)
        matmul(r)
        if after is not None and r > 0:
            after(r - 1)
    if after is not None:
        after(n - 1)


def _ffn_kernel(x_ref, pre_ref, post_ref, wgu_ref, wd_ref, o_ref, xn_ref):
    j = pl.program_id(1)
    last = pl.num_programs(1) - 1
    nsub = x_ref.shape[0] // SUB_ROWS
    sub = [slice(r * SUB_ROWS, (r + 1) * SUB_ROWS) for r in range(nsub)]

    def swiglu(rows):
        gu = jnp.dot(xn_ref[rows, :], wgu_ref[0], preferred_element_type=F32)
        g, u = gu[:, :FFN_TF], gu[:, FFN_TF:]
        h = (g * _sigmoid(g) * u).astype(BF16)
        return jnp.dot(h, wd_ref[...], preferred_element_type=F32)

    @pl.when(j == 0)
    def _():
        def norm(r):
            xn_ref[sub[r], :] = _rms(x_ref[sub[r], :], pre_ref[...]).astype(BF16)

        def first(r):
            o_ref[sub[r], :] = swiglu(sub[r])

        _pipelined(nsub, norm, first, None)

    @pl.when((j > 0) & (j < last))
    def _():
        o_ref[...] += swiglu(slice(None))

    @pl.when(j == last)
    def _():
        ys = [None] * nsub

        def final(r):
            ys[r] = o_ref[sub[r], :] + swiglu(sub[r])

        def finish(r):
            o_ref[sub[r], :] = x_ref[sub[r], :] + 0.5 * _rms(ys[r], post_ref[...])

        _pipelined(nsub, None, final, finish)


def _ffn_gate_up(w_gate, w_up):
    def tiles(w):
        return w.astype(BF16).reshape(D_MODEL, D_FF // FFN_TF, FFN_TF).transpose(1, 0, 2)
    return jnp.concatenate([tiles(w_gate), tiles(w_up)], axis=2)


def _ffn(x, pre, post, wgu, wd):
    t = x.shape[0]
    tm = min(FFN_TM, t)
    grid = (t // tm, D_FF // FFN_TF)
    assert grid[1] >= 3
    return pl.pallas_call(
        _ffn_kernel,
        grid=grid,
        in_specs=[
            pl.BlockSpec((tm, D_MODEL), lambda i, j: (i, 0)),
            pl.BlockSpec((1, D_MODEL), lambda i, j: (0, 0)),
            pl.BlockSpec((1, D_MODEL), lambda i, j: (0, 0)),
            pl.BlockSpec((1, D_MODEL, 2 * FFN_TF), lambda i, j: (j, 0, 0)),
            pl.BlockSpec((FFN_TF, D_MODEL), lambda i, j: (j, 0)),
        ],
        out_specs=pl.BlockSpec((tm, D_MODEL), lambda i, j: (i, 0)),
        out_shape=jax.ShapeDtypeStruct((t, D_MODEL), F32),
        scratch_shapes=[pltpu.VMEM((tm, D_MODEL), BF16)],
        compiler_params=_cparams(("parallel", "arbitrary")),
        name="ffn_half_step",
    )(x, pre, post, wgu, wd)


def _inproj_kernel(x_ref, pre_ref, w_ref, z_ref, xn_ref):
    j = pl.program_id(1)
    tm = x_ref.shape[0]

    @pl.when(j == 0)
    def _():
        def body(r, c):
            rows = pl.ds(pl.multiple_of(r * NORM_ROWS, NORM_ROWS), NORM_ROWS)
            xn_ref[rows, :] = _rms(x_ref[rows, :], pre_ref[...]).astype(BF16)
            return c
        lax.fori_loop(0, tm // NORM_ROWS, body, 0)

    z_ref[...] = jnp.dot(xn_ref[...], w_ref[...], preferred_element_type=F32)


def _inproj(x, pre, w_in):
    t = x.shape[0]
    tm = min(PROJ_TM, t)
    grid = (t // tm, D_IN // PROJ_TN)
    return pl.pallas_call(
        _inproj_kernel,
        grid=grid,
        in_specs=[
            pl.BlockSpec((tm, D_MODEL), lambda i, j: (i, 0)),
            pl.BlockSpec((1, D_MODEL), lambda i, j: (0, 0)),
            pl.BlockSpec((D_MODEL, PROJ_TN), lambda i, j: (0, j)),
        ],
        out_specs=pl.BlockSpec((tm, PROJ_TN), lambda i, j: (i, j)),
        out_shape=jax.ShapeDtypeStruct((t, D_IN), F32),
        scratch_shapes=[pltpu.VMEM((tm, D_MODEL), BF16)],
        compiler_params=_cparams(("parallel", "arbitrary")),
        name="mixer_in_proj",
    )(x, pre, w_in)


def _fill_bias(bucket_ref, table_ref, bias_ref):
    bucket = bucket_ref[...]
    for h in range(N_HEADS):
        acc = jnp.zeros((CHUNK, LK), F32)
        for n in range(NUM_BUCKETS):
            acc = jnp.where(bucket == n, table_ref[n, h], acc)
        g, hh = divmod(h, GROUP)
        bias_ref[g, hh * CHUNK:(hh + 1) * CHUNK, :] = acc


def _sink_column(sinks_ref, g):
    row = lax.broadcasted_iota(jnp.int32, (GROUP * CHUNK, 1), 0)
    sink = jnp.zeros((GROUP * CHUNK, 1), F32)
    for hh in range(GROUP):
        sink = jnp.where((row >> CHUNK_SHIFT) == hh, sinks_ref[g * GROUP + hh], sink)
    return sink


def _attend_chunk(q_of_head, kk, vv, bias_g, sink, first_key_pos):
    qs = jnp.concatenate([q_of_head(hh) for hh in range(GROUP)], axis=0).astype(BF16)
    s = lax.dot_general(qs, kk, (((1,), (1,)), ((), ())), preferred_element_type=F32)
    s = s * (HEAD_DIM ** -0.5) + bias_g
    if first_key_pos is not None:
        col = lax.broadcasted_iota(jnp.int32, (GROUP * CHUNK, LK), 1)
        s = jnp.where(col + first_key_pos >= 0, s, NEG_INF)
    m = jnp.maximum(jnp.max(s, axis=-1, keepdims=True), sink)
    e = jnp.exp(s - m)
    p = e / (jnp.sum(e, axis=-1, keepdims=True) + jnp.exp(sink - m))
    return jnp.dot(p.astype(BF16), vv, preferred_element_type=F32)


def _attend_stages(q_ref, o_ref, k_all, v_all, bias_ref, sinks_ref, pos0):
    def stage(c, g):
        rows = slice(c * CHUNK, (c + 1) * CHUNK)
        cols = slice(g * HEAD_DIM, (g + 1) * HEAD_DIM)
        masked = pos0 is not None and c * CHUNK < WINDOW

        def q_of_head(hh):
            h = g * GROUP + hh
            return q_ref[rows, h * HEAD_DIM:(h + 1) * HEAD_DIM]

        o = _attend_chunk(q_of_head, k_all[c * CHUNK:c * CHUNK + LK, cols],
                          v_all[c * CHUNK:c * CHUNK + LK, cols], bias_ref[g], _sink_column(sinks_ref, g),
                          pos0 + (c * CHUNK - WINDOW) if masked else None)
        for hh in range(GROUP):
            h = g * GROUP + hh
            o_ref[rows, h * HEAD_DIM:(h + 1) * HEAD_DIM] = o[hh * CHUNK:(hh + 1) * CHUNK, :].astype(BF16)

    return [functools.partial(stage, c, g) for c in range(q_ref.shape[0] // CHUNK) for g in range(N_KV)]


def _interleave(a, b):
    total_a = sum(c for c, _ in a)
    total_b = sum(c for c, _ in b)
    ia = ib = 0
    done_a = done_b = 0.0
    while ia < len(a) or ib < len(b):
        if ib >= len(b) or (ia < len(a) and done_a * total_b <= done_b * total_a):
            done_a += a[ia][0]
            a[ia][1]()
            ia += 1
        else:
            done_b += b[ib][0]
            b[ib][1]()
            ib += 1


def _attn_kernel(bucket_ref, table_ref, sinks_ref, q_ref, kc_ref, vc_ref, kp_ref, vp_ref,
                 o_ref, bias_ref, k_all, v_all):
    @pl.when(pl.program_id(0) == 0)
    def _():
        _fill_bias(bucket_ref, table_ref, bias_ref)

    k_all[0:WINDOW, :] = kp_ref[...].astype(BF16)
    v_all[0:WINDOW, :] = vp_ref[...].astype(BF16)
    k_all[WINDOW:, :] = kc_ref[...].astype(BF16)
    v_all[WINDOW:, :] = vc_ref[...].astype(BF16)
    for stage in _attend_stages(q_ref, o_ref, k_all, v_all, bias_ref, sinks_ref, None):
        stage()


def _attention_cached(bucket, table, sinks, z, k_prev, v_prev, *, n_seq):
    kv_w = N_KV * HEAD_DIM
    smem = pl.BlockSpec(memory_space=pltpu.SMEM)
    return pl.pallas_call(
        _attn_kernel,
        grid=(n_seq,),
        in_specs=[
            pl.BlockSpec((CHUNK, LK), lambda b: (0, 0)),
            smem,
            smem,
            pl.BlockSpec((CHUNK, D_ATTN), lambda b: (b, 0)),
            pl.BlockSpec((CHUNK, kv_w), lambda b: (b, Z_K)),
            pl.BlockSpec((CHUNK, kv_w), lambda b: (b, Z_V)),
            pl.BlockSpec((WINDOW, kv_w), lambda b: (b, 0)),
            pl.BlockSpec((WINDOW, kv_w), lambda b: (b, 0)),
        ],
        out_specs=pl.BlockSpec((CHUNK, D_ATTN), lambda b: (b, 0)),
        out_shape=jax.ShapeDtypeStruct((n_seq * CHUNK, D_ATTN), BF16),
        scratch_shapes=[
            pltpu.VMEM((N_KV, GROUP * CHUNK, LK), F32),
            pltpu.VMEM((WINDOW + CHUNK, kv_w), BF16),
            pltpu.VMEM((WINDOW + CHUNK, kv_w), BF16),
        ],
        compiler_params=_cparams(("arbitrary",)),
        name="swa_attention",
    )(bucket, table, sinks, z, z, z, k_prev, v_prev)


_HG_RANGES = [(0, 16), (0, 32), (0, 48), (0, 64), (16, 32), (32, 48), (48, 64), (16, 48), (16, 64), (32, 64)]
_HG_RANGE_ROWS = 16


def _hgrn_selector():
    t = np.arange(CHUNK)
    within = (t[:, None] // HG_SUB == t[None, :] // HG_SUB) & (t[None, :] <= t[:, None])
    ranges = np.zeros((_HG_RANGE_ROWS, CHUNK), bool)
    for n, (lo, hi) in enumerate(_HG_RANGES):
        ranges[n, lo:hi] = True
    once = np.concatenate([within, ranges], axis=0)
    return jnp.asarray(np.concatenate([once, once, once], axis=1), BF16)


def _lower_bound(lbl_ref, layer):
    logits = lbl_ref[...]
    ex = jnp.exp(logits - jnp.max(logits, axis=0, keepdims=True))
    return jnp.sum(ex[:layer + 1], axis=0, keepdims=True) / jnp.sum(ex, axis=0, keepdims=True)


def _hgrn_stages(load, sel, lb, gain, st_ref, store):
    ctx = {}

    def issue():
        ctx['issued'] = _hgrn_issue(*load(), sel, lb, st_ref)

    def consume():
        for h, out in enumerate(_hgrn_consume(*ctx['issued'], gain, st_ref)):
            store(h, out)

    return [issue, consume]


def _hgrn_issue(q, fl, iv, gate, sel, lb, st_ref):
    nsub = CHUNK // HG_SUB
    s1, s2, s3, s4 = HG_SUB, 2 * HG_SUB, 3 * HG_SUB, 4 * HG_SUB
    zeros_sub = jnp.zeros((HG_SUB, HG_DK), BF16)

    def sub(a, j):
        return a[j * HG_SUB:(j + 1) * HG_SUB]

    f = lb + (1.0 - lb) * _sigmoid(fl)
    lf = jnp.log(f)
    k = 1.0 - f
    hi = lf.astype(BF16)
    r1 = lf - hi.astype(F32)
    mid = r1.astype(BF16)
    lo = (r1 - mid.astype(F32)).astype(BF16)
    cr = jnp.dot(sel, jnp.concatenate([hi, mid, lo], axis=0), preferred_element_type=F32)
    cc = cr[:CHUNK]
    ev = jnp.exp(cr[CHUNK:])
    qd = q * jnp.exp(cc)
    ki = k * jnp.exp(-cc)
    iv = iv.astype(BF16)
    gate = gate * _sigmoid(gate)
    head_cols = [slice(h * HG_DK, (h + 1) * HG_DK) for h in range(HG_HEADS)]

    def dk(sl, lo_row, hi_row):
        n = _HG_RANGES.index((lo_row, hi_row))
        return ev[n:n + 1, sl]

    scores, inter, upds = [], [], []
    for h, sl in enumerate(head_cols):
        qd_h, ki_h = qd[:, sl], ki[:, sl]
        qd_b = qd_h.astype(BF16)
        ki_b = [sub(ki_h, j).astype(BF16) for j in range(nsub)]
        kend = [sub(ki_h, 0) * dk(sl, 0, s1), sub(ki_h, 1) * dk(sl, s1, s2),
                sub(ki_h, 2) * dk(sl, s2, s3), sub(ki_h, 3) * dk(sl, s3, s4)]
        kend_b = [kend[j].astype(BF16) for j in range(nsub - 1)]
        kmat = [
            [ki_b[0], zeros_sub, zeros_sub, zeros_sub],
            [kend_b[0], ki_b[1], zeros_sub, zeros_sub],
            [(kend[0] * dk(sl, s1, s2)).astype(BF16), kend_b[1], ki_b[2], zeros_sub],
            [(kend[0] * dk(sl, s1, s3)).astype(BF16), (kend[1] * dk(sl, s2, s3)).astype(BF16),
             kend_b[2], ki_b[3]],
        ]
        scores.append(lax.dot_general(qd_b, jnp.concatenate([p for row in kmat for p in row], axis=0),
                                      (((1,), (1,)), ((), ())), preferred_element_type=F32))
        qe = jnp.concatenate([sub(qd_h, 0), sub(qd_h, 1) * dk(sl, 0, s1), sub(qd_h, 2) * dk(sl, 0, s2),
                              sub(qd_h, 3) * dk(sl, 0, s3)], axis=0).astype(BF16)
        ke = jnp.concatenate([kend[0] * dk(sl, s1, s4), kend[1] * dk(sl, s2, s4),
                              kend[2] * dk(sl, s3, s4), kend[3]], axis=0).astype(BF16)
        inter.append(lax.dot_general(qe, st_ref[h].astype(BF16), (((1,), (1,)), ((), ())),
                                     preferred_element_type=F32))
        upds.append(lax.dot_general(iv[:, sl], ke, (((0,), (0,)), ((), ())),
                                    preferred_element_type=F32))
    return scores, inter, upds, iv, gate, ev[_HG_RANGES.index((0, s4)):_HG_RANGES.index((0, s4)) + 1]


def _hgrn_consume(scores, inter, upds, iv, gate, chunk_decay, gain, st_ref):
    nsub = CHUNK // HG_SUB
    t_i = lax.broadcasted_iota(jnp.int32, (CHUNK, nsub * CHUNK), 0)
    c_i = lax.broadcasted_iota(jnp.int32, (CHUNK, nsub * CHUNK), 1)
    keep = ((t_i >> HG_SUB_SHIFT) == (c_i >> CHUNK_SHIFT)) & ((c_i & (CHUNK - 1)) <= t_i)
    head_cols = [slice(h * HG_DK, (h + 1) * HG_DK) for h in range(HG_HEADS)]
    outs = []
    for h, sl in enumerate(head_cols):
        a = jnp.where(keep, scores[h], 0.0).astype(BF16)
        outs.append(inter[h] + jnp.dot(a, jnp.concatenate([iv[:, sl]] * nsub, axis=0),
                                       preferred_element_type=F32))
        st_ref[h] = st_ref[h] * chunk_decay[:, sl] + upds[h]
    return [(_rms(outs[h], gain) * gate[:, sl]).astype(BF16) for h, sl in enumerate(head_cols)]


def _hgrn_kernel(sel_ref, lbl_ref, gain_ref, s0_ref, q_ref, f_ref, i_ref, g_ref, rec_ref, sout_ref,
                 st_ref, *, layer):
    for h in range(HG_HEADS):
        st_ref[h] = s0_ref[0, h].T

    def load():
        return q_ref[...], f_ref[...], i_ref[...], g_ref[...]

    def store(h, out):
        rec_ref[:, h * HG_DV:(h + 1) * HG_DV] = out

    for stage in _hgrn_stages(load, sel_ref[...], _lower_bound(lbl_ref, layer), gain_ref[...], st_ref, store):
        stage()
    for h in range(HG_HEADS):
        sout_ref[0, h] = st_ref[h].T


def _hgrn_single_chunk(lb_logits, gain, s0, z, *, n_seq, layer):
    n_layers = lb_logits.shape[0]
    sel = _hgrn_selector()

    def zspec(col):
        return pl.BlockSpec((CHUNK, D_HG), lambda b: (b, col))

    state_spec = pl.BlockSpec((1, HG_HEADS, HG_DK, HG_DV), lambda b: (b, 0, 0, 0))
    return pl.pallas_call(
        functools.partial(_hgrn_kernel, layer=layer),
        grid=(n_seq,),
        in_specs=[
            pl.BlockSpec(sel.shape, lambda b: (0, 0)),
            pl.BlockSpec((n_layers, D_HG), lambda b: (0, 0)),
            pl.BlockSpec((1, HG_DV), lambda b: (0, 0)),
            state_spec,
            zspec(Z_QH), zspec(Z_FH), zspec(Z_IH), zspec(Z_GH),
        ],
        out_specs=[pl.BlockSpec((CHUNK, D_HG), lambda b: (b, 0)), state_spec],
        out_shape=[
            jax.ShapeDtypeStruct((n_seq * CHUNK, D_HG), BF16),
            jax.ShapeDtypeStruct((n_seq, HG_HEADS, HG_DK, HG_DV), F32),
        ],
        scratch_shapes=[pltpu.VMEM((HG_HEADS, HG_DV, HG_DK), F32)],
        compiler_params=_cparams(("arbitrary",)),
        name="hgrn2_mixer",
    )(sel, lb_logits, gain, s0, z, z, z, z)


def _mixer_kernel(bucket_ref, table_ref, sinks_ref, sel_ref, lbl_ref, gain_ref, pre_ref, x_ref, w_ref,
                  attn_ref, rec_ref, kv_ref, sout_ref,
                  bias_ref, z_even, z_odd, xn_ref, k_all, v_all, st_ref, *, layer, units_per_seq):
    s = pl.program_id(0)
    unit = MIX_UNIT
    kv_w = N_KV * HEAD_DIM
    u_in_seq = (s + (units_per_seq - 1)) % units_per_seq

    @pl.when(s == 0)
    def _():
        _fill_bias(bucket_ref, table_ref, bias_ref)
        z_odd[...] = jnp.zeros(z_odd.shape, F32)
        k_all[0:WINDOW, :] = jnp.zeros((WINDOW, kv_w), BF16)
        v_all[0:WINDOW, :] = jnp.zeros((WINDOW, kv_w), BF16)

    @pl.when((s == 0) | (u_in_seq == 0))
    def _():
        st_ref[...] = jnp.zeros(st_ref.shape, F32)

    def step(z_write, z_read):
        xn_ref[...] = _rms(x_ref[...], pre_ref[...]).astype(BF16)

        def project(c):
            cols = slice(c * MIX_PROJ_TN, (c + 1) * MIX_PROJ_TN)
            z_write[:, cols] = jnp.dot(xn_ref[...], w_ref[:, cols], preferred_element_type=F32)

        k_cols = slice(Z_K * kv_w, (Z_K + 1) * kv_w)
        v_cols = slice(Z_V * kv_w, (Z_V + 1) * kv_w)
        lb = _lower_bound(lbl_ref, layer)

        def load_kv():
            k_all[WINDOW:, :] = z_read[:, k_cols].astype(BF16)
            v_all[WINDOW:, :] = z_read[:, v_cols].astype(BF16)
            kv_ref[:, 0:kv_w] = z_read[:, k_cols]
            kv_ref[:, kv_w:] = z_read[:, v_cols]

        def keep_history():
            k_all[0:WINDOW, :] = k_all[unit:unit + WINDOW, :]
            v_all[0:WINDOW, :] = v_all[unit:unit + WINDOW, :]

        def recur_stages(c):
            rows = slice(c * CHUNK, (c + 1) * CHUNK)

            def load():
                return tuple(z_read[rows, col * D_HG:(col + 1) * D_HG] for col in (Z_QH, Z_FH, Z_IH, Z_GH))

            def store(h, out):
                rec_ref[rows, h * HG_DV:(h + 1) * HG_DV] = out

            return _hgrn_stages(load, sel_ref[...], lb, gain_ref[...], st_ref, store)

        attend = _attend_stages(z_read, attn_ref, k_all, v_all, bias_ref, sinks_ref, u_in_seq * unit)
        mixers = ([(MIX_COST_SMALL, load_kv)] + [(MIX_COST_ATTEND, t) for t in attend]
                  + [(MIX_COST_SMALL, keep_history)])
        for c in range(unit // CHUNK):
            issue, consume = recur_stages(c)
            mixers += [(MIX_COST_RECUR_ISSUE, issue), (MIX_COST_RECUR_CONSUME, consume)]
        pieces = [(1.0, functools.partial(project, c)) for c in range(D_IN // MIX_PROJ_TN)]
        _interleave(pieces, mixers)

    @pl.when(s % 2 == 0)
    def _():
        step(z_even, z_odd)

    @pl.when(s % 2 == 1)
    def _():
        step(z_odd, z_even)

    @pl.when((s > 0) & (u_in_seq == units_per_seq - 1))
    def _():
        for h in range(HG_HEADS):
            sout_ref[0, h] = st_ref[h].T


def _mixer_front(x, pre, w_in, bucket, table, sinks, lb_logits, gain, *, n_seq, seq_len, layer):
    unit = MIX_UNIT
    ups = seq_len // unit
    n_units = n_seq * ups
    n_layers = lb_logits.shape[0]
    kv_w = N_KV * HEAD_DIM
    sel = _hgrn_selector()
    smem = pl.BlockSpec(memory_space=pltpu.SMEM)

    def const(shape):
        return pl.BlockSpec(shape, lambda s: (0,) * len(shape))

    def prev_unit(s):
        return (jnp.maximum(s - 1, 0), 0)

    return pl.pallas_call(
        functools.partial(_mixer_kernel, layer=layer, units_per_seq=ups),
        grid=(n_units + 1,),
        in_specs=[
            const((CHUNK, LK)), smem, smem, const(sel.shape), const((n_layers, D_HG)),
            const((1, HG_DV)), const((1, D_MODEL)),
            pl.BlockSpec((unit, D_MODEL), lambda s: (jnp.minimum(s, n_units - 1), 0)),
            pl.BlockSpec((D_MODEL, D_IN), lambda s: (0, 0), pipeline_mode=pl.Buffered(1)),
        ],
        out_specs=[
            pl.BlockSpec((unit, D_ATTN), prev_unit),
            pl.BlockSpec((unit, D_HG), prev_unit),
            pl.BlockSpec((unit, 2 * kv_w), prev_unit),
            pl.BlockSpec((1, HG_HEADS, HG_DK, HG_DV), lambda s: (jnp.maximum(s - 1, 0) // ups, 0, 0, 0)),
        ],
        out_shape=[
            jax.ShapeDtypeStruct((n_units * unit, D_ATTN), BF16),
            jax.ShapeDtypeStruct((n_units * unit, D_HG), BF16),
            jax.ShapeDtypeStruct((n_units * unit, 2 * kv_w), F32),
            jax.ShapeDtypeStruct((n_seq, HG_HEADS, HG_DK, HG_DV), F32),
        ],
        scratch_shapes=[
            pltpu.VMEM((N_KV, GROUP * CHUNK, LK), F32),
            pltpu.VMEM((unit, D_IN), F32),
            pltpu.VMEM((unit, D_IN), F32),
            pltpu.VMEM((unit, D_MODEL), BF16),
            pltpu.VMEM((WINDOW + unit, kv_w), BF16),
            pltpu.VMEM((WINDOW + unit, kv_w), BF16),
            pltpu.VMEM((HG_HEADS, HG_DV, HG_DK), F32),
        ],
        compiler_params=_cparams(("arbitrary",)),
        name="mixer_front",
    )(bucket, table, sinks, sel, lb_logits, gain, pre, x, w_in)


def _outproj_kernel(x_ref, a_ref, r_ref, w_ref, post_ref, o_ref):
    nsub = x_ref.shape[0] // SUB_ROWS
    sub = [slice(r * SUB_ROWS, (r + 1) * SUB_ROWS) for r in range(nsub)]
    mix = [None] * nsub

    def project(r):
        mix[r] = (jnp.dot(a_ref[sub[r], :], w_ref[0:D_ATTN, :], preferred_element_type=F32)
                  + jnp.dot(r_ref[sub[r], :], w_ref[D_ATTN:, :], preferred_element_type=F32))

    def finish(r):
        o_ref[sub[r], :] = x_ref[sub[r], :] + _rms(mix[r], post_ref[...])

    _pipelined(nsub, None, project, finish)


def _outproj(x, attn, rec, w_out, post):
    t = x.shape[0]
    tm = min(ROW_TM, t)
    return pl.pallas_call(
        _outproj_kernel,
        grid=(t // tm,),
        in_specs=[
            pl.BlockSpec((tm, D_MODEL), lambda i: (i, 0)),
            pl.BlockSpec((tm, D_ATTN), lambda i: (i, 0)),
            pl.BlockSpec((tm, D_HG), lambda i: (i, 0)),
            pl.BlockSpec((D_MODEL, D_MODEL), lambda i: (0, 0), pipeline_mode=pl.Buffered(1)),
            pl.BlockSpec((1, D_MODEL), lambda i: (0, 0)),
        ],
        out_specs=pl.BlockSpec((tm, D_MODEL), lambda i: (i, 0)),
        out_shape=jax.ShapeDtypeStruct((t, D_MODEL), F32),
        compiler_params=_cparams(("parallel",)),
        name="mixer_out_proj",
    )(x, attn, rec, w_out, post)


def _ple_kernel(x_ref, p_ref, pre_ref, post_ref, wg_ref, wp_ref, o_ref):
    nsub = x_ref.shape[0] // SUB_ROWS
    sub = [slice(r * SUB_ROWS, (r + 1) * SUB_ROWS) for r in range(nsub)]
    hs, gates, projs = [None] * nsub, [None] * nsub, [None] * nsub

    def norm(r):
        hs[r] = _rms(x_ref[sub[r], :], pre_ref[...]).astype(BF16)

    def project(r):
        gates[r] = jnp.dot(hs[r], wg_ref[...], preferred_element_type=F32)
        projs[r] = jnp.dot(p_ref[sub[r], :].astype(BF16), wp_ref[...], preferred_element_type=F32)

    def finish(r):
        o_ref[sub[r], :] = x_ref[sub[r], :] + _rms(_sigmoid(gates[r]) * projs[r], post_ref[...])

    _pipelined(nsub, norm, project, finish)


def _ple(x, p, pre, post, w_gate, w_proj):
    t = x.shape[0]
    tm = min(ROW_TM, t)
    return pl.pallas_call(
        _ple_kernel,
        grid=(t // tm,),
        in_specs=[
            pl.BlockSpec((tm, D_MODEL), lambda i: (i, 0)),
            pl.BlockSpec((tm, PLE_DIM), lambda i: (i, 0)),
            pl.BlockSpec((1, D_MODEL), lambda i: (0, 0)),
            pl.BlockSpec((1, D_MODEL), lambda i: (0, 0)),
            pl.BlockSpec((D_MODEL, D_MODEL), lambda i: (0, 0), pipeline_mode=pl.Buffered(1)),
            pl.BlockSpec((PLE_DIM, D_MODEL), lambda i: (0, 0)),
        ],
        out_specs=pl.BlockSpec((tm, D_MODEL), lambda i: (i, 0)),
        out_shape=jax.ShapeDtypeStruct((t, D_MODEL), F32),
        compiler_params=_cparams(("parallel",)),
        name="ple_embed",
    )(x, p, pre, post, w_gate, w_proj)


def _t5_bucket(rel):
    half = NUM_BUCKETS // 2
    max_exact = half // 2
    n = jnp.abs(rel)
    nf = jnp.maximum(n, 1).astype(jnp.float32)
    large = max_exact + (jnp.log(nf / max_exact) / math.log(MAX_DISTANCE / max_exact)
                         * (half - max_exact)).astype(jnp.int32)
    large = jnp.minimum(large, half - 1)
    return jnp.where(rel > 0, half, 0) + jnp.where(n < max_exact, n, large)


def _ffn1(x, w):
    row = lambda v: v.reshape(1, -1)
    return _ffn(x, row(w['ffn1_pre']), row(w['ffn1_post']), w['ffn1_w_gate_up'], w['ffn1_w_down'])


def _back_half(x, p, attn, rec, w):
    row = lambda v: v.reshape(1, -1)
    x = _outproj(x, attn, rec, w['w_out'], row(w['mix_post']))
    x = _ffn(x, row(w['ffn2_pre']), row(w['ffn2_post']), w['ffn2_w_gate_up'], w['ffn2_w_down'])
    return _ple(x, p, row(w['ple_pre']), row(w['ple_post']), w['w_ple_gate'], w['w_ple_proj'])


def _prompt_layer(x, p, w, table, bucket, layer, *, n_seq, seq_len):
    x = _ffn1(x, w)
    attn, rec, kv, s_new = _mixer_front(
        x, w['mix_pre'].reshape(1, -1), w['w_in'], bucket, table, w['attn_sinks'], w['hgrn_lb_logits'],
        w['hgrn_norm'].reshape(1, -1), n_seq=n_seq, seq_len=seq_len, layer=layer)
    return _back_half(x, p, attn, rec, w), kv, s_new


def _sample_layer(x, p, w, table, bucket, layer, *, n_seq, s0, k_cache, v_cache):
    kv_w = N_KV * HEAD_DIM
    x = _ffn1(x, w)
    z = _inproj(x, w['mix_pre'].reshape(1, -1), w['w_in'])
    attn = _attention_cached(bucket, table, w['attn_sinks'], z, k_cache.reshape(n_seq * WINDOW, kv_w),
                             v_cache.reshape(n_seq * WINDOW, kv_w), n_seq=n_seq)
    rec, s_new = _hgrn_single_chunk(w['hgrn_lb_logits'], w['hgrn_norm'].reshape(1, -1), s0, z,
                                    n_seq=n_seq, layer=layer)
    return _back_half(x, p, attn, rec, w), z[:, Z_K * kv_w:], s_new


def kernel(x_prompt, x_sample, cache_attn_k, cache_attn_v, state_hgrn, p_prompt, p_sample,
           rel_bias_table, ffn1_pre, ffn1_post, ffn1_w_gate, ffn1_w_up, ffn1_w_down,
           mix_pre, mix_post, w_in, w_out, attn_sinks, hgrn_lb_logits, hgrn_norm,
           ffn2_pre, ffn2_post, ffn2_w_gate, ffn2_w_up, ffn2_w_down,
           ple_pre, ple_post, w_ple_gate, w_ple_proj):
    depth = w_in.shape[0]
    bp, sp, _ = x_prompt.shape
    bs, ss, _ = x_sample.shape
    wc = cache_attn_k.shape[2]
    assert wc == WINDOW and ss == CHUNK and sp % MIX_UNIT == 0 and sp >= WINDOW

    rel = jnp.arange(LK)[None, :] - WINDOW - jnp.arange(CHUNK)[:, None]
    bucket = _t5_bucket(rel).astype(jnp.int32)

    kv_w = N_KV * HEAD_DIM
    k_lo, v_lo = D_ATTN, D_ATTN + kv_w
    yp = x_prompt.reshape(bp * sp, D_MODEL)
    ys = x_sample.reshape(bs * ss, D_MODEL)
    outs = [[] for _ in range(6)]
    for l in range(depth):
        w_in_l = w_in[l]
        w = {
            'ffn1_pre': ffn1_pre[l], 'ffn1_post': ffn1_post[l],
            'ffn1_w_gate_up': _ffn_gate_up(ffn1_w_gate[l], ffn1_w_up[l]),
            'ffn1_w_down': ffn1_w_down[l].astype(BF16),
            'mix_pre': mix_pre[l], 'mix_post': mix_post[l],
            'w_in': jnp.concatenate([w_in_l[:, :k_lo], w_in_l[:, v_lo + kv_w:], w_in_l[:, k_lo:v_lo + kv_w]],
                                    axis=1).astype(BF16),
            'w_out': w_out[l].astype(BF16), 'attn_sinks': attn_sinks[l],
            'hgrn_lb_logits': hgrn_lb_logits, 'hgrn_norm': hgrn_norm[l],
            'ffn2_pre': ffn2_pre[l], 'ffn2_post': ffn2_post[l],
            'ffn2_w_gate_up': _ffn_gate_up(ffn2_w_gate[l], ffn2_w_up[l]),
            'ffn2_w_down': ffn2_w_down[l].astype(BF16),
            'ple_pre': ple_pre[l], 'ple_post': ple_post[l],
            'w_ple_gate': w_ple_gate[l].astype(BF16), 'w_ple_proj': w_ple_proj[l].astype(BF16),
        }
        yp, kv_p, st_p = _prompt_layer(yp, p_prompt[l].reshape(bp * sp, PLE_DIM), w, rel_bias_table, bucket, l,
                                       n_seq=bp, seq_len=sp)
        ys, kv_s, st_s = _sample_layer(ys, p_sample[l].reshape(bs * ss, PLE_DIM), w, rel_bias_table, bucket, l,
                                       n_seq=bs, s0=state_hgrn[l], k_cache=cache_attn_k[l],
                                       v_cache=cache_attn_v[l])
        kv_p = kv_p.reshape(bp, sp, 2 * kv_w)[:, sp - WINDOW:].reshape(bp, WINDOW, 2, N_KV, HEAD_DIM)
        outs[0].append(kv_p[:, :, 0])
        outs[1].append(kv_p[:, :, 1])
        outs[2].append(st_p)
        kv_s = kv_s.reshape(bs, ss, 2, N_KV, HEAD_DIM)
        outs[3].append(jnp.concatenate([cache_attn_k[l], kv_s[:, :, 0]], axis=1)[:, ss:])
        outs[4].append(jnp.concatenate([cache_attn_v[l], kv_s[:, :, 1]], axis=1)[:, ss:])
        outs[5].append(st_s)
    return (yp.reshape(bp, sp, D_MODEL), ys.reshape(bs, ss, D_MODEL),
            jnp.stack(outs[0]), jnp.stack(outs[1]), jnp.stack(outs[2]),
            jnp.stack(outs[3]), jnp.stack(outs[4]), jnp.stack(outs[5]))
```

```python
import functools
import math

import jax
import jax.numpy as jnp
import numpy as np
from jax import lax
from jax.experimental import pallas as pl
from jax.experimental.pallas import tpu as pltpu

F32 = jnp.float32
BF16 = jnp.bfloat16

D_MODEL = 2048
CHUNK = 64
D_ATTN = 1024
HEAD_DIM = 128
N_HEADS = 8
N_KV = 2
GROUP = N_HEADS // N_KV
WINDOW = 128
LK = WINDOW + CHUNK
NUM_BUCKETS = 32
MAX_DISTANCE = 128
HG_DK = 128
HG_DV = 128
HG_HEADS = 8
D_HG = HG_HEADS * HG_DV
HG_SUB = 16
HG_SUB_SHIFT = HG_SUB.bit_length() - 1
CHUNK_SHIFT = CHUNK.bit_length() - 1
D_FF = 5632
PLE_DIM = 256
EPS = 1e-6
NEG_INF = -1e30
D_IN = D_ATTN + 2 * N_KV * HEAD_DIM + 4 * D_HG

Z_QH, Z_FH, Z_IH, Z_GH = 1, 2, 3, 4
Z_K, Z_V = 20, 21

VMEM_LIMIT_BYTES = 60 * 1024 * 1024

FFN_TM = 1024
FFN_TF = 512
PROJ_TM = 1024
PROJ_TN = 512
ROW_TM = 1024
SUB_ROWS = 256
NORM_ROWS = 256
MIX_UNIT = 256
MIX_PROJ_TN = 512


def _rms(x, gain):
    y = x * lax.rsqrt(jnp.mean(x * x, axis=-1, keepdims=True) + EPS)
    return y * gain


def _sigmoid(x):
    return 1.0 / (1.0 + jnp.exp(-x))


def _cparams(sem):
    return pltpu.CompilerParams(dimension_semantics=sem, vmem_limit_bytes=VMEM_LIMIT_BYTES)


def _pipelined(n, before, matmul, after):
    if before is not None:
        before(0)
    for r in range(n):
        if before is not None and r + 1 < n:
            before(r + 1)
        matmul(r)
        if after is not None and r > 0:
            after(r - 1)
    if after is not None:
        after(n - 1)


def _ffn_kernel(x_ref, pre_ref, post_ref, wg_ref, wu_ref, wd_ref, o_ref, xn_ref):
    j = pl.program_id(1)
    last = pl.num_programs(1) - 1
    nsub = x_ref.shape[0] // SUB_ROWS
    sub = [slice(r * SUB_ROWS, (r + 1) * SUB_ROWS) for r in range(nsub)]

    def swiglu(rows):
        xn = xn_ref[rows, :]
        g = jnp.dot(xn, wg_ref[...], preferred_element_type=F32)
        u = jnp.dot(xn, wu_ref[...], preferred_element_type=F32)
        h = (g * _sigmoid(g) * u).astype(BF16)
        return jnp.dot(h, wd_ref[...], preferred_element_type=F32)

    @pl.when(j == 0)
    def _():
        def norm(r):
            xn_ref[sub[r], :] = _rms(x_ref[sub[r], :], pre_ref[...]).astype(BF16)

        def first(r):
            o_ref[sub[r], :] = swiglu(sub[r])

        _pipelined(nsub, norm, first, None)

    @pl.when((j > 0) & (j < last))
    def _():
        o_ref[...] += swiglu(slice(None))

    @pl.when(j == last)
    def _():
        ys = [None] * nsub

        def final(r):
            ys[r] = o_ref[sub[r], :] + swiglu(sub[r])

        def finish(r):
            o_ref[sub[r], :] = x_ref[sub[r], :] + 0.5 * _rms(ys[r], post_ref[...])

        _pipelined(nsub, None, final, finish)


def _ffn(x, pre, post, wg, wu, wd):
    t = x.shape[0]
    tm = min(FFN_TM, t)
    grid = (t // tm, D_FF // FFN_TF)
    assert grid[1] >= 3
    return pl.pallas_call(
        _ffn_kernel,
        grid=grid,
        in_specs=[
            pl.BlockSpec((tm, D_MODEL), lambda i, j: (i, 0)),
            pl.BlockSpec((1, D_MODEL), lambda i, j: (0, 0)),
            pl.BlockSpec((1, D_MODEL), lambda i, j: (0, 0)),
            pl.BlockSpec((D_MODEL, FFN_TF), lambda i, j: (0, j)),
            pl.BlockSpec((D_MODEL, FFN_TF), lambda i, j: (0, j)),
            pl.BlockSpec((FFN_TF, D_MODEL), lambda i, j: (j, 0)),
        ],
        out_specs=pl.BlockSpec((tm, D_MODEL), lambda i, j: (i, 0)),
        out_shape=jax.ShapeDtypeStruct((t, D_MODEL), F32),
        scratch_shapes=[pltpu.VMEM((tm, D_MODEL), BF16)],
        compiler_params=_cparams(("parallel", "arbitrary")),
        name="ffn_half_step",
    )(x, pre, post, wg, wu, wd)


def _inproj_kernel(x_ref, pre_ref, w_ref, z_ref, xn_ref):
    j = pl.program_id(1)
    tm = x_ref.shape[0]

    @pl.when(j == 0)
    def _():
        def body(r, c):
            rows = pl.ds(pl.multiple_of(r * NORM_ROWS, NORM_ROWS), NORM_ROWS)
            xn_ref[rows, :] = _rms(x_ref[rows, :], pre_ref[...]).astype(BF16)
            return c
        lax.fori_loop(0, tm // NORM_ROWS, body, 0)

    z_ref[...] = jnp.dot(xn_ref[...], w_ref[...], preferred_element_type=F32)


def _inproj(x, pre, w_in):
    t = x.shape[0]
    tm = min(PROJ_TM, t)
    grid = (t // tm, D_IN // PROJ_TN)
    return pl.pallas_call(
        _inproj_kernel,
        grid=grid,
        in_specs=[
            pl.BlockSpec((tm, D_MODEL), lambda i, j: (i, 0)),
            pl.BlockSpec((1, D_MODEL), lambda i, j: (0, 0)),
            pl.BlockSpec((D_MODEL, PROJ_TN), lambda i, j: (0, j)),
        ],
        out_specs=pl.BlockSpec((tm, PROJ_TN), lambda i, j: (i, j)),
        out_shape=jax.ShapeDtypeStruct((t, D_IN), F32),
        scratch_shapes=[pltpu.VMEM((tm, D_MODEL), BF16)],
        compiler_params=_cparams(("parallel", "arbitrary")),
        name="mixer_in_proj",
    )(x, pre, w_in)


def _fill_bias(bucket_ref, table_ref, bias_ref):
    bucket = bucket_ref[...]
    for h in range(N_HEADS):
        acc = jnp.zeros((CHUNK, LK), F32)
        for n in range(NUM_BUCKETS):
            acc = jnp.where(bucket == n, table_ref[n, h], acc)
        g, hh = divmod(h, GROUP)
        bias_ref[g, hh * CHUNK:(hh + 1) * CHUNK, :] = acc


def _sink_column(sinks_ref, g):
    row = lax.broadcasted_iota(jnp.int32, (GROUP * CHUNK, 1), 0)
    sink = jnp.zeros((GROUP * CHUNK, 1), F32)
    for hh in range(GROUP):
        sink = jnp.where((row >> CHUNK_SHIFT) == hh, sinks_ref[g * GROUP + hh], sink)
    return sink


def _attend_chunk(q_of_head, kk, vv, bias_g, sink, first_key_pos):
    qs = jnp.concatenate([q_of_head(hh) for hh in range(GROUP)], axis=0).astype(BF16)
    s = lax.dot_general(qs, kk, (((1,), (1,)), ((), ())), preferred_element_type=F32)
    s = s * (HEAD_DIM ** -0.5) + bias_g
    if first_key_pos is not None:
        col = lax.broadcasted_iota(jnp.int32, (GROUP * CHUNK, LK), 1)
        s = jnp.where(col + first_key_pos >= 0, s, NEG_INF)
    m = jnp.maximum(jnp.max(s, axis=-1, keepdims=True), sink)
    e = jnp.exp(s - m)
    p = e / (jnp.sum(e, axis=-1, keepdims=True) + jnp.exp(sink - m))
    return jnp.dot(p.astype(BF16), vv, preferred_element_type=F32)


def _attend_stages(q_ref, o_ref, k_all, v_all, bias_ref, sinks_ref, pos0):
    def stage(c, g):
        rows = slice(c * CHUNK, (c + 1) * CHUNK)
        cols = slice(g * HEAD_DIM, (g + 1) * HEAD_DIM)
        masked = pos0 is not None and c * CHUNK < WINDOW

        def q_of_head(hh):
            h = g * GROUP + hh
            return q_ref[rows, h * HEAD_DIM:(h + 1) * HEAD_DIM]

        o = _attend_chunk(q_of_head, k_all[c * CHUNK:c * CHUNK + LK, cols],
                          v_all[c * CHUNK:c * CHUNK + LK, cols], bias_ref[g], _sink_column(sinks_ref, g),
                          pos0 + (c * CHUNK - WINDOW) if masked else None)
        for hh in range(GROUP):
            h = g * GROUP + hh
            o_ref[rows, h * HEAD_DIM:(h + 1) * HEAD_DIM] = o[hh * CHUNK:(hh + 1) * CHUNK, :].astype(BF16)

    return [functools.partial(stage, c, g) for c in range(q_ref.shape[0] // CHUNK) for g in range(N_KV)]


def _interleave(a, b):
    ia = ib = 0
    while ia < len(a) or ib < len(b):
        if ib >= len(b) or (ia < len(a) and ia * len(b) <= ib * len(a)):
            a[ia]()
            ia += 1
        else:
            b[ib]()
            ib += 1


def _attn_kernel(bucket_ref, table_ref, sinks_ref, q_ref, kc_ref, vc_ref, kp_ref, vp_ref,
                 o_ref, bias_ref, k_all, v_all):
    @pl.when(pl.program_id(0) == 0)
    def _():
        _fill_bias(bucket_ref, table_ref, bias_ref)

    k_all[0:WINDOW, :] = kp_ref[...].astype(BF16)
    v_all[0:WINDOW, :] = vp_ref[...].astype(BF16)
    k_all[WINDOW:, :] = kc_ref[...].astype(BF16)
    v_all[WINDOW:, :] = vc_ref[...].astype(BF16)
    for stage in _attend_stages(q_ref, o_ref, k_all, v_all, bias_ref, sinks_ref, None):
        stage()


def _attention_cached(bucket, table, sinks, z, k_prev, v_prev, *, n_seq):
    kv_w = N_KV * HEAD_DIM
    smem = pl.BlockSpec(memory_space=pltpu.SMEM)
    return pl.pallas_call(
        _attn_kernel,
        grid=(n_seq,),
        in_specs=[
            pl.BlockSpec((CHUNK, LK), lambda b: (0, 0)),
            smem,
            smem,
            pl.BlockSpec((CHUNK, D_ATTN), lambda b: (b, 0)),
            pl.BlockSpec((CHUNK, kv_w), lambda b: (b, Z_K)),
            pl.BlockSpec((CHUNK, kv_w), lambda b: (b, Z_V)),
            pl.BlockSpec((WINDOW, kv_w), lambda b: (b, 0)),
            pl.BlockSpec((WINDOW, kv_w), lambda b: (b, 0)),
        ],
        out_specs=pl.BlockSpec((CHUNK, D_ATTN), lambda b: (b, 0)),
        out_shape=jax.ShapeDtypeStruct((n_seq * CHUNK, D_ATTN), BF16),
        scratch_shapes=[
            pltpu.VMEM((N_KV, GROUP * CHUNK, LK), F32),
            pltpu.VMEM((WINDOW + CHUNK, kv_w), BF16),
            pltpu.VMEM((WINDOW + CHUNK, kv_w), BF16),
        ],
        compiler_params=_cparams(("arbitrary",)),
        name="swa_attention",
    )(bucket, table, sinks, z, z, z, k_prev, v_prev)


_HG_RANGES = [(0, 16), (0, 32), (0, 48), (0, 64), (16, 32), (32, 48), (48, 64), (16, 48), (16, 64), (32, 64)]
_HG_RANGE_ROWS = 16


def _hgrn_selector():
    t = np.arange(CHUNK)
    within = (t[:, None] // HG_SUB == t[None, :] // HG_SUB) & (t[None, :] <= t[:, None])
    ranges = np.zeros((_HG_RANGE_ROWS, CHUNK), bool)
    for n, (lo, hi) in enumerate(_HG_RANGES):
        ranges[n, lo:hi] = True
    once = np.concatenate([within, ranges], axis=0)
    return jnp.asarray(np.concatenate([once, once, once], axis=1), BF16)


def _lower_bound(lbl_ref, layer):
    logits = lbl_ref[...]
    ex = jnp.exp(logits - jnp.max(logits, axis=0, keepdims=True))
    return jnp.sum(ex[:layer + 1], axis=0, keepdims=True) / jnp.sum(ex, axis=0, keepdims=True)


def _hgrn_stages(load, sel, lb, gain, st_ref, store, live=None):
    ctx = {}

    def issue():
        ctx['issued'] = _hgrn_issue(*load(), sel, lb, st_ref)

    def consume():
        for h, out in enumerate(_hgrn_consume(*ctx['issued'], gain, st_ref, live)):
            store(h, out)

    return [issue, consume]


def _hgrn_issue(q, fl, iv, gate, sel, lb, st_ref):
    nsub = CHUNK // HG_SUB
    s1, s2, s3, s4 = HG_SUB, 2 * HG_SUB, 3 * HG_SUB, 4 * HG_SUB
    zeros_sub = jnp.zeros((HG_SUB, HG_DK), BF16)

    def sub(a, j):
        return a[j * HG_SUB:(j + 1) * HG_SUB]

    f = lb + (1.0 - lb) * _sigmoid(fl)
    lf = jnp.log(f)
    k = 1.0 - f
    hi = lf.astype(BF16)
    r1 = lf - hi.astype(F32)
    mid = r1.astype(BF16)
    lo = (r1 - mid.astype(F32)).astype(BF16)
    cr = jnp.dot(sel, jnp.concatenate([hi, mid, lo], axis=0), preferred_element_type=F32)
    cc = cr[:CHUNK]
    ev = jnp.exp(cr[CHUNK:])
    qd = q * jnp.exp(cc)
    ki = k * jnp.exp(-cc)
    iv = iv.astype(BF16)
    gate = gate * _sigmoid(gate)
    head_cols = [slice(h * HG_DK, (h + 1) * HG_DK) for h in range(HG_HEADS)]

    def dk(sl, lo_row, hi_row):
        n = _HG_RANGES.index((lo_row, hi_row))
        return ev[n:n + 1, sl]

    scores, inter, upds = [], [], []
    for h, sl in enumerate(head_cols):
        qd_h, ki_h = qd[:, sl], ki[:, sl]
        qd_b = qd_h.astype(BF16)
        ki_b = [sub(ki_h, j).astype(BF16) for j in range(nsub)]
        kend = [sub(ki_h, 0) * dk(sl, 0, s1), sub(ki_h, 1) * dk(sl, s1, s2),
                sub(ki_h, 2) * dk(sl, s2, s3), sub(ki_h, 3) * dk(sl, s3, s4)]
        kend_b = [kend[j].astype(BF16) for j in range(nsub - 1)]
        kmat = [
            [ki_b[0], zeros_sub, zeros_sub, zeros_sub],
            [kend_b[0], ki_b[1], zeros_sub, zeros_sub],
            [(kend[0] * dk(sl, s1, s2)).astype(BF16), kend_b[1], ki_b[2], zeros_sub],
            [(kend[0] * dk(sl, s1, s3)).astype(BF16), (kend[1] * dk(sl, s2, s3)).astype(BF16),
             kend_b[2], ki_b[3]],
        ]
        scores.append(lax.dot_general(qd_b, jnp.concatenate([p for row in kmat for p in row], axis=0),
                                      (((1,), (1,)), ((), ())), preferred_element_type=F32))
        qe = jnp.concatenate([sub(qd_h, 0), sub(qd_h, 1) * dk(sl, 0, s1), sub(qd_h, 2) * dk(sl, 0, s2),
                              sub(qd_h, 3) * dk(sl, 0, s3)], axis=0).astype(BF16)
        ke = jnp.concatenate([kend[0] * dk(sl, s1, s4), kend[1] * dk(sl, s2, s4),
                              kend[2] * dk(sl, s3, s4), kend[3]], axis=0).astype(BF16)
        inter.append(lax.dot_general(qe, st_ref[h].astype(BF16), (((1,), (1,)), ((), ())),
                                     preferred_element_type=F32))
        upds.append(lax.dot_general(iv[:, sl], ke, (((0,), (0,)), ((), ())),
                                    preferred_element_type=F32))
    return scores, inter, upds, iv, gate, ev[_HG_RANGES.index((0, s4)):_HG_RANGES.index((0, s4)) + 1]


def _hgrn_consume(scores, inter, upds, iv, gate, chunk_decay, gain, st_ref, live):
    nsub = CHUNK // HG_SUB
    t_i = lax.broadcasted_iota(jnp.int32, (CHUNK, nsub * CHUNK), 0)
    c_i = lax.broadcasted_iota(jnp.int32, (CHUNK, nsub * CHUNK), 1)
    keep = ((t_i >> HG_SUB_SHIFT) == (c_i >> CHUNK_SHIFT)) & ((c_i & (CHUNK - 1)) <= t_i)
    head_cols = [slice(h * HG_DK, (h + 1) * HG_DK) for h in range(HG_HEADS)]
    outs = []
    for h, sl in enumerate(head_cols):
        a = jnp.where(keep, scores[h], 0.0).astype(BF16)
        outs.append(inter[h] + jnp.dot(a, jnp.concatenate([iv[:, sl]] * nsub, axis=0),
                                       preferred_element_type=F32))
        new_state = st_ref[h] * chunk_decay[:, sl] + upds[h]
        st_ref[h] = new_state if live is None else jnp.where(live, new_state, st_ref[h])
    return [(_rms(outs[h], gain) * gate[:, sl]).astype(BF16) for h, sl in enumerate(head_cols)]


def _hgrn_kernel(sel_ref, lbl_ref, gain_ref, s0_ref, q_ref, f_ref, i_ref, g_ref, rec_ref, sout_ref,
                 st_ref, *, layer):
    for h in range(HG_HEADS):
        st_ref[h] = s0_ref[0, h].T

    def load():
        return q_ref[...], f_ref[...], i_ref[...], g_ref[...]

    def store(h, out):
        rec_ref[:, h * HG_DV:(h + 1) * HG_DV] = out

    for stage in _hgrn_stages(load, sel_ref[...], _lower_bound(lbl_ref, layer), gain_ref[...], st_ref, store):
        stage()
    for h in range(HG_HEADS):
        sout_ref[0, h] = st_ref[h].T


def _hgrn_single_chunk(lb_logits, gain, s0, z, *, n_seq, layer):
    n_layers = lb_logits.shape[0]
    sel = _hgrn_selector()

    def zspec(col):
        return pl.BlockSpec((CHUNK, D_HG), lambda b: (b, col))

    state_spec = pl.BlockSpec((1, HG_HEADS, HG_DK, HG_DV), lambda b: (b, 0, 0, 0))
    return pl.pallas_call(
        functools.partial(_hgrn_kernel, layer=layer),
        grid=(n_seq,),
        in_specs=[
            pl.BlockSpec(sel.shape, lambda b: (0, 0)),
            pl.BlockSpec((n_layers, D_HG), lambda b: (0, 0)),
            pl.BlockSpec((1, HG_DV), lambda b: (0, 0)),
            state_spec,
            zspec(Z_QH), zspec(Z_FH), zspec(Z_IH), zspec(Z_GH),
        ],
        out_specs=[pl.BlockSpec((CHUNK, D_HG), lambda b: (b, 0)), state_spec],
        out_shape=[
            jax.ShapeDtypeStruct((n_seq * CHUNK, D_HG), BF16),
            jax.ShapeDtypeStruct((n_seq, HG_HEADS, HG_DK, HG_DV), F32),
        ],
        scratch_shapes=[pltpu.VMEM((HG_HEADS, HG_DV, HG_DK), F32)],
        compiler_params=_cparams(("arbitrary",)),
        name="hgrn2_mixer",
    )(sel, lb_logits, gain, s0, z, z, z, z)


Z_ATTN_W = D_ATTN + 2 * N_KV * HEAD_DIM


def _mixer_front_kernel(bucket_ref, table_ref, sinks_ref, pre_ref, x_ref, w_ref,
                        attn_ref, zhg_ref, kv_ref,
                        bias_ref, z_even, z_odd, xn_ref, k_all, v_all, *, units_per_seq):
    s = pl.program_id(0)
    unit = MIX_UNIT
    kv_w = N_KV * HEAD_DIM
    u_in_seq = (s + (units_per_seq - 1)) % units_per_seq

    @pl.when(s == 0)
    def _():
        _fill_bias(bucket_ref, table_ref, bias_ref)
        z_odd[...] = jnp.zeros(z_odd.shape, F32)
        k_all[0:WINDOW, :] = jnp.zeros((WINDOW, kv_w), BF16)
        v_all[0:WINDOW, :] = jnp.zeros((WINDOW, kv_w), BF16)

    def step(z_write, z_read):
        xn_ref[...] = _rms(x_ref[...], pre_ref[...]).astype(BF16)

        def project(c):
            lo = c * MIX_PROJ_TN
            res = jnp.dot(xn_ref[...], w_ref[:, lo:lo + MIX_PROJ_TN], preferred_element_type=F32)
            if lo < D_ATTN:
                z_write[:, lo:lo + MIX_PROJ_TN] = res
            elif lo < D_ATTN + 4 * D_HG:
                zhg_ref[:, lo - D_ATTN:lo - D_ATTN + MIX_PROJ_TN] = res
            else:
                z_write[:, lo - 4 * D_HG:lo - 4 * D_HG + MIX_PROJ_TN] = res

        k_cols = slice(D_ATTN, D_ATTN + kv_w)
        v_cols = slice(D_ATTN + kv_w, D_ATTN + 2 * kv_w)

        def load_kv():
            k_all[WINDOW:, :] = z_read[:, k_cols].astype(BF16)
            v_all[WINDOW:, :] = z_read[:, v_cols].astype(BF16)
            kv_ref[...] = z_read[:, D_ATTN:]

        def keep_history():
            k_all[0:WINDOW, :] = k_all[unit:unit + WINDOW, :]
            v_all[0:WINDOW, :] = v_all[unit:unit + WINDOW, :]

        attend = _attend_stages(z_read, attn_ref, k_all, v_all, bias_ref, sinks_ref, u_in_seq * unit)
        _interleave([functools.partial(project, c) for c in range(D_IN // MIX_PROJ_TN)],
                    [load_kv] + attend + [keep_history])

    @pl.when(s % 2 == 0)
    def _():
        step(z_even, z_odd)

    @pl.when(s % 2 == 1)
    def _():
        step(z_odd, z_even)


def _mixer_front(x, pre, w_in, bucket, table, sinks, *, n_seq, seq_len):
    unit = MIX_UNIT
    ups = seq_len // unit
    n_units = n_seq * ups
    kv_w = N_KV * HEAD_DIM
    smem = pl.BlockSpec(memory_space=pltpu.SMEM)

    def const(shape):
        return pl.BlockSpec(shape, lambda s: (0,) * len(shape))

    def this_unit(s):
        return (jnp.minimum(s, n_units - 1), 0)

    def prev_unit(s):
        return (jnp.maximum(s - 1, 0), 0)

    return pl.pallas_call(
        functools.partial(_mixer_front_kernel, units_per_seq=ups),
        grid=(n_units + 1,),
        in_specs=[
            const((CHUNK, LK)), smem, smem, const((1, D_MODEL)),
            pl.BlockSpec((unit, D_MODEL), this_unit),
            pl.BlockSpec((D_MODEL, D_IN), lambda s: (0, 0), pipeline_mode=pl.Buffered(1)),
        ],
        out_specs=[
            pl.BlockSpec((unit, D_ATTN), prev_unit),
            pl.BlockSpec((unit, 4 * D_HG), this_unit),
            pl.BlockSpec((unit, 2 * kv_w), prev_unit),
        ],
        out_shape=[
            jax.ShapeDtypeStruct((n_units * unit, D_ATTN), BF16),
            jax.ShapeDtypeStruct((n_units * unit, 4 * D_HG), F32),
            jax.ShapeDtypeStruct((n_units * unit, 2 * kv_w), F32),
        ],
        scratch_shapes=[
            pltpu.VMEM((N_KV, GROUP * CHUNK, LK), F32),
            pltpu.VMEM((unit, Z_ATTN_W), F32),
            pltpu.VMEM((unit, Z_ATTN_W), F32),
            pltpu.VMEM((unit, D_MODEL), BF16),
            pltpu.VMEM((WINDOW + unit, kv_w), BF16),
            pltpu.VMEM((WINDOW + unit, kv_w), BF16),
        ],
        compiler_params=_cparams(("arbitrary",)),
        name="mixer_front",
    )(bucket, table, sinks, pre, x, w_in)


def _mixer_back_kernel(sel_ref, lbl_ref, gain_ref, post_ref, zhg_ref, x_ref, attn_ref, w_ref,
                       o_ref, sout_ref,
                       rec_even, rec_odd, mix_ref, st_ref, *, layer, units_per_seq):
    s = pl.program_id(0)
    n_units = pl.num_programs(0) - 1
    unit = MIX_UNIT
    u_in_seq = s % units_per_seq
    live = s < n_units

    @pl.when(s == 0)
    def _():
        rec_odd[...] = jnp.zeros(rec_odd.shape, BF16)

    @pl.when(u_in_seq == 0)
    def _():
        st_ref[...] = jnp.zeros(st_ref.shape, F32)

    def step(rec_write, rec_read):
        lb = _lower_bound(lbl_ref, layer)

        def recur_stages(c):
            rows = slice(c * CHUNK, (c + 1) * CHUNK)

            def load():
                return tuple(zhg_ref[rows, col * D_HG:(col + 1) * D_HG] for col in range(4))

            def store(h, out):
                rec_write[rows, h * HG_DV:(h + 1) * HG_DV] = out

            return _hgrn_stages(load, sel_ref[...], lb, gain_ref[...], st_ref, store, live)

        def project(c):
            cols = slice(c * MIX_PROJ_TN, (c + 1) * MIX_PROJ_TN)
            mix_ref[:, cols] = (
                jnp.dot(attn_ref[...], w_ref[0:D_ATTN, cols], preferred_element_type=F32)
                + jnp.dot(rec_read[...], w_ref[D_ATTN:, cols], preferred_element_type=F32))

        recur = [t for c in range(unit // CHUNK) for t in recur_stages(c)]
        _interleave([functools.partial(project, c) for c in range(D_MODEL // MIX_PROJ_TN)], recur)
        o_ref[...] = x_ref[...] + _rms(mix_ref[...], post_ref[...])

    @pl.when(s % 2 == 0)
    def _():
        step(rec_even, rec_odd)

    @pl.when(s % 2 == 1)
    def _():
        step(rec_odd, rec_even)

    @pl.when(live & (u_in_seq == units_per_seq - 1))
    def _():
        for h in range(HG_HEADS):
            sout_ref[0, h] = st_ref[h].T


def _mixer_back(x, attn, zhg, w_out, post, lb_logits, gain, *, n_seq, seq_len, layer):
    unit = MIX_UNIT
    ups = seq_len // unit
    n_units = n_seq * ups
    n_layers = lb_logits.shape[0]
    sel = _hgrn_selector()

    def const(shape):
        return pl.BlockSpec(shape, lambda s: (0,) * len(shape))

    def this_unit(s):
        return (jnp.minimum(s, n_units - 1), 0)

    def prev_unit(s):
        return (jnp.maximum(s - 1, 0), 0)

    return pl.pallas_call(
        functools.partial(_mixer_back_kernel, layer=layer, units_per_seq=ups),
        grid=(n_units + 1,),
        in_specs=[
            const(sel.shape), const((n_layers, D_HG)), const((1, HG_DV)), const((1, D_MODEL)),
            pl.BlockSpec((unit, 4 * D_HG), this_unit),
            pl.BlockSpec((unit, D_MODEL), prev_unit),
            pl.BlockSpec((unit, D_ATTN), prev_unit),
            pl.BlockSpec((D_MODEL, D_MODEL), lambda s: (0, 0), pipeline_mode=pl.Buffered(1)),
        ],
        out_specs=[
            pl.BlockSpec((unit, D_MODEL), prev_unit),
            pl.BlockSpec((1, HG_HEADS, HG_DK, HG_DV),
                         lambda s: (jnp.minimum(s, n_units - 1) // ups, 0, 0, 0)),
        ],
        out_shape=[
            jax.ShapeDtypeStruct((n_units * unit, D_MODEL), F32),
            jax.ShapeDtypeStruct((n_seq, HG_HEADS, HG_DK, HG_DV), F32),
        ],
        scratch_shapes=[
            pltpu.VMEM((unit, D_HG), BF16),
            pltpu.VMEM((unit, D_HG), BF16),
            pltpu.VMEM((unit, D_MODEL), F32),
            pltpu.VMEM((HG_HEADS, HG_DV, HG_DK), F32),
        ],
        compiler_params=_cparams(("arbitrary",)),
        name="mixer_back",
    )(sel, lb_logits, gain, post, zhg, x, attn, w_out)


def _outproj_kernel(x_ref, a_ref, r_ref, w_ref, post_ref, o_ref):
    nsub = x_ref.shape[0] // SUB_ROWS
    sub = [slice(r * SUB_ROWS, (r + 1) * SUB_ROWS) for r in range(nsub)]
    mix = [None] * nsub

    def project(r):
        mix[r] = (jnp.dot(a_ref[sub[r], :], w_ref[0:D_ATTN, :], preferred_element_type=F32)
                  + jnp.dot(r_ref[sub[r], :], w_ref[D_ATTN:, :], preferred_element_type=F32))

    def finish(r):
        o_ref[sub[r], :] = x_ref[sub[r], :] + _rms(mix[r], post_ref[...])

    _pipelined(nsub, None, project, finish)


def _outproj(x, attn, rec, w_out, post):
    t = x.shape[0]
    tm = min(ROW_TM, t)
    return pl.pallas_call(
        _outproj_kernel,
        grid=(t // tm,),
        in_specs=[
            pl.BlockSpec((tm, D_MODEL), lambda i: (i, 0)),
            pl.BlockSpec((tm, D_ATTN), lambda i: (i, 0)),
            pl.BlockSpec((tm, D_HG), lambda i: (i, 0)),
            pl.BlockSpec((D_MODEL, D_MODEL), lambda i: (0, 0), pipeline_mode=pl.Buffered(1)),
            pl.BlockSpec((1, D_MODEL), lambda i: (0, 0)),
        ],
        out_specs=pl.BlockSpec((tm, D_MODEL), lambda i: (i, 0)),
        out_shape=jax.ShapeDtypeStruct((t, D_MODEL), F32),
        compiler_params=_cparams(("parallel",)),
        name="mixer_out_proj",
    )(x, attn, rec, w_out, post)


def _ple_kernel(x_ref, p_ref, pre_ref, post_ref, wg_ref, wp_ref, o_ref):
    nsub = x_ref.shape[0] // SUB_ROWS
    sub = [slice(r * SUB_ROWS, (r + 1) * SUB_ROWS) for r in range(nsub)]
    hs, gates, projs = [None] * nsub, [None] * nsub, [None] * nsub

    def norm(r):
        hs[r] = _rms(x_ref[sub[r], :], pre_ref[...]).astype(BF16)

    def project(r):
        gates[r] = jnp.dot(hs[r], wg_ref[...], preferred_element_type=F32)
        projs[r] = jnp.dot(p_ref[sub[r], :].astype(BF16), wp_ref[...], preferred_element_type=F32)

    def finish(r):
        o_ref[sub[r], :] = x_ref[sub[r], :] + _rms(_sigmoid(gates[r]) * projs[r], post_ref[...])

    _pipelined(nsub, norm, project, finish)


def _ple(x, p, pre, post, w_gate, w_proj):
    t = x.shape[0]
    tm = min(ROW_TM, t)
    return pl.pallas_call(
        _ple_kernel,
        grid=(t // tm,),
        in_specs=[
            pl.BlockSpec((tm, D_MODEL), lambda i: (i, 0)),
            pl.BlockSpec((tm, PLE_DIM), lambda i: (i, 0)),
            pl.BlockSpec((1, D_MODEL), lambda i: (0, 0)),
            pl.BlockSpec((1, D_MODEL), lambda i: (0, 0)),
            pl.BlockSpec((D_MODEL, D_MODEL), lambda i: (0, 0), pipeline_mode=pl.Buffered(1)),
            pl.BlockSpec((PLE_DIM, D_MODEL), lambda i: (0, 0)),
        ],
        out_specs=pl.BlockSpec((tm, D_MODEL), lambda i: (i, 0)),
        out_shape=jax.ShapeDtypeStruct((t, D_MODEL), F32),
        compiler_params=_cparams(("parallel",)),
        name="ple_embed",
    )(x, p, pre, post, w_gate, w_proj)


def _t5_bucket(rel):
    half = NUM_BUCKETS // 2
    max_exact = half // 2
    n = jnp.abs(rel)
    nf = jnp.maximum(n, 1).astype(jnp.float32)
    large = max_exact + (jnp.log(nf / max_exact) / math.log(MAX_DISTANCE / max_exact)
                         * (half - max_exact)).astype(jnp.int32)
    large = jnp.minimum(large, half - 1)
    return jnp.where(rel > 0, half, 0) + jnp.where(n < max_exact, n, large)


def _ffn1(x, w):
    row = lambda v: v.reshape(1, -1)
    return _ffn(x, row(w['ffn1_pre']), row(w['ffn1_post']), w['ffn1_w_gate'], w['ffn1_w_up'], w['ffn1_w_down'])


def _ffn2_ple(x, p, w):
    row = lambda v: v.reshape(1, -1)
    x = _ffn(x, row(w['ffn2_pre']), row(w['ffn2_post']), w['ffn2_w_gate'], w['ffn2_w_up'], w['ffn2_w_down'])
    return _ple(x, p, row(w['ple_pre']), row(w['ple_post']), w['w_ple_gate'], w['w_ple_proj'])


def _prompt_layer(x, p, w, table, bucket, layer, *, n_seq, seq_len):
    row = lambda v: v.reshape(1, -1)
    x = _ffn1(x, w)
    attn, zhg, kv = _mixer_front(x, row(w['mix_pre']), w['w_in'], bucket, table, w['attn_sinks'],
                                 n_seq=n_seq, seq_len=seq_len)
    x, s_new = _mixer_back(x, attn, zhg, w['w_out'], row(w['mix_post']), w['hgrn_lb_logits'],
                           row(w['hgrn_norm']), n_seq=n_seq, seq_len=seq_len, layer=layer)
    return _ffn2_ple(x, p, w), kv, s_new


def _sample_layer(x, p, w, table, bucket, layer, *, n_seq, s0, k_cache, v_cache):
    row = lambda v: v.reshape(1, -1)
    kv_w = N_KV * HEAD_DIM
    x = _ffn1(x, w)
    z = _inproj(x, row(w['mix_pre']), w['w_in'])
    attn = _attention_cached(bucket, table, w['attn_sinks'], z, k_cache.reshape(n_seq * WINDOW, kv_w),
                             v_cache.reshape(n_seq * WINDOW, kv_w), n_seq=n_seq)
    rec, s_new = _hgrn_single_chunk(w['hgrn_lb_logits'], row(w['hgrn_norm']), s0, z, n_seq=n_seq, layer=layer)
    x = _outproj(x, attn, rec, w['w_out'], row(w['mix_post']))
    return _ffn2_ple(x, p, w), z[:, Z_K * kv_w:], s_new


def kernel(x_prompt, x_sample, cache_attn_k, cache_attn_v, state_hgrn, p_prompt, p_sample,
           rel_bias_table, ffn1_pre, ffn1_post, ffn1_w_gate, ffn1_w_up, ffn1_w_down,
           mix_pre, mix_post, w_in, w_out, attn_sinks, hgrn_lb_logits, hgrn_norm,
           ffn2_pre, ffn2_post, ffn2_w_gate, ffn2_w_up, ffn2_w_down,
           ple_pre, ple_post, w_ple_gate, w_ple_proj):
    depth = w_in.shape[0]
    bp, sp, _ = x_prompt.shape
    bs, ss, _ = x_sample.shape
    wc = cache_attn_k.shape[2]
    assert wc == WINDOW and ss == CHUNK and sp % MIX_UNIT == 0 and sp >= WINDOW

    rel = jnp.arange(LK)[None, :] - WINDOW - jnp.arange(CHUNK)[:, None]
    bucket = _t5_bucket(rel).astype(jnp.int32)

    kv_w = N_KV * HEAD_DIM
    k_lo, v_lo = D_ATTN, D_ATTN + kv_w
    yp = x_prompt.reshape(bp * sp, D_MODEL)
    ys = x_sample.reshape(bs * ss, D_MODEL)
    outs = [[] for _ in range(6)]
    for l in range(depth):
        w_in_l = w_in[l]
        w = {
            'ffn1_pre': ffn1_pre[l], 'ffn1_post': ffn1_post[l],
            'ffn1_w_gate': ffn1_w_gate[l].astype(BF16), 'ffn1_w_up': ffn1_w_up[l].astype(BF16),
            'ffn1_w_down': ffn1_w_down[l].astype(BF16),
            'mix_pre': mix_pre[l], 'mix_post': mix_post[l],
            'w_in': jnp.concatenate([w_in_l[:, :k_lo], w_in_l[:, v_lo + kv_w:], w_in_l[:, k_lo:v_lo + kv_w]],
                                    axis=1).astype(BF16),
            'w_out': w_out[l].astype(BF16), 'attn_sinks': attn_sinks[l],
            'hgrn_lb_logits': hgrn_lb_logits, 'hgrn_norm': hgrn_norm[l],
            'ffn2_pre': ffn2_pre[l], 'ffn2_post': ffn2_post[l],
            'ffn2_w_gate': ffn2_w_gate[l].astype(BF16), 'ffn2_w_up': ffn2_w_up[l].astype(BF16),
            'ffn2_w_down': ffn2_w_down[l].astype(BF16),
            'ple_pre': ple_pre[l], 'ple_post': ple_post[l],
            'w_ple_gate': w_ple_gate[l].astype(BF16), 'w_ple_proj': w_ple_proj[l].astype(BF16),
        }
        yp, kv_p, st_p = _prompt_layer(yp, p_prompt[l].reshape(bp * sp, PLE_DIM), w, rel_bias_table, bucket, l,
                                       n_seq=bp, seq_len=sp)
        ys, kv_s, st_s = _sample_layer(ys, p_sample[l].reshape(bs * ss, PLE_DIM), w, rel_bias_table, bucket, l,
                                       n_seq=bs, s0=state_hgrn[l], k_cache=cache_attn_k[l],
                                       v_cache=cache_attn_v[l])
        kv_p = kv_p.reshape(bp, sp, 2 * kv_w)[:, sp - WINDOW:].reshape(bp, WINDOW, 2, N_KV, HEAD_DIM)
        outs[0].append(kv_p[:, :, 0])
        outs[1].append(kv_p[:, :, 1])
        outs[2].append(st_p)
        kv_s = kv_s.reshape(bs, ss, 2, N_KV, HEAD_DIM)
        outs[3].append(jnp.concatenate([cache_attn_k[l], kv_s[:, :, 0]], axis=1)[:, ss:])
        outs[4].append(jnp.concatenate([cache_attn_v[l], kv_s[:, :, 1]], axis=1)[:, ss:])
        outs[5].append(st_s)
    return (yp.reshape(bp, sp, D_MODEL), ys.reshape(bs, ss, D_MODEL),
            jnp.stack(outs[0]), jnp.stack(outs[1]), jnp.stack(outs[2]),
            jnp.stack(outs[3]), jnp.stack(outs[4]), jnp.stack(outs[5]))
```

```python
import functools
import math

import jax
import jax.numpy as jnp
import numpy as np
from jax import lax
from jax.experimental import pallas as pl
from jax.experimental.pallas import tpu as pltpu

F32 = jnp.float32
BF16 = jnp.bfloat16

D_MODEL = 2048
CHUNK = 64
D_ATTN = 1024
HEAD_DIM = 128
N_HEADS = 8
N_KV = 2
GROUP = N_HEADS // N_KV
WINDOW = 128
LK = WINDOW + CHUNK
NUM_BUCKETS = 32
MAX_DISTANCE = 128
HG_DK = 128
HG_DV = 128
HG_HEADS = 8
D_HG = HG_HEADS * HG_DV
HG_SUB = 16
HG_GROUP = 8
HG_SUB_SHIFT = HG_SUB.bit_length() - 1
CHUNK_SHIFT = CHUNK.bit_length() - 1
D_FF = 5632
PLE_DIM = 256
EPS = 1e-6
NEG_INF = -1e30
D_IN = D_ATTN + 2 * N_KV * HEAD_DIM + 4 * D_HG

Z_QH, Z_FH, Z_IH, Z_GH = 1, 2, 3, 4
Z_K, Z_V = 20, 21

VMEM_LIMIT_BYTES = 60 * 1024 * 1024

FFN_TM = 1024
FFN_TF = 512
PROJ_TM = 1024
PROJ_TN = 512
ROW_TM = 1024
SUB_ROWS = 256
NORM_ROWS = 256
MIX_UNIT = 256
MIX_PROJ_TN = 512


def _rms(x, gain):
    y = x * lax.rsqrt(jnp.mean(x * x, axis=-1, keepdims=True) + EPS)
    return y * gain


def _sigmoid(x):
    return 1.0 / (1.0 + jnp.exp(-x))


def _cparams(sem):
    return pltpu.CompilerParams(dimension_semantics=sem, vmem_limit_bytes=VMEM_LIMIT_BYTES)


def _pipelined(n, before, matmul, after):
    if before is not None:
        before(0)
    for r in range(n):
        if before is not None and r + 1 < n:
            before(r + 1)
        matmul(r)
        if after is not None and r > 0:
            after(r - 1)
    if after is not None:
        after(n - 1)


def _ffn_kernel(x_ref, pre_ref, post_ref, wg_ref, wu_ref, wd_ref, o_ref, xn_ref):
    j = pl.program_id(1)
    last = pl.num_programs(1) - 1
    nsub = x_ref.shape[0] // SUB_ROWS
    sub = [slice(r * SUB_ROWS, (r + 1) * SUB_ROWS) for r in range(nsub)]

    def swiglu(rows):
        xn = xn_ref[rows, :]
        g = jnp.dot(xn, wg_ref[...], preferred_element_type=F32)
        u = jnp.dot(xn, wu_ref[...], preferred_element_type=F32)
        h = (g * _sigmoid(g) * u).astype(BF16)
        return jnp.dot(h, wd_ref[...], preferred_element_type=F32)

    @pl.when(j == 0)
    def _():
        def norm(r):
            xn_ref[sub[r], :] = _rms(x_ref[sub[r], :], pre_ref[...]).astype(BF16)

        def first(r):
            o_ref[sub[r], :] = swiglu(sub[r])

        _pipelined(nsub, norm, first, None)

    @pl.when((j > 0) & (j < last))
    def _():
        o_ref[...] += swiglu(slice(None))

    @pl.when(j == last)
    def _():
        ys = [None] * nsub

        def final(r):
            ys[r] = o_ref[sub[r], :] + swiglu(sub[r])

        def finish(r):
            o_ref[sub[r], :] = x_ref[sub[r], :] + 0.5 * _rms(ys[r], post_ref[...])

        _pipelined(nsub, None, final, finish)


def _ffn(x, pre, post, wg, wu, wd):
    t = x.shape[0]
    tm = min(FFN_TM, t)
    grid = (t // tm, D_FF // FFN_TF)
    assert grid[1] >= 3
    return pl.pallas_call(
        _ffn_kernel,
        grid=grid,
        in_specs=[
            pl.BlockSpec((tm, D_MODEL), lambda i, j: (i, 0)),
            pl.BlockSpec((1, D_MODEL), lambda i, j: (0, 0)),
            pl.BlockSpec((1, D_MODEL), lambda i, j: (0, 0)),
            pl.BlockSpec((D_MODEL, FFN_TF), lambda i, j: (0, j)),
            pl.BlockSpec((D_MODEL, FFN_TF), lambda i, j: (0, j)),
            pl.BlockSpec((FFN_TF, D_MODEL), lambda i, j: (j, 0)),
        ],
        out_specs=pl.BlockSpec((tm, D_MODEL), lambda i, j: (i, 0)),
        out_shape=jax.ShapeDtypeStruct((t, D_MODEL), F32),
        scratch_shapes=[pltpu.VMEM((tm, D_MODEL), BF16)],
        compiler_params=_cparams(("parallel", "arbitrary")),
        name="ffn_half_step",
    )(x, pre, post, wg, wu, wd)


def _inproj_kernel(x_ref, pre_ref, w_ref, z_ref, xn_ref):
    j = pl.program_id(1)
    tm = x_ref.shape[0]

    @pl.when(j == 0)
    def _():
        def body(r, c):
            rows = pl.ds(pl.multiple_of(r * NORM_ROWS, NORM_ROWS), NORM_ROWS)
            xn_ref[rows, :] = _rms(x_ref[rows, :], pre_ref[...]).astype(BF16)
            return c
        lax.fori_loop(0, tm // NORM_ROWS, body, 0)

    z_ref[...] = jnp.dot(xn_ref[...], w_ref[...], preferred_element_type=F32)


def _inproj(x, pre, w_in):
    t = x.shape[0]
    tm = min(PROJ_TM, t)
    grid = (t // tm, D_IN // PROJ_TN)
    return pl.pallas_call(
        _inproj_kernel,
        grid=grid,
        in_specs=[
            pl.BlockSpec((tm, D_MODEL), lambda i, j: (i, 0)),
            pl.BlockSpec((1, D_MODEL), lambda i, j: (0, 0)),
            pl.BlockSpec((D_MODEL, PROJ_TN), lambda i, j: (0, j)),
        ],
        out_specs=pl.BlockSpec((tm, PROJ_TN), lambda i, j: (i, j)),
        out_shape=jax.ShapeDtypeStruct((t, D_IN), F32),
        scratch_shapes=[pltpu.VMEM((tm, D_MODEL), BF16)],
        compiler_params=_cparams(("parallel", "arbitrary")),
        name="mixer_in_proj",
    )(x, pre, w_in)


def _fill_bias(bucket_ref, table_ref, bias_ref):
    bucket = bucket_ref[...]
    for h in range(N_HEADS):
        acc = jnp.zeros((CHUNK, LK), F32)
        for n in range(NUM_BUCKETS):
            acc = jnp.where(bucket == n, table_ref[n, h], acc)
        g, hh = divmod(h, GROUP)
        bias_ref[g, hh * CHUNK:(hh + 1) * CHUNK, :] = acc


def _sink_column(sinks_ref, g):
    row = lax.broadcasted_iota(jnp.int32, (GROUP * CHUNK, 1), 0)
    sink = jnp.zeros((GROUP * CHUNK, 1), F32)
    for hh in range(GROUP):
        sink = jnp.where((row >> CHUNK_SHIFT) == hh, sinks_ref[g * GROUP + hh], sink)
    return sink


def _attend_chunk(q_of_head, kk, vv, bias_g, sink, first_key_pos):
    qs = jnp.concatenate([q_of_head(hh) for hh in range(GROUP)], axis=0).astype(BF16)
    s = lax.dot_general(qs, kk, (((1,), (1,)), ((), ())), preferred_element_type=F32)
    s = s * (HEAD_DIM ** -0.5) + bias_g
    if first_key_pos is not None:
        col = lax.broadcasted_iota(jnp.int32, (GROUP * CHUNK, LK), 1)
        s = jnp.where(col + first_key_pos >= 0, s, NEG_INF)
    m = jnp.maximum(jnp.max(s, axis=-1, keepdims=True), sink)
    e = jnp.exp(s - m)
    p = e / (jnp.sum(e, axis=-1, keepdims=True) + jnp.exp(sink - m))
    return jnp.dot(p.astype(BF16), vv, preferred_element_type=F32)


def _attend_stages(q_ref, o_ref, k_all, v_all, bias_ref, sinks_ref, pos0):
    def stage(c, g):
        rows = slice(c * CHUNK, (c + 1) * CHUNK)
        cols = slice(g * HEAD_DIM, (g + 1) * HEAD_DIM)
        masked = pos0 is not None and c * CHUNK < WINDOW

        def q_of_head(hh):
            h = g * GROUP + hh
            return q_ref[rows, h * HEAD_DIM:(h + 1) * HEAD_DIM]

        o = _attend_chunk(q_of_head, k_all[c * CHUNK:c * CHUNK + LK, cols],
                          v_all[c * CHUNK:c * CHUNK + LK, cols], bias_ref[g], _sink_column(sinks_ref, g),
                          pos0 + (c * CHUNK - WINDOW) if masked else None)
        for hh in range(GROUP):
            h = g * GROUP + hh
            o_ref[rows, h * HEAD_DIM:(h + 1) * HEAD_DIM] = o[hh * CHUNK:(hh + 1) * CHUNK, :].astype(BF16)

    return [functools.partial(stage, c, g) for c in range(q_ref.shape[0] // CHUNK) for g in range(N_KV)]


def _interleave(a, b):
    ia = ib = 0
    while ia < len(a) or ib < len(b):
        if ib >= len(b) or (ia < len(a) and ia * len(b) <= ib * len(a)):
            a[ia]()
            ia += 1
        else:
            b[ib]()
            ib += 1


def _attn_kernel(bucket_ref, table_ref, sinks_ref, q_ref, kc_ref, vc_ref, kp_ref, vp_ref,
                 o_ref, bias_ref, k_all, v_all):
    @pl.when(pl.program_id(0) == 0)
    def _():
        _fill_bias(bucket_ref, table_ref, bias_ref)

    k_all[0:WINDOW, :] = kp_ref[...].astype(BF16)
    v_all[0:WINDOW, :] = vp_ref[...].astype(BF16)
    k_all[WINDOW:, :] = kc_ref[...].astype(BF16)
    v_all[WINDOW:, :] = vc_ref[...].astype(BF16)
    for stage in _attend_stages(q_ref, o_ref, k_all, v_all, bias_ref, sinks_ref, None):
        stage()


def _attention_cached(bucket, table, sinks, z, k_prev, v_prev, *, n_seq):
    kv_w = N_KV * HEAD_DIM
    smem = pl.BlockSpec(memory_space=pltpu.SMEM)
    return pl.pallas_call(
        _attn_kernel,
        grid=(n_seq,),
        in_specs=[
            pl.BlockSpec((CHUNK, LK), lambda b: (0, 0)),
            smem,
            smem,
            pl.BlockSpec((CHUNK, D_ATTN), lambda b: (b, 0)),
            pl.BlockSpec((CHUNK, kv_w), lambda b: (b, Z_K)),
            pl.BlockSpec((CHUNK, kv_w), lambda b: (b, Z_V)),
            pl.BlockSpec((WINDOW, kv_w), lambda b: (b, 0)),
            pl.BlockSpec((WINDOW, kv_w), lambda b: (b, 0)),
        ],
        out_specs=pl.BlockSpec((CHUNK, D_ATTN), lambda b: (b, 0)),
        out_shape=jax.ShapeDtypeStruct((n_seq * CHUNK, D_ATTN), BF16),
        scratch_shapes=[
            pltpu.VMEM((N_KV, GROUP * CHUNK, LK), F32),
            pltpu.VMEM((WINDOW + CHUNK, kv_w), BF16),
            pltpu.VMEM((WINDOW + CHUNK, kv_w), BF16),
        ],
        compiler_params=_cparams(("arbitrary",)),
        name="swa_attention",
    )(bucket, table, sinks, z, z, z, k_prev, v_prev)


_HG_RANGES = [(0, 16), (0, 32), (0, 48), (0, 64), (16, 32), (32, 48), (48, 64), (16, 48), (16, 64), (32, 64)]
_HG_RANGE_ROWS = 16


def _hgrn_selector():
    t = np.arange(CHUNK)
    within = (t[:, None] // HG_SUB == t[None, :] // HG_SUB) & (t[None, :] <= t[:, None])
    ranges = np.zeros((_HG_RANGE_ROWS, CHUNK), bool)
    for n, (lo, hi) in enumerate(_HG_RANGES):
        ranges[n, lo:hi] = True
    once = np.concatenate([within, ranges], axis=0)
    return jnp.asarray(np.concatenate([once, once, once], axis=1), BF16)


def _lower_bound(lbl_ref, layer):
    logits = lbl_ref[...]
    ex = jnp.exp(logits - jnp.max(logits, axis=0, keepdims=True))
    return jnp.sum(ex[:layer + 1], axis=0, keepdims=True) / jnp.sum(ex, axis=0, keepdims=True)


def _hgrn_stages(load, sel, lb, gain, st_ref, store):
    ctx = {}

    def front():
        ctx['front'] = _hgrn_front(*load(), sel, lb)

    def issue(heads):
        ctx[heads] = _hgrn_issue(ctx['front'], heads)

    def consume(heads):
        for h, out in zip(heads, _hgrn_consume(ctx['front'], ctx[heads], heads, gain, st_ref)):
            store(h, out)

    stages = [front]
    for g in range(HG_HEADS // HG_GROUP):
        heads = tuple(range(g * HG_GROUP, (g + 1) * HG_GROUP))
        stages += [functools.partial(issue, heads), functools.partial(consume, heads)]
    return stages


def _hgrn_front(q, fl, iv, gate, sel, lb):
    f = lb + (1.0 - lb) * _sigmoid(fl)
    lf = jnp.log(f)
    k = 1.0 - f
    hi = lf.astype(BF16)
    r1 = lf - hi.astype(F32)
    mid = r1.astype(BF16)
    lo = (r1 - mid.astype(F32)).astype(BF16)
    cr = jnp.dot(sel, jnp.concatenate([hi, mid, lo], axis=0), preferred_element_type=F32)
    cc = cr[:CHUNK]
    return dict(
        ev=jnp.exp(cr[CHUNK:]),
        qd=q * jnp.exp(cc),
        ki=k * jnp.exp(-cc),
        iv=iv.astype(BF16),
        gate=gate * _sigmoid(gate))


def _hgrn_issue(front, heads):
    nsub = CHUNK // HG_SUB
    s1, s2, s3, s4 = HG_SUB, 2 * HG_SUB, 3 * HG_SUB, 4 * HG_SUB
    zeros_sub = jnp.zeros((HG_SUB, HG_DK), BF16)
    ev, qd, ki, iv = front['ev'], front['qd'], front['ki'], front['iv']

    def sub(a, j):
        return a[j * HG_SUB:(j + 1) * HG_SUB]

    def dk(sl, lo_row, hi_row):
        n = _HG_RANGES.index((lo_row, hi_row))
        return ev[n:n + 1, sl]

    scores, queries, upds = [], [], []
    for h in heads:
        sl = slice(h * HG_DK, (h + 1) * HG_DK)
        qd_h, ki_h = qd[:, sl], ki[:, sl]
        qd_b = qd_h.astype(BF16)
        ki_b = [sub(ki_h, j).astype(BF16) for j in range(nsub)]
        kend = [sub(ki_h, 0) * dk(sl, 0, s1), sub(ki_h, 1) * dk(sl, s1, s2),
                sub(ki_h, 2) * dk(sl, s2, s3), sub(ki_h, 3) * dk(sl, s3, s4)]
        kend_b = [kend[j].astype(BF16) for j in range(nsub - 1)]
        kmat = [
            [ki_b[0], zeros_sub, zeros_sub, zeros_sub],
            [kend_b[0], ki_b[1], zeros_sub, zeros_sub],
            [(kend[0] * dk(sl, s1, s2)).astype(BF16), kend_b[1], ki_b[2], zeros_sub],
            [(kend[0] * dk(sl, s1, s3)).astype(BF16), (kend[1] * dk(sl, s2, s3)).astype(BF16),
             kend_b[2], ki_b[3]],
        ]
        scores.append(lax.dot_general(qd_b, jnp.concatenate([p for row in kmat for p in row], axis=0),
                                      (((1,), (1,)), ((), ())), preferred_element_type=F32))
        qe = jnp.concatenate([sub(qd_h, 0), sub(qd_h, 1) * dk(sl, 0, s1), sub(qd_h, 2) * dk(sl, 0, s2),
                              sub(qd_h, 3) * dk(sl, 0, s3)], axis=0).astype(BF16)
        ke = jnp.concatenate([kend[0] * dk(sl, s1, s4), kend[1] * dk(sl, s2, s4),
                              kend[2] * dk(sl, s3, s4), kend[3]], axis=0).astype(BF16)
        queries.append(qe)
        upds.append(lax.dot_general(iv[:, sl], ke, (((0,), (0,)), ((), ())),
                                    preferred_element_type=F32))
    return scores, queries, upds


def _hgrn_consume(front, issued, heads, gain, st_ref):
    nsub = CHUNK // HG_SUB
    scores, queries, upds = issued
    ev, iv, gate = front['ev'], front['iv'], front['gate']
    chunk_row = _HG_RANGES.index((0, CHUNK))
    t_i = lax.broadcasted_iota(jnp.int32, (CHUNK, nsub * CHUNK), 0)
    c_i = lax.broadcasted_iota(jnp.int32, (CHUNK, nsub * CHUNK), 1)
    keep = ((t_i >> HG_SUB_SHIFT) == (c_i >> CHUNK_SHIFT)) & ((c_i & (CHUNK - 1)) <= t_i)
    inter = [lax.dot_general(queries[n], st_ref[h].astype(BF16), (((1,), (1,)), ((), ())),
                             preferred_element_type=F32) for n, h in enumerate(heads)]
    outs = []
    for n, h in enumerate(heads):
        sl = slice(h * HG_DK, (h + 1) * HG_DK)
        a = jnp.where(keep, scores[n], 0.0).astype(BF16)
        o = inter[n] + jnp.dot(a, jnp.concatenate([iv[:, sl]] * nsub, axis=0), preferred_element_type=F32)
        st_ref[h] = st_ref[h] * ev[chunk_row:chunk_row + 1, sl] + upds[n]
        outs.append((_rms(o, gain) * gate[:, sl]).astype(BF16))
    return outs


def _hgrn_kernel(sel_ref, lbl_ref, gain_ref, s0_ref, q_ref, f_ref, i_ref, g_ref, rec_ref, sout_ref,
                 st_ref, *, layer):
    for h in range(HG_HEADS):
        st_ref[h] = s0_ref[0, h].T

    def load():
        return q_ref[...], f_ref[...], i_ref[...], g_ref[...]

    def store(h, out):
        rec_ref[:, h * HG_DV:(h + 1) * HG_DV] = out

    for stage in _hgrn_stages(load, sel_ref[...], _lower_bound(lbl_ref, layer), gain_ref[...], st_ref, store):
        stage()
    for h in range(HG_HEADS):
        sout_ref[0, h] = st_ref[h].T


def _hgrn_single_chunk(lb_logits, gain, s0, z, *, n_seq, layer):
    n_layers = lb_logits.shape[0]
    sel = _hgrn_selector()

    def zspec(col):
        return pl.BlockSpec((CHUNK, D_HG), lambda b: (b, col))

    state_spec = pl.BlockSpec((1, HG_HEADS, HG_DK, HG_DV), lambda b: (b, 0, 0, 0))
    return pl.pallas_call(
        functools.partial(_hgrn_kernel, layer=layer),
        grid=(n_seq,),
        in_specs=[
            pl.BlockSpec(sel.shape, lambda b: (0, 0)),
            pl.BlockSpec((n_layers, D_HG), lambda b: (0, 0)),
            pl.BlockSpec((1, HG_DV), lambda b: (0, 0)),
            state_spec,
            zspec(Z_QH), zspec(Z_FH), zspec(Z_IH), zspec(Z_GH),
        ],
        out_specs=[pl.BlockSpec((CHUNK, D_HG), lambda b: (b, 0)), state_spec],
        out_shape=[
            jax.ShapeDtypeStruct((n_seq * CHUNK, D_HG), BF16),
            jax.ShapeDtypeStruct((n_seq, HG_HEADS, HG_DK, HG_DV), F32),
        ],
        scratch_shapes=[pltpu.VMEM((HG_HEADS, HG_DV, HG_DK), F32)],
        compiler_params=_cparams(("arbitrary",)),
        name="hgrn2_mixer",
    )(sel, lb_logits, gain, s0, z, z, z, z)


def _mixer_kernel(bucket_ref, table_ref, sinks_ref, sel_ref, lbl_ref, gain_ref, pre_ref, x_ref, w_ref,
                  attn_ref, rec_ref, kv_ref, sout_ref,
                  bias_ref, z_even, z_odd, xn_ref, k_all, v_all, st_ref, *, layer, units_per_seq):
    s = pl.program_id(0)
    unit = MIX_UNIT
    kv_w = N_KV * HEAD_DIM
    u_in_seq = (s + (units_per_seq - 1)) % units_per_seq

    @pl.when(s == 0)
    def _():
        _fill_bias(bucket_ref, table_ref, bias_ref)
        z_odd[...] = jnp.zeros(z_odd.shape, F32)
        k_all[0:WINDOW, :] = jnp.zeros((WINDOW, kv_w), BF16)
        v_all[0:WINDOW, :] = jnp.zeros((WINDOW, kv_w), BF16)

    @pl.when((s == 0) | (u_in_seq == 0))
    def _():
        st_ref[...] = jnp.zeros(st_ref.shape, F32)

    def step(z_write, z_read):
        xn_ref[...] = _rms(x_ref[...], pre_ref[...]).astype(BF16)

        def project(c):
            cols = slice(c * MIX_PROJ_TN, (c + 1) * MIX_PROJ_TN)
            z_write[:, cols] = jnp.dot(xn_ref[...], w_ref[:, cols], preferred_element_type=F32)

        k_cols = slice(Z_K * kv_w, (Z_K + 1) * kv_w)
        v_cols = slice(Z_V * kv_w, (Z_V + 1) * kv_w)
        lb = _lower_bound(lbl_ref, layer)

        def load_kv():
            k_all[WINDOW:, :] = z_read[:, k_cols].astype(BF16)
            v_all[WINDOW:, :] = z_read[:, v_cols].astype(BF16)
            kv_ref[:, 0:kv_w] = z_read[:, k_cols]
            kv_ref[:, kv_w:] = z_read[:, v_cols]

        def keep_history():
            k_all[0:WINDOW, :] = k_all[unit:unit + WINDOW, :]
            v_all[0:WINDOW, :] = v_all[unit:unit + WINDOW, :]

        def recur_stages(c):
            rows = slice(c * CHUNK, (c + 1) * CHUNK)

            def load():
                return tuple(z_read[rows, col * D_HG:(col + 1) * D_HG] for col in (Z_QH, Z_FH, Z_IH, Z_GH))

            def store(h, out):
                rec_ref[rows, h * HG_DV:(h + 1) * HG_DV] = out

            return _hgrn_stages(load, sel_ref[...], lb, gain_ref[...], st_ref, store)

        attend = _attend_stages(z_read, attn_ref, k_all, v_all, bias_ref, sinks_ref, u_in_seq * unit)
        recur = [t for c in range(unit // CHUNK) for t in recur_stages(c)]
        _interleave([functools.partial(project, c) for c in range(D_IN // MIX_PROJ_TN)],
                    [load_kv] + attend + [keep_history] + recur)

    @pl.when(s % 2 == 0)
    def _():
        step(z_even, z_odd)

    @pl.when(s % 2 == 1)
    def _():
        step(z_odd, z_even)

    @pl.when((s > 0) & (u_in_seq == units_per_seq - 1))
    def _():
        for h in range(HG_HEADS):
            sout_ref[0, h] = st_ref[h].T


def _mixer_front(x, pre, w_in, bucket, table, sinks, lb_logits, gain, *, n_seq, seq_len, layer):
    unit = MIX_UNIT
    ups = seq_len // unit
    n_units = n_seq * ups
    n_layers = lb_logits.shape[0]
    kv_w = N_KV * HEAD_DIM
    sel = _hgrn_selector()
    smem = pl.BlockSpec(memory_space=pltpu.SMEM)

    def const(shape):
        return pl.BlockSpec(shape, lambda s: (0,) * len(shape))

    def prev_unit(s):
        return (jnp.maximum(s - 1, 0), 0)

    return pl.pallas_call(
        functools.partial(_mixer_kernel, layer=layer, units_per_seq=ups),
        grid=(n_units + 1,),
        in_specs=[
            const((CHUNK, LK)), smem, smem, const(sel.shape), const((n_layers, D_HG)),
            const((1, HG_DV)), const((1, D_MODEL)),
            pl.BlockSpec((unit, D_MODEL), lambda s: (jnp.minimum(s, n_units - 1), 0)),
            pl.BlockSpec((D_MODEL, D_IN), lambda s: (0, 0), pipeline_mode=pl.Buffered(1)),
        ],
        out_specs=[
            pl.BlockSpec((unit, D_ATTN), prev_unit),
            pl.BlockSpec((unit, D_HG), prev_unit),
            pl.BlockSpec((unit, 2 * kv_w), prev_unit),
            pl.BlockSpec((1, HG_HEADS, HG_DK, HG_DV), lambda s: (jnp.maximum(s - 1, 0) // ups, 0, 0, 0)),
        ],
        out_shape=[
            jax.ShapeDtypeStruct((n_units * unit, D_ATTN), BF16),
            jax.ShapeDtypeStruct((n_units * unit, D_HG), BF16),
            jax.ShapeDtypeStruct((n_units * unit, 2 * kv_w), F32),
            jax.ShapeDtypeStruct((n_seq, HG_HEADS, HG_DK, HG_DV), F32),
        ],
        scratch_shapes=[
            pltpu.VMEM((N_KV, GROUP * CHUNK, LK), F32),
            pltpu.VMEM((unit, D_IN), F32),
            pltpu.VMEM((unit, D_IN), F32),
            pltpu.VMEM((unit, D_MODEL), BF16),
            pltpu.VMEM((WINDOW + unit, kv_w), BF16),
            pltpu.VMEM((WINDOW + unit, kv_w), BF16),
            pltpu.VMEM((HG_HEADS, HG_DV, HG_DK), F32),
        ],
        compiler_params=_cparams(("arbitrary",)),
        name="mixer_front",
    )(bucket, table, sinks, sel, lb_logits, gain, pre, x, w_in)


def _outproj_kernel(x_ref, a_ref, r_ref, w_ref, post_ref, o_ref):
    nsub = x_ref.shape[0] // SUB_ROWS
    sub = [slice(r * SUB_ROWS, (r + 1) * SUB_ROWS) for r in range(nsub)]
    mix = [None] * nsub

    def project(r):
        mix[r] = (jnp.dot(a_ref[sub[r], :], w_ref[0:D_ATTN, :], preferred_element_type=F32)
                  + jnp.dot(r_ref[sub[r], :], w_ref[D_ATTN:, :], preferred_element_type=F32))

    def finish(r):
        o_ref[sub[r], :] = x_ref[sub[r], :] + _rms(mix[r], post_ref[...])

    _pipelined(nsub, None, project, finish)


def _outproj(x, attn, rec, w_out, post):
    t = x.shape[0]
    tm = min(ROW_TM, t)
    return pl.pallas_call(
        _outproj_kernel,
        grid=(t // tm,),
        in_specs=[
            pl.BlockSpec((tm, D_MODEL), lambda i: (i, 0)),
            pl.BlockSpec((tm, D_ATTN), lambda i: (i, 0)),
            pl.BlockSpec((tm, D_HG), lambda i: (i, 0)),
            pl.BlockSpec((D_MODEL, D_MODEL), lambda i: (0, 0), pipeline_mode=pl.Buffered(1)),
            pl.BlockSpec((1, D_MODEL), lambda i: (0, 0)),
        ],
        out_specs=pl.BlockSpec((tm, D_MODEL), lambda i: (i, 0)),
        out_shape=jax.ShapeDtypeStruct((t, D_MODEL), F32),
        compiler_params=_cparams(("parallel",)),
        name="mixer_out_proj",
    )(x, attn, rec, w_out, post)


def _ple_kernel(x_ref, p_ref, pre_ref, post_ref, wg_ref, wp_ref, o_ref):
    nsub = x_ref.shape[0] // SUB_ROWS
    sub = [slice(r * SUB_ROWS, (r + 1) * SUB_ROWS) for r in range(nsub)]
    hs, gates, projs = [None] * nsub, [None] * nsub, [None] * nsub

    def norm(r):
        hs[r] = _rms(x_ref[sub[r], :], pre_ref[...]).astype(BF16)

    def project(r):
        gates[r] = jnp.dot(hs[r], wg_ref[...], preferred_element_type=F32)
        projs[r] = jnp.dot(p_ref[sub[r], :].astype(BF16), wp_ref[...], preferred_element_type=F32)

    def finish(r):
        o_ref[sub[r], :] = x_ref[sub[r], :] + _rms(_sigmoid(gates[r]) * projs[r], post_ref[...])

    _pipelined(nsub, norm, project, finish)


def _ple(x, p, pre, post, w_gate, w_proj):
    t = x.shape[0]
    tm = min(ROW_TM, t)
    return pl.pallas_call(
        _ple_kernel,
        grid=(t // tm,),
        in_specs=[
            pl.BlockSpec((tm, D_MODEL), lambda i: (i, 0)),
            pl.BlockSpec((tm, PLE_DIM), lambda i: (i, 0)),
            pl.BlockSpec((1, D_MODEL), lambda i: (0, 0)),
            pl.BlockSpec((1, D_MODEL), lambda i: (0, 0)),
            pl.BlockSpec((D_MODEL, D_MODEL), lambda i: (0, 0), pipeline_mode=pl.Buffered(1)),
            pl.BlockSpec((PLE_DIM, D_MODEL), lambda i: (0, 0)),
        ],
        out_specs=pl.BlockSpec((tm, D_MODEL), lambda i: (i, 0)),
        out_shape=jax.ShapeDtypeStruct((t, D_MODEL), F32),
        compiler_params=_cparams(("parallel",)),
        name="ple_embed",
    )(x, p, pre, post, w_gate, w_proj)


def _t5_bucket(rel):
    half = NUM_BUCKETS // 2
    max_exact = half // 2
    n = jnp.abs(rel)
    nf = jnp.maximum(n, 1).astype(jnp.float32)
    large = max_exact + (jnp.log(nf / max_exact) / math.log(MAX_DISTANCE / max_exact)
                         * (half - max_exact)).astype(jnp.int32)
    large = jnp.minimum(large, half - 1)
    return jnp.where(rel > 0, half, 0) + jnp.where(n < max_exact, n, large)


def _ffn1(x, w):
    row = lambda v: v.reshape(1, -1)
    return _ffn(x, row(w['ffn1_pre']), row(w['ffn1_post']), w['ffn1_w_gate'], w['ffn1_w_up'], w['ffn1_w_down'])


def _ffn2_ple(x, p, w):
    row = lambda v: v.reshape(1, -1)
    x = _ffn(x, row(w['ffn2_pre']), row(w['ffn2_post']), w['ffn2_w_gate'], w['ffn2_w_up'], w['ffn2_w_down'])
    return _ple(x, p, row(w['ple_pre']), row(w['ple_post']), w['w_ple_gate'], w['w_ple_proj'])


def _prompt_layer(x, p, w, table, bucket, layer, *, n_seq, seq_len):
    row = lambda v: v.reshape(1, -1)
    x = _ffn1(x, w)
    attn, rec, kv, s_new = _mixer_front(
        x, row(w['mix_pre']), w['w_in'], bucket, table, w['attn_sinks'], w['hgrn_lb_logits'],
        row(w['hgrn_norm']), n_seq=n_seq, seq_len=seq_len, layer=layer)
    x = _outproj(x, attn, rec, w['w_out'], row(w['mix_post']))
    return _ffn2_ple(x, p, w), kv, s_new


def _sample_layer(x, p, w, table, bucket, layer, *, n_seq, s0, k_cache, v_cache):
    row = lambda v: v.reshape(1, -1)
    kv_w = N_KV * HEAD_DIM
    x = _ffn1(x, w)
    z = _inproj(x, row(w['mix_pre']), w['w_in'])
    attn = _attention_cached(bucket, table, w['attn_sinks'], z, k_cache.reshape(n_seq * WINDOW, kv_w),
                             v_cache.reshape(n_seq * WINDOW, kv_w), n_seq=n_seq)
    rec, s_new = _hgrn_single_chunk(w['hgrn_lb_logits'], row(w['hgrn_norm']), s0, z, n_seq=n_seq, layer=layer)
    x = _outproj(x, attn, rec, w['w_out'], row(w['mix_post']))
    return _ffn2_ple(x, p, w), z[:, Z_K * kv_w:], s_new


def kernel(x_prompt, x_sample, cache_attn_k, cache_attn_v, state_hgrn, p_prompt, p_sample,
           rel_bias_table, ffn1_pre, ffn1_post, ffn1_w_gate, ffn1_w_up, ffn1_w_down,
           mix_pre, mix_post, w_in, w_out, attn_sinks, hgrn_lb_logits, hgrn_norm,
           ffn2_pre, ffn2_post, ffn2_w_gate, ffn2_w_up, ffn2_w_down,
           ple_pre, ple_post, w_ple_gate, w_ple_proj):
    depth = w_in.shape[0]
    bp, sp, _ = x_prompt.shape
    bs, ss, _ = x_sample.shape
    wc = cache_attn_k.shape[2]
    assert wc == WINDOW and ss == CHUNK and sp % MIX_UNIT == 0 and sp >= WINDOW

    rel = jnp.arange(LK)[None, :] - WINDOW - jnp.arange(CHUNK)[:, None]
    bucket = _t5_bucket(rel).astype(jnp.int32)

    kv_w = N_KV * HEAD_DIM
    k_lo, v_lo = D_ATTN, D_ATTN + kv_w
    yp = x_prompt.reshape(bp * sp, D_MODEL)
    ys = x_sample.reshape(bs * ss, D_MODEL)
    outs = [[] for _ in range(6)]
    for l in range(depth):
        w_in_l = w_in[l]
        w = {
            'ffn1_pre': ffn1_pre[l], 'ffn1_post': ffn1_post[l],
            'ffn1_w_gate': ffn1_w_gate[l].astype(BF16), 'ffn1_w_up': ffn1_w_up[l].astype(BF16),
            'ffn1_w_down': ffn1_w_down[l].astype(BF16),
            'mix_pre': mix_pre[l], 'mix_post': mix_post[l],
            'w_in': jnp.concatenate([w_in_l[:, :k_lo], w_in_l[:, v_lo + kv_w:], w_in_l[:, k_lo:v_lo + kv_w]],
                                    axis=1).astype(BF16),
            'w_out': w_out[l].astype(BF16), 'attn_sinks': attn_sinks[l],
            'hgrn_lb_logits': hgrn_lb_logits, 'hgrn_norm': hgrn_norm[l],
            'ffn2_pre': ffn2_pre[l], 'ffn2_post': ffn2_post[l],
            'ffn2_w_gate': ffn2_w_gate[l].astype(BF16), 'ffn2_w_up': ffn2_w_up[l].astype(BF16),
            'ffn2_w_down': ffn2_w_down[l].astype(BF16),
            'ple_pre': ple_pre[l], 'ple_post': ple_post[l],
            'w_ple_gate': w_ple_gate[l].astype(BF16), 'w_ple_proj': w_ple_proj[l].astype(BF16),
        }
        yp, kv_p, st_p = _prompt_layer(yp, p_prompt[l].reshape(bp * sp, PLE_DIM), w, rel_bias_table, bucket, l,
                                       n_seq=bp, seq_len=sp)
        ys, kv_s, st_s = _sample_layer(ys, p_sample[l].reshape(bs * ss, PLE_DIM), w, rel_bias_table, bucket, l,
                                       n_seq=bs, s0=state_hgrn[l], k_cache=cache_attn_k[l],
                                       v_cache=cache_attn_v[l])
        kv_p = kv_p.reshape(bp, sp, 2 * kv_w)[:, sp - WINDOW:].reshape(bp, WINDOW, 2, N_KV, HEAD_DIM)
        outs[0].append(kv_p[:, :, 0])
        outs[1].append(kv_p[:, :, 1])
        outs[2].append(st_p)
        kv_s = kv_s.reshape(bs, ss, 2, N_KV, HEAD_DIM)
        outs[3].append(jnp.concatenate([cache_attn_k[l], kv_s[:, :, 0]], axis=1)[:, ss:])
        outs[4].append(jnp.concatenate([cache_attn_v[l], kv_s[:, :, 1]], axis=1)[:, ss:])
        outs[5].append(st_s)
    return (yp.reshape(bp, sp, D_MODEL), ys.reshape(bs, ss, D_MODEL),
            jnp.stack(outs[0]), jnp.stack(outs[1]), jnp.stack(outs[2]),
            jnp.stack(outs[3]), jnp.stack(outs[4]), jnp.stack(outs[5]))
```

```python
import functools
import math

import jax
import jax.numpy as jnp
import numpy as np
from jax import lax
from jax.experimental import pallas as pl
from jax.experimental.pallas import tpu as pltpu

F32 = jnp.float32
BF16 = jnp.bfloat16

D_MODEL = 2048
CHUNK = 64
D_ATTN = 1024
HEAD_DIM = 128
N_HEADS = 8
N_KV = 2
GROUP = N_HEADS // N_KV
WINDOW = 128
LK = WINDOW + CHUNK
NUM_BUCKETS = 32
MAX_DISTANCE = 128
HG_DK = 128
HG_DV = 128
HG_HEADS = 8
D_HG = HG_HEADS * HG_DV
HG_SUB = 16
HG_SUB_SHIFT = HG_SUB.bit_length() - 1
CHUNK_SHIFT = CHUNK.bit_length() - 1
D_FF = 5632
PLE_DIM = 256
EPS = 1e-6
NEG_INF = -1e30
D_IN = D_ATTN + 2 * N_KV * HEAD_DIM + 4 * D_HG

Z_QH, Z_FH, Z_IH, Z_GH = 1, 2, 3, 4
Z_K, Z_V = 20, 21

VMEM_LIMIT_BYTES = 60 * 1024 * 1024

FFN_TM = 1024
FFN_TF = 512
ROW_TM = 1024
SUB_ROWS = 256
MIX_UNIT = 256
MIX_PROJ_TN = 512


def _rms(x, gain):
    y = x * lax.rsqrt(jnp.mean(x * x, axis=-1, keepdims=True) + EPS)
    return y * gain


def _sigmoid(x):
    return 1.0 / (1.0 + jnp.exp(-x))


def _cparams(sem):
    return pltpu.CompilerParams(dimension_semantics=sem, vmem_limit_bytes=VMEM_LIMIT_BYTES)


def _pipelined(n, before, matmul, after):
    if before is not None:
        before(0)
    for r in range(n):
        if before is not None and r + 1 < n:
            before(r + 1)
        matmul(r)
        if after is not None and r > 0:
            after(r - 1)
    if after is not None:
        after(n - 1)


def _ffn_kernel(x_ref, pre_ref, post_ref, wg_ref, wu_ref, wd_ref, o_ref, xn_ref):
    j = pl.program_id(1)
    last = pl.num_programs(1) - 1
    nsub = x_ref.shape[0] // SUB_ROWS
    sub = [slice(r * SUB_ROWS, (r + 1) * SUB_ROWS) for r in range(nsub)]

    def swiglu(rows):
        xn = xn_ref[rows, :]
        g = jnp.dot(xn, wg_ref[...], preferred_element_type=F32)
        u = jnp.dot(xn, wu_ref[...], preferred_element_type=F32)
        h = (g * _sigmoid(g) * u).astype(BF16)
        return jnp.dot(h, wd_ref[...], preferred_element_type=F32)

    @pl.when(j == 0)
    def _():
        def norm(r):
            xn_ref[sub[r], :] = _rms(x_ref[sub[r], :], pre_ref[...]).astype(BF16)

        def first(r):
            o_ref[sub[r], :] = swiglu(sub[r])

        _pipelined(nsub, norm, first, None)

    @pl.when((j > 0) & (j < last))
    def _():
        o_ref[...] += swiglu(slice(None))

    @pl.when(j == last)
    def _():
        ys = [None] * nsub

        def final(r):
            ys[r] = o_ref[sub[r], :] + swiglu(sub[r])

        def finish(r):
            o_ref[sub[r], :] = x_ref[sub[r], :] + 0.5 * _rms(ys[r], post_ref[...])

        _pipelined(nsub, None, final, finish)


def _ffn(x, pre, post, wg, wu, wd):
    t = x.shape[0]
    tm = min(FFN_TM, t)
    grid = (t // tm, D_FF // FFN_TF)
    assert grid[1] >= 3
    return pl.pallas_call(
        _ffn_kernel,
        grid=grid,
        in_specs=[
            pl.BlockSpec((tm, D_MODEL), lambda i, j: (i, 0)),
            pl.BlockSpec((1, D_MODEL), lambda i, j: (0, 0)),
            pl.BlockSpec((1, D_MODEL), lambda i, j: (0, 0)),
            pl.BlockSpec((D_MODEL, FFN_TF), lambda i, j: (0, j)),
            pl.BlockSpec((D_MODEL, FFN_TF), lambda i, j: (0, j)),
            pl.BlockSpec((FFN_TF, D_MODEL), lambda i, j: (j, 0)),
        ],
        out_specs=pl.BlockSpec((tm, D_MODEL), lambda i, j: (i, 0)),
        out_shape=jax.ShapeDtypeStruct((t, D_MODEL), F32),
        scratch_shapes=[pltpu.VMEM((tm, D_MODEL), BF16)],
        compiler_params=_cparams(("parallel", "arbitrary")),
        name="ffn_half_step",
    )(x, pre, post, wg, wu, wd)


def _fill_bias(bucket_ref, table_ref, bias_ref):
    bucket = bucket_ref[...]
    for h in range(N_HEADS):
        acc = jnp.zeros((CHUNK, LK), F32)
        for n in range(NUM_BUCKETS):
            acc = jnp.where(bucket == n, table_ref[n, h], acc)
        g, hh = divmod(h, GROUP)
        bias_ref[g, hh * CHUNK:(hh + 1) * CHUNK, :] = acc


def _sink_column(sinks_ref, g):
    row = lax.broadcasted_iota(jnp.int32, (GROUP * CHUNK, 1), 0)
    sink = jnp.zeros((GROUP * CHUNK, 1), F32)
    for hh in range(GROUP):
        sink = jnp.where((row >> CHUNK_SHIFT) == hh, sinks_ref[g * GROUP + hh], sink)
    return sink


def _attend_chunk(q_of_head, kk, vv, bias_g, sink, first_key_pos):
    qs = jnp.concatenate([q_of_head(hh) for hh in range(GROUP)], axis=0).astype(BF16)
    s = lax.dot_general(qs, kk, (((1,), (1,)), ((), ())), preferred_element_type=F32)
    s = s * (HEAD_DIM ** -0.5) + bias_g
    if first_key_pos is not None:
        col = lax.broadcasted_iota(jnp.int32, (GROUP * CHUNK, LK), 1)
        s = jnp.where(col + first_key_pos >= 0, s, NEG_INF)
    m = jnp.maximum(jnp.max(s, axis=-1, keepdims=True), sink)
    e = jnp.exp(s - m)
    p = e / (jnp.sum(e, axis=-1, keepdims=True) + jnp.exp(sink - m))
    return jnp.dot(p.astype(BF16), vv, preferred_element_type=F32)


def _attend_stages(q_ref, o_ref, k_all, v_all, bias_ref, sinks_ref, pos0, win_stride=CHUNK):
    def stage(c, g):
        rows = slice(c * CHUNK, (c + 1) * CHUNK)
        cols = slice(g * HEAD_DIM, (g + 1) * HEAD_DIM)
        win = slice(c * win_stride, c * win_stride + LK)
        masked = pos0 is not None and c * CHUNK < WINDOW

        def q_of_head(hh):
            h = g * GROUP + hh
            return q_ref[rows, h * HEAD_DIM:(h + 1) * HEAD_DIM]

        o = _attend_chunk(q_of_head, k_all[win, cols], v_all[win, cols], bias_ref[g], _sink_column(sinks_ref, g),
                          pos0 + (c * CHUNK - WINDOW) if masked else None)
        for hh in range(GROUP):
            h = g * GROUP + hh
            o_ref[rows, h * HEAD_DIM:(h + 1) * HEAD_DIM] = o[hh * CHUNK:(hh + 1) * CHUNK, :].astype(BF16)

    return [functools.partial(stage, c, g) for c in range(q_ref.shape[0] // CHUNK) for g in range(N_KV)]


def _interleave(a, b):
    ia = ib = 0
    while ia < len(a) or ib < len(b):
        if ib >= len(b) or (ia < len(a) and ia * len(b) <= ib * len(a)):
            a[ia]()
            ia += 1
        else:
            b[ib]()
            ib += 1


_HG_RANGES = [(0, 16), (0, 32), (0, 48), (0, 64), (16, 32), (32, 48), (48, 64), (16, 48), (16, 64), (32, 64)]
_HG_RANGE_ROWS = 16


def _hgrn_selector():
    t = np.arange(CHUNK)
    within = (t[:, None] // HG_SUB == t[None, :] // HG_SUB) & (t[None, :] <= t[:, None])
    ranges = np.zeros((_HG_RANGE_ROWS, CHUNK), bool)
    for n, (lo, hi) in enumerate(_HG_RANGES):
        ranges[n, lo:hi] = True
    once = np.concatenate([within, ranges], axis=0)
    return jnp.asarray(np.concatenate([once, once, once], axis=1), BF16)


def _lower_bound(lbl_ref, layer):
    logits = lbl_ref[...]
    ex = jnp.exp(logits - jnp.max(logits, axis=0, keepdims=True))
    return jnp.sum(ex[:layer + 1], axis=0, keepdims=True) / jnp.sum(ex, axis=0, keepdims=True)


def _hgrn_chunk(q, fl, iv, gate, sel, lb, gain, st_ref):
    nsub = CHUNK // HG_SUB
    s1, s2, s3, s4 = HG_SUB, 2 * HG_SUB, 3 * HG_SUB, 4 * HG_SUB
    t_i = lax.broadcasted_iota(jnp.int32, (CHUNK, nsub * CHUNK), 0)
    c_i = lax.broadcasted_iota(jnp.int32, (CHUNK, nsub * CHUNK), 1)
    keep = ((t_i >> HG_SUB_SHIFT) == (c_i >> CHUNK_SHIFT)) & ((c_i & (CHUNK - 1)) <= t_i)
    zeros_sub = jnp.zeros((HG_SUB, HG_DK), BF16)

    def sub(a, j):
        return a[j * HG_SUB:(j + 1) * HG_SUB]

    f = lb + (1.0 - lb) * _sigmoid(fl)
    lf = jnp.log(f)
    k = 1.0 - f
    hi = lf.astype(BF16)
    r1 = lf - hi.astype(F32)
    mid = r1.astype(BF16)
    lo = (r1 - mid.astype(F32)).astype(BF16)
    cr = jnp.dot(sel, jnp.concatenate([hi, mid, lo], axis=0), preferred_element_type=F32)
    cc = cr[:CHUNK]
    ev = jnp.exp(cr[CHUNK:])
    qd = q * jnp.exp(cc)
    ki = k * jnp.exp(-cc)
    iv = iv.astype(BF16)
    gate = gate * _sigmoid(gate)
    head_cols = [slice(h * HG_DK, (h + 1) * HG_DK) for h in range(HG_HEADS)]

    def dk(sl, lo_row, hi_row):
        n = _HG_RANGES.index((lo_row, hi_row))
        return ev[n:n + 1, sl]

    scores, inter, upds = [], [], []
    for h, sl in enumerate(head_cols):
        qd_h, ki_h = qd[:, sl], ki[:, sl]
        qd_b = qd_h.astype(BF16)
        ki_b = [sub(ki_h, j).astype(BF16) for j in range(nsub)]
        kend = [sub(ki_h, 0) * dk(sl, 0, s1), sub(ki_h, 1) * dk(sl, s1, s2),
                sub(ki_h, 2) * dk(sl, s2, s3), sub(ki_h, 3) * dk(sl, s3, s4)]
        kend_b = [kend[j].astype(BF16) for j in range(nsub - 1)]
        kmat = [
            [ki_b[0], zeros_sub, zeros_sub, zeros_sub],
            [kend_b[0], ki_b[1], zeros_sub, zeros_sub],
            [(kend[0] * dk(sl, s1, s2)).astype(BF16), kend_b[1], ki_b[2], zeros_sub],
            [(kend[0] * dk(sl, s1, s3)).astype(BF16), (kend[1] * dk(sl, s2, s3)).astype(BF16),
             kend_b[2], ki_b[3]],
        ]
        scores.append(lax.dot_general(qd_b, jnp.concatenate([p for row in kmat for p in row], axis=0),
                                      (((1,), (1,)), ((), ())), preferred_element_type=F32))
        qe = jnp.concatenate([sub(qd_h, 0), sub(qd_h, 1) * dk(sl, 0, s1), sub(qd_h, 2) * dk(sl, 0, s2),
                              sub(qd_h, 3) * dk(sl, 0, s3)], axis=0).astype(BF16)
        ke = jnp.concatenate([kend[0] * dk(sl, s1, s4), kend[1] * dk(sl, s2, s4),
                              kend[2] * dk(sl, s3, s4), kend[3]], axis=0).astype(BF16)
        inter.append(lax.dot_general(qe, st_ref[h].astype(BF16), (((1,), (1,)), ((), ())),
                                     preferred_element_type=F32))
        upds.append(lax.dot_general(iv[:, sl], ke, (((0,), (0,)), ((), ())),
                                    preferred_element_type=F32))
    outs = []
    for h, sl in enumerate(head_cols):
        a = jnp.where(keep, scores[h], 0.0).astype(BF16)
        outs.append(inter[h] + jnp.dot(a, jnp.concatenate([iv[:, sl]] * nsub, axis=0),
                                       preferred_element_type=F32))
        st_ref[h] = st_ref[h] * dk(sl, 0, s4) + upds[h]
    return [(_rms(outs[h], gain) * gate[:, sl]).astype(BF16) for h, sl in enumerate(head_cols)]


def _mixer_kernel(bucket_ref, table_ref, sinks_ref, sel_ref, lbl_ref, gain_ref, pre_ref, x_ref, w_ref,
                  attn_ref, rec_ref, kv_ref, sout_ref,
                  bias_ref, z_even, z_odd, xn_ref, k_all, v_all, st_ref, *, layer, units_per_seq):
    s = pl.program_id(0)
    unit = MIX_UNIT
    kv_w = N_KV * HEAD_DIM
    u_in_seq = (s + (units_per_seq - 1)) % units_per_seq

    @pl.when(s == 0)
    def _():
        _fill_bias(bucket_ref, table_ref, bias_ref)
        z_odd[...] = jnp.zeros(z_odd.shape, F32)
        k_all[0:WINDOW, :] = jnp.zeros((WINDOW, kv_w), BF16)
        v_all[0:WINDOW, :] = jnp.zeros((WINDOW, kv_w), BF16)

    @pl.when((s == 0) | (u_in_seq == 0))
    def _():
        st_ref[...] = jnp.zeros(st_ref.shape, F32)

    def step(z_write, z_read):
        xn_ref[...] = _rms(x_ref[...], pre_ref[...]).astype(BF16)

        def project(c):
            cols = slice(c * MIX_PROJ_TN, (c + 1) * MIX_PROJ_TN)
            z_write[:, cols] = jnp.dot(xn_ref[...], w_ref[:, cols], preferred_element_type=F32)

        k_cols = slice(Z_K * kv_w, (Z_K + 1) * kv_w)
        v_cols = slice(Z_V * kv_w, (Z_V + 1) * kv_w)
        lb = _lower_bound(lbl_ref, layer)

        def load_kv():
            k_all[WINDOW:, :] = z_read[:, k_cols].astype(BF16)
            v_all[WINDOW:, :] = z_read[:, v_cols].astype(BF16)
            kv_ref[:, 0:kv_w] = z_read[:, k_cols]
            kv_ref[:, kv_w:] = z_read[:, v_cols]

        def keep_history():
            k_all[0:WINDOW, :] = k_all[unit:unit + WINDOW, :]
            v_all[0:WINDOW, :] = v_all[unit:unit + WINDOW, :]

        def recur(c):
            rows = slice(c * CHUNK, (c + 1) * CHUNK)

            def zcols(col):
                return z_read[rows, col * D_HG:(col + 1) * D_HG]

            outs = _hgrn_chunk(zcols(Z_QH), zcols(Z_FH), zcols(Z_IH), zcols(Z_GH), sel_ref[...],
                               lb, gain_ref[...], st_ref)
            for h in range(HG_HEADS):
                rec_ref[rows, h * HG_DV:(h + 1) * HG_DV] = outs[h]

        mixers = ([load_kv]
                  + _attend_stages(z_read, attn_ref, k_all, v_all, bias_ref, sinks_ref, u_in_seq * unit)
                  + [keep_history]
                  + [functools.partial(recur, c) for c in range(unit // CHUNK)])
        _interleave([functools.partial(project, c) for c in range(D_IN // MIX_PROJ_TN)], mixers)

    @pl.when(s % 2 == 0)
    def _():
        step(z_even, z_odd)

    @pl.when(s % 2 == 1)
    def _():
        step(z_odd, z_even)

    @pl.when((s > 0) & (u_in_seq == units_per_seq - 1))
    def _():
        for h in range(HG_HEADS):
            sout_ref[0, h] = st_ref[h].T


def _mixer_front(x, pre, w_in, bucket, table, sinks, lb_logits, gain, *, n_seq, seq_len, layer):
    unit = MIX_UNIT
    ups = seq_len // unit
    n_units = n_seq * ups
    n_layers = lb_logits.shape[0]
    kv_w = N_KV * HEAD_DIM
    sel = _hgrn_selector()
    smem = pl.BlockSpec(memory_space=pltpu.SMEM)

    def const(shape):
        return pl.BlockSpec(shape, lambda s: (0,) * len(shape))

    def prev_unit(s):
        return (jnp.maximum(s - 1, 0), 0)

    return pl.pallas_call(
        functools.partial(_mixer_kernel, layer=layer, units_per_seq=ups),
        grid=(n_units + 1,),
        in_specs=[
            const((CHUNK, LK)), smem, smem, const(sel.shape), const((n_layers, D_HG)),
            const((1, HG_DV)), const((1, D_MODEL)),
            pl.BlockSpec((unit, D_MODEL), lambda s: (jnp.minimum(s, n_units - 1), 0)),
            pl.BlockSpec((D_MODEL, D_IN), lambda s: (0, 0), pipeline_mode=pl.Buffered(1)),
        ],
        out_specs=[
            pl.BlockSpec((unit, D_ATTN), prev_unit),
            pl.BlockSpec((unit, D_HG), prev_unit),
            pl.BlockSpec((unit, 2 * kv_w), prev_unit),
            pl.BlockSpec((1, HG_HEADS, HG_DK, HG_DV), lambda s: (jnp.maximum(s - 1, 0) // ups, 0, 0, 0)),
        ],
        out_shape=[
            jax.ShapeDtypeStruct((n_units * unit, D_ATTN), BF16),
            jax.ShapeDtypeStruct((n_units * unit, D_HG), BF16),
            jax.ShapeDtypeStruct((n_units * unit, 2 * kv_w), F32),
            jax.ShapeDtypeStruct((n_seq, HG_HEADS, HG_DK, HG_DV), F32),
        ],
        scratch_shapes=[
            pltpu.VMEM((N_KV, GROUP * CHUNK, LK), F32),
            pltpu.VMEM((unit, D_IN), F32),
            pltpu.VMEM((unit, D_IN), F32),
            pltpu.VMEM((unit, D_MODEL), BF16),
            pltpu.VMEM((WINDOW + unit, kv_w), BF16),
            pltpu.VMEM((WINDOW + unit, kv_w), BF16),
            pltpu.VMEM((HG_HEADS, HG_DV, HG_DK), F32),
        ],
        compiler_params=_cparams(("arbitrary",)),
        name="mixer_front",
    )(bucket, table, sinks, sel, lb_logits, gain, pre, x, w_in)


def _mixer_cached_kernel(bucket_ref, table_ref, sinks_ref, sel_ref, lbl_ref, gain_ref, pre_ref, x_ref, w_ref,
                         kp_ref, vp_ref, s0_ref,
                         attn_ref, rec_ref, kv_ref, sout_ref,
                         bias_ref, z_even, z_odd, xn_ref, k_win, v_win, st_ref, *, layer):
    s = pl.program_id(0)
    unit = MIX_UNIT
    kv_w = N_KV * HEAD_DIM

    @pl.when(s == 0)
    def _():
        _fill_bias(bucket_ref, table_ref, bias_ref)
        z_odd[...] = jnp.zeros(z_odd.shape, F32)

    def step(z_write, z_read):
        xn_ref[...] = _rms(x_ref[...], pre_ref[...]).astype(BF16)

        def project(c):
            cols = slice(c * MIX_PROJ_TN, (c + 1) * MIX_PROJ_TN)
            z_write[:, cols] = jnp.dot(xn_ref[...], w_ref[:, cols], preferred_element_type=F32)

        k_cols = slice(Z_K * kv_w, (Z_K + 1) * kv_w)
        v_cols = slice(Z_V * kv_w, (Z_V + 1) * kv_w)
        lb = _lower_bound(lbl_ref, layer)

        def load_kv():
            for c in range(unit // CHUNK):
                hist = slice(c * WINDOW, (c + 1) * WINDOW)
                rows = slice(c * CHUNK, (c + 1) * CHUNK)
                k_win[c * LK:c * LK + WINDOW, :] = kp_ref[hist, :].astype(BF16)
                v_win[c * LK:c * LK + WINDOW, :] = vp_ref[hist, :].astype(BF16)
                k_win[c * LK + WINDOW:(c + 1) * LK, :] = z_read[rows, k_cols].astype(BF16)
                v_win[c * LK + WINDOW:(c + 1) * LK, :] = z_read[rows, v_cols].astype(BF16)
            kv_ref[:, 0:kv_w] = z_read[:, k_cols]
            kv_ref[:, kv_w:] = z_read[:, v_cols]

        def recur(c):
            rows = slice(c * CHUNK, (c + 1) * CHUNK)
            for h in range(HG_HEADS):
                st_ref[h] = s0_ref[c, h].T

            def zcols(col):
                return z_read[rows, col * D_HG:(col + 1) * D_HG]

            outs = _hgrn_chunk(zcols(Z_QH), zcols(Z_FH), zcols(Z_IH), zcols(Z_GH), sel_ref[...],
                               lb, gain_ref[...], st_ref)
            for h in range(HG_HEADS):
                rec_ref[rows, h * HG_DV:(h + 1) * HG_DV] = outs[h]
                sout_ref[c, h] = st_ref[h].T

        mixers = ([load_kv]
                  + _attend_stages(z_read, attn_ref, k_win, v_win, bias_ref, sinks_ref, None, win_stride=LK)
                  + [functools.partial(recur, c) for c in range(unit // CHUNK)])
        _interleave([functools.partial(project, c) for c in range(D_IN // MIX_PROJ_TN)], mixers)

    @pl.when(s % 2 == 0)
    def _():
        step(z_even, z_odd)

    @pl.when(s % 2 == 1)
    def _():
        step(z_odd, z_even)


def _mixer_front_cached(x, pre, w_in, bucket, table, sinks, lb_logits, gain, s0, k_cache, v_cache, *,
                        n_seq, layer):
    unit = MIX_UNIT
    per_unit = unit // CHUNK
    n_units = n_seq // per_unit
    n_layers = lb_logits.shape[0]
    kv_w = N_KV * HEAD_DIM
    sel = _hgrn_selector()
    smem = pl.BlockSpec(memory_space=pltpu.SMEM)

    def const(shape):
        return pl.BlockSpec(shape, lambda s: (0,) * len(shape))

    def prev_unit(s):
        return (jnp.maximum(s - 1, 0), 0)

    state_spec = pl.BlockSpec((per_unit, HG_HEADS, HG_DK, HG_DV), lambda s: (jnp.maximum(s - 1, 0), 0, 0, 0))
    return pl.pallas_call(
        functools.partial(_mixer_cached_kernel, layer=layer),
        grid=(n_units + 1,),
        in_specs=[
            const((CHUNK, LK)), smem, smem, const(sel.shape), const((n_layers, D_HG)),
            const((1, HG_DV)), const((1, D_MODEL)),
            pl.BlockSpec((unit, D_MODEL), lambda s: (jnp.minimum(s, n_units - 1), 0)),
            pl.BlockSpec((D_MODEL, D_IN), lambda s: (0, 0), pipeline_mode=pl.Buffered(1)),
            pl.BlockSpec((per_unit * WINDOW, kv_w), prev_unit),
            pl.BlockSpec((per_unit * WINDOW, kv_w), prev_unit),
            state_spec,
        ],
        out_specs=[
            pl.BlockSpec((unit, D_ATTN), prev_unit),
            pl.BlockSpec((unit, D_HG), prev_unit),
            pl.BlockSpec((unit, 2 * kv_w), prev_unit),
            state_spec,
        ],
        out_shape=[
            jax.ShapeDtypeStruct((n_units * unit, D_ATTN), BF16),
            jax.ShapeDtypeStruct((n_units * unit, D_HG), BF16),
            jax.ShapeDtypeStruct((n_units * unit, 2 * kv_w), F32),
            jax.ShapeDtypeStruct((n_seq, HG_HEADS, HG_DK, HG_DV), F32),
        ],
        scratch_shapes=[
            pltpu.VMEM((N_KV, GROUP * CHUNK, LK), F32),
            pltpu.VMEM((unit, D_IN), F32),
            pltpu.VMEM((unit, D_IN), F32),
            pltpu.VMEM((unit, D_MODEL), BF16),
            pltpu.VMEM((per_unit * LK, kv_w), BF16),
            pltpu.VMEM((per_unit * LK, kv_w), BF16),
            pltpu.VMEM((HG_HEADS, HG_DV, HG_DK), F32),
        ],
        compiler_params=_cparams(("arbitrary",)),
        name="mixer_front_cached",
    )(bucket, table, sinks, sel, lb_logits, gain, pre, x, w_in, k_cache, v_cache, s0)


def _outproj_kernel(x_ref, a_ref, r_ref, w_ref, post_ref, o_ref):
    nsub = x_ref.shape[0] // SUB_ROWS
    sub = [slice(r * SUB_ROWS, (r + 1) * SUB_ROWS) for r in range(nsub)]
    mix = [None] * nsub

    def project(r):
        mix[r] = (jnp.dot(a_ref[sub[r], :], w_ref[0:D_ATTN, :], preferred_element_type=F32)
                  + jnp.dot(r_ref[sub[r], :], w_ref[D_ATTN:, :], preferred_element_type=F32))

    def finish(r):
        o_ref[sub[r], :] = x_ref[sub[r], :] + _rms(mix[r], post_ref[...])

    _pipelined(nsub, None, project, finish)


def _outproj(x, attn, rec, w_out, post):
    t = x.shape[0]
    tm = min(ROW_TM, t)
    return pl.pallas_call(
        _outproj_kernel,
        grid=(t // tm,),
        in_specs=[
            pl.BlockSpec((tm, D_MODEL), lambda i: (i, 0)),
            pl.BlockSpec((tm, D_ATTN), lambda i: (i, 0)),
            pl.BlockSpec((tm, D_HG), lambda i: (i, 0)),
            pl.BlockSpec((D_MODEL, D_MODEL), lambda i: (0, 0), pipeline_mode=pl.Buffered(1)),
            pl.BlockSpec((1, D_MODEL), lambda i: (0, 0)),
        ],
        out_specs=pl.BlockSpec((tm, D_MODEL), lambda i: (i, 0)),
        out_shape=jax.ShapeDtypeStruct((t, D_MODEL), F32),
        compiler_params=_cparams(("parallel",)),
        name="mixer_out_proj",
    )(x, attn, rec, w_out, post)


def _ple_kernel(x_ref, p_ref, pre_ref, post_ref, wg_ref, wp_ref, o_ref):
    nsub = x_ref.shape[0] // SUB_ROWS
    sub = [slice(r * SUB_ROWS, (r + 1) * SUB_ROWS) for r in range(nsub)]
    hs, gates, projs = [None] * nsub, [None] * nsub, [None] * nsub

    def norm(r):
        hs[r] = _rms(x_ref[sub[r], :], pre_ref[...]).astype(BF16)

    def project(r):
        gates[r] = jnp.dot(hs[r], wg_ref[...], preferred_element_type=F32)
        projs[r] = jnp.dot(p_ref[sub[r], :].astype(BF16), wp_ref[...], preferred_element_type=F32)

    def finish(r):
        o_ref[sub[r], :] = x_ref[sub[r], :] + _rms(_sigmoid(gates[r]) * projs[r], post_ref[...])

    _pipelined(nsub, norm, project, finish)


def _ple(x, p, pre, post, w_gate, w_proj):
    t = x.shape[0]
    tm = min(ROW_TM, t)
    return pl.pallas_call(
        _ple_kernel,
        grid=(t // tm,),
        in_specs=[
            pl.BlockSpec((tm, D_MODEL), lambda i: (i, 0)),
            pl.BlockSpec((tm, PLE_DIM), lambda i: (i, 0)),
            pl.BlockSpec((1, D_MODEL), lambda i: (0, 0)),
            pl.BlockSpec((1, D_MODEL), lambda i: (0, 0)),
            pl.BlockSpec((D_MODEL, D_MODEL), lambda i: (0, 0), pipeline_mode=pl.Buffered(1)),
            pl.BlockSpec((PLE_DIM, D_MODEL), lambda i: (0, 0)),
        ],
        out_specs=pl.BlockSpec((tm, D_MODEL), lambda i: (i, 0)),
        out_shape=jax.ShapeDtypeStruct((t, D_MODEL), F32),
        compiler_params=_cparams(("parallel",)),
        name="ple_embed",
    )(x, p, pre, post, w_gate, w_proj)


def _t5_bucket(rel):
    half = NUM_BUCKETS // 2
    max_exact = half // 2
    n = jnp.abs(rel)
    nf = jnp.maximum(n, 1).astype(jnp.float32)
    large = max_exact + (jnp.log(nf / max_exact) / math.log(MAX_DISTANCE / max_exact)
                         * (half - max_exact)).astype(jnp.int32)
    large = jnp.minimum(large, half - 1)
    return jnp.where(rel > 0, half, 0) + jnp.where(n < max_exact, n, large)


def _ffn1(x, w):
    row = lambda v: v.reshape(1, -1)
    return _ffn(x, row(w['ffn1_pre']), row(w['ffn1_post']), w['ffn1_w_gate'], w['ffn1_w_up'], w['ffn1_w_down'])


def _back_half(x, p, attn, rec, w):
    row = lambda v: v.reshape(1, -1)
    x = _outproj(x, attn, rec, w['w_out'], row(w['mix_post']))
    x = _ffn(x, row(w['ffn2_pre']), row(w['ffn2_post']), w['ffn2_w_gate'], w['ffn2_w_up'], w['ffn2_w_down'])
    return _ple(x, p, row(w['ple_pre']), row(w['ple_post']), w['w_ple_gate'], w['w_ple_proj'])


def _prompt_layer(x, p, w, table, bucket, layer, *, n_seq, seq_len):
    x = _ffn1(x, w)
    attn, rec, kv, s_new = _mixer_front(
        x, w['mix_pre'].reshape(1, -1), w['w_in'], bucket, table, w['attn_sinks'], w['hgrn_lb_logits'],
        w['hgrn_norm'].reshape(1, -1), n_seq=n_seq, seq_len=seq_len, layer=layer)
    return _back_half(x, p, attn, rec, w), kv, s_new


def _sample_layer(x, p, w, table, bucket, layer, *, n_seq, s0, k_cache, v_cache):
    kv_w = N_KV * HEAD_DIM
    x = _ffn1(x, w)
    attn, rec, kv, s_new = _mixer_front_cached(
        x, w['mix_pre'].reshape(1, -1), w['w_in'], bucket, table, w['attn_sinks'], w['hgrn_lb_logits'],
        w['hgrn_norm'].reshape(1, -1), s0, k_cache.reshape(n_seq * WINDOW, kv_w),
        v_cache.reshape(n_seq * WINDOW, kv_w), n_seq=n_seq, layer=layer)
    return _back_half(x, p, attn, rec, w), kv, s_new


def kernel(x_prompt, x_sample, cache_attn_k, cache_attn_v, state_hgrn, p_prompt, p_sample,
           rel_bias_table, ffn1_pre, ffn1_post, ffn1_w_gate, ffn1_w_up, ffn1_w_down,
           mix_pre, mix_post, w_in, w_out, attn_sinks, hgrn_lb_logits, hgrn_norm,
           ffn2_pre, ffn2_post, ffn2_w_gate, ffn2_w_up, ffn2_w_down,
           ple_pre, ple_post, w_ple_gate, w_ple_proj):
    depth = w_in.shape[0]
    bp, sp, _ = x_prompt.shape
    bs, ss, _ = x_sample.shape
    wc = cache_attn_k.shape[2]
    assert wc == WINDOW and ss == CHUNK and sp % MIX_UNIT == 0 and sp >= WINDOW
    assert (bs * ss) % MIX_UNIT == 0

    rel = jnp.arange(LK)[None, :] - WINDOW - jnp.arange(CHUNK)[:, None]
    bucket = _t5_bucket(rel).astype(jnp.int32)

    kv_w = N_KV * HEAD_DIM
    k_lo, v_lo = D_ATTN, D_ATTN + kv_w
    yp = x_prompt.reshape(bp * sp, D_MODEL)
    ys = x_sample.reshape(bs * ss, D_MODEL)
    outs = [[] for _ in range(6)]
    for l in range(depth):
        w_in_l = w_in[l]
        w = {
            'ffn1_pre': ffn1_pre[l], 'ffn1_post': ffn1_post[l],
            'ffn1_w_gate': ffn1_w_gate[l].astype(BF16), 'ffn1_w_up': ffn1_w_up[l].astype(BF16),
            'ffn1_w_down': ffn1_w_down[l].astype(BF16),
            'mix_pre': mix_pre[l], 'mix_post': mix_post[l],
            'w_in': jnp.concatenate([w_in_l[:, :k_lo], w_in_l[:, v_lo + kv_w:], w_in_l[:, k_lo:v_lo + kv_w]],
                                    axis=1).astype(BF16),
            'w_out': w_out[l].astype(BF16), 'attn_sinks': attn_sinks[l],
            'hgrn_lb_logits': hgrn_lb_logits, 'hgrn_norm': hgrn_norm[l],
            'ffn2_pre': ffn2_pre[l], 'ffn2_post': ffn2_post[l],
            'ffn2_w_gate': ffn2_w_gate[l].astype(BF16), 'ffn2_w_up': ffn2_w_up[l].astype(BF16),
            'ffn2_w_down': ffn2_w_down[l].astype(BF16),
            'ple_pre': ple_pre[l], 'ple_post': ple_post[l],
            'w_ple_gate': w_ple_gate[l].astype(BF16), 'w_ple_proj': w_ple_proj[l].astype(BF16),
        }
        yp, kv_p, st_p = _prompt_layer(yp, p_prompt[l].reshape(bp * sp, PLE_DIM), w, rel_bias_table, bucket, l,
                                       n_seq=bp, seq_len=sp)
        ys, kv_s, st_s = _sample_layer(ys, p_sample[l].reshape(bs * ss, PLE_DIM), w, rel_bias_table, bucket, l,
                                       n_seq=bs, s0=state_hgrn[l], k_cache=cache_attn_k[l],
                                       v_cache=cache_attn_v[l])
        kv_p = kv_p.reshape(bp, sp, 2 * kv_w)[:, sp - WINDOW:].reshape(bp, WINDOW, 2, N_KV, HEAD_DIM)
        outs[0].append(kv_p[:, :, 0])
        outs[1].append(kv_p[:, :, 1])
        outs[2].append(st_p)
        kv_s = kv_s.reshape(bs, ss, 2, N_KV, HEAD_DIM)
        outs[3].append(jnp.concatenate([cache_attn_k[l], kv_s[:, :, 0]], axis=1)[:, ss:])
        outs[4].append(jnp.concatenate([cache_attn_v[l], kv_s[:, :, 1]], axis=1)[:, ss:])
        outs[5].append(st_s)
    return (yp.reshape(bp, sp, D_MODEL), ys.reshape(bs, ss, D_MODEL),
            jnp.stack(outs[0]), jnp.stack(outs[1]), jnp.stack(outs[2]),
            jnp.stack(outs[3]), jnp.stack(outs[4]), jnp.stack(outs[5]))
```

```python
import functools
import math

import jax
import jax.numpy as jnp
import numpy as np
from jax import lax
from jax.experimental import pallas as pl
from jax.experimental.pallas import tpu as pltpu

F32 = jnp.float32
BF16 = jnp.bfloat16

D_MODEL = 2048
CHUNK = 64
D_ATTN = 1024
HEAD_DIM = 128
N_HEADS = 8
N_KV = 2
GROUP = N_HEADS // N_KV
WINDOW = 128
LK = WINDOW + CHUNK
NUM_BUCKETS = 32
MAX_DISTANCE = 128
HG_DK = 128
HG_DV = 128
HG_HEADS = 8
D_HG = HG_HEADS * HG_DV
HG_SUB = 16
HG_SUB_SHIFT = HG_SUB.bit_length() - 1
CHUNK_SHIFT = CHUNK.bit_length() - 1
D_FF = 5632
PLE_DIM = 256
EPS = 1e-6
NEG_INF = -1e30
D_IN = D_ATTN + 2 * N_KV * HEAD_DIM + 4 * D_HG

Z_QH, Z_FH, Z_IH, Z_GH = 1, 2, 3, 4
Z_K, Z_V = 20, 21

VMEM_LIMIT_BYTES = 60 * 1024 * 1024

FFN_TM = 1024
FFN_TF = 512
ROW_TM = 1024
SUB_ROWS = 256
MIX_UNIT = 256
MIX_PROJ_TN = 512


def _rms(x, gain):
    y = x * lax.rsqrt(jnp.mean(x * x, axis=-1, keepdims=True) + EPS)
    return y * gain


def _sigmoid(x):
    return 1.0 / (1.0 + jnp.exp(-x))


def _cparams(sem):
    return pltpu.CompilerParams(dimension_semantics=sem, vmem_limit_bytes=VMEM_LIMIT_BYTES)


def _pipelined(n, before, matmul, after):
    if before is not None:
        before(0)
    for r in range(n):
        if before is not None and r + 1 < n:
            before(r + 1)
        matmul(r)
        if after is not None and r > 0:
            after(r - 1)
    if after is not None:
        after(n - 1)


def _ffn_kernel(x_ref, pre_ref, post_ref, wg_ref, wu_ref, wd_ref, o_ref, xn_ref):
    j = pl.program_id(1)
    last = pl.num_programs(1) - 1
    nsub = x_ref.shape[0] // SUB_ROWS
    sub = [slice(r * SUB_ROWS, (r + 1) * SUB_ROWS) for r in range(nsub)]

    def swiglu(rows):
        xn = xn_ref[rows, :]
        g = jnp.dot(xn, wg_ref[...], preferred_element_type=F32)
        u = jnp.dot(xn, wu_ref[...], preferred_element_type=F32)
        h = (g * _sigmoid(g) * u).astype(BF16)
        return jnp.dot(h, wd_ref[...], preferred_element_type=F32)

    @pl.when(j == 0)
    def _():
        def norm(r):
            xn_ref[sub[r], :] = _rms(x_ref[sub[r], :], pre_ref[...]).astype(BF16)

        def first(r):
            o_ref[sub[r], :] = swiglu(sub[r])

        _pipelined(nsub, norm, first, None)

    @pl.when((j > 0) & (j < last))
    def _():
        o_ref[...] += swiglu(slice(None))

    @pl.when(j == last)
    def _():
        ys = [None] * nsub

        def final(r):
            ys[r] = o_ref[sub[r], :] + swiglu(sub[r])

        def finish(r):
            o_ref[sub[r], :] = x_ref[sub[r], :] + 0.5 * _rms(ys[r], post_ref[...])

        _pipelined(nsub, None, final, finish)


def _ffn(x, pre, post, wg, wu, wd):
    t = x.shape[0]
    tm = min(FFN_TM, t)
    grid = (t // tm, D_FF // FFN_TF)
    assert grid[1] >= 3
    return pl.pallas_call(
        _ffn_kernel,
        grid=grid,
        in_specs=[
            pl.BlockSpec((tm, D_MODEL), lambda i, j: (i, 0)),
            pl.BlockSpec((1, D_MODEL), lambda i, j: (0, 0)),
            pl.BlockSpec((1, D_MODEL), lambda i, j: (0, 0)),
            pl.BlockSpec((D_MODEL, FFN_TF), lambda i, j: (0, j)),
            pl.BlockSpec((D_MODEL, FFN_TF), lambda i, j: (0, j)),
            pl.BlockSpec((FFN_TF, D_MODEL), lambda i, j: (j, 0)),
        ],
        out_specs=pl.BlockSpec((tm, D_MODEL), lambda i, j: (i, 0)),
        out_shape=jax.ShapeDtypeStruct((t, D_MODEL), F32),
        scratch_shapes=[pltpu.VMEM((tm, D_MODEL), BF16)],
        compiler_params=_cparams(("parallel", "arbitrary")),
        name="ffn_half_step",
    )(x, pre, post, wg, wu, wd)


def _fill_bias(bucket_ref, table_ref, bias_ref):
    bucket = bucket_ref[...]
    for h in range(N_HEADS):
        acc = jnp.zeros((CHUNK, LK), F32)
        for n in range(NUM_BUCKETS):
            acc = jnp.where(bucket == n, table_ref[n, h], acc)
        g, hh = divmod(h, GROUP)
        bias_ref[g, hh * CHUNK:(hh + 1) * CHUNK, :] = acc


def _sink_column(sinks_ref, g):
    row = lax.broadcasted_iota(jnp.int32, (GROUP * CHUNK, 1), 0)
    sink = jnp.zeros((GROUP * CHUNK, 1), F32)
    for hh in range(GROUP):
        sink = jnp.where((row >> CHUNK_SHIFT) == hh, sinks_ref[g * GROUP + hh], sink)
    return sink


def _attend_chunk(q_of_head, kk, vv, bias_g, sink, first_key_pos):
    qs = jnp.concatenate([q_of_head(hh) for hh in range(GROUP)], axis=0).astype(BF16)
    s = lax.dot_general(qs, kk, (((1,), (1,)), ((), ())), preferred_element_type=F32)
    s = s * (HEAD_DIM ** -0.5) + bias_g
    if first_key_pos is not None:
        col = lax.broadcasted_iota(jnp.int32, (GROUP * CHUNK, LK), 1)
        s = jnp.where(col + first_key_pos >= 0, s, NEG_INF)
    m = jnp.maximum(jnp.max(s, axis=-1, keepdims=True), sink)
    e = jnp.exp(s - m)
    p = e / (jnp.sum(e, axis=-1, keepdims=True) + jnp.exp(sink - m))
    return jnp.dot(p.astype(BF16), vv, preferred_element_type=F32)


def _attend_stages(q_ref, o_ref, k_all, v_all, bias_ref, sinks_ref, pos0, win_stride=CHUNK):
    def stage(c, g):
        rows = slice(c * CHUNK, (c + 1) * CHUNK)
        cols = slice(g * HEAD_DIM, (g + 1) * HEAD_DIM)
        win = slice(c * win_stride, c * win_stride + LK)
        masked = pos0 is not None and c * CHUNK < WINDOW

        def q_of_head(hh):
            h = g * GROUP + hh
            return q_ref[rows, h * HEAD_DIM:(h + 1) * HEAD_DIM]

        o = _attend_chunk(q_of_head, k_all[win, cols], v_all[win, cols], bias_ref[g], _sink_column(sinks_ref, g),
                          pos0 + (c * CHUNK - WINDOW) if masked else None)
        for hh in range(GROUP):
            h = g * GROUP + hh
            o_ref[rows, h * HEAD_DIM:(h + 1) * HEAD_DIM] = o[hh * CHUNK:(hh + 1) * CHUNK, :].astype(BF16)

    return [functools.partial(stage, c, g) for c in range(q_ref.shape[0] // CHUNK) for g in range(N_KV)]


def _interleave(a, b):
    ia = ib = 0
    while ia < len(a) or ib < len(b):
        if ib >= len(b) or (ia < len(a) and ia * len(b) <= ib * len(a)):
            a[ia]()
            ia += 1
        else:
            b[ib]()
            ib += 1


_HG_RANGES = [(0, 16), (0, 32), (0, 48), (0, 64), (16, 32), (32, 48), (48, 64), (16, 48), (16, 64), (32, 64)]
_HG_RANGE_ROWS = 16


def _hgrn_selector():
    t = np.arange(CHUNK)
    within = (t[:, None] // HG_SUB == t[None, :] // HG_SUB) & (t[None, :] <= t[:, None])
    ranges = np.zeros((_HG_RANGE_ROWS, CHUNK), bool)
    for n, (lo, hi) in enumerate(_HG_RANGES):
        ranges[n, lo:hi] = True
    once = np.concatenate([within, ranges], axis=0)
    return jnp.asarray(np.concatenate([once, once, once], axis=1), BF16)


def _lower_bound(lbl_ref, layer):
    logits = lbl_ref[...]
    ex = jnp.exp(logits - jnp.max(logits, axis=0, keepdims=True))
    return jnp.sum(ex[:layer + 1], axis=0, keepdims=True) / jnp.sum(ex, axis=0, keepdims=True)


def _hgrn_chunk(q, fl, iv, gate, sel, lb, gain, st_ref):
    nsub = CHUNK // HG_SUB
    s1, s2, s3, s4 = HG_SUB, 2 * HG_SUB, 3 * HG_SUB, 4 * HG_SUB
    t_i = lax.broadcasted_iota(jnp.int32, (CHUNK, nsub * CHUNK), 0)
    c_i = lax.broadcasted_iota(jnp.int32, (CHUNK, nsub * CHUNK), 1)
    keep = ((t_i >> HG_SUB_SHIFT) == (c_i >> CHUNK_SHIFT)) & ((c_i & (CHUNK - 1)) <= t_i)
    zeros_sub = jnp.zeros((HG_SUB, HG_DK), BF16)

    def sub(a, j):
        return a[j * HG_SUB:(j + 1) * HG_SUB]

    f = lb + (1.0 - lb) * _sigmoid(fl)
    lf = jnp.log(f)
    k = 1.0 - f
    hi = lf.astype(BF16)
    r1 = lf - hi.astype(F32)
    mid = r1.astype(BF16)
    lo = (r1 - mid.astype(F32)).astype(BF16)
    cr = jnp.dot(sel, jnp.concatenate([hi, mid, lo], axis=0), preferred_element_type=F32)
    cc = cr[:CHUNK]
    ev = jnp.exp(cr[CHUNK:])
    qd = q * jnp.exp(cc)
    ki = k * jnp.exp(-cc)
    iv = iv.astype(BF16)
    gate = gate * _sigmoid(gate)
    head_cols = [slice(h * HG_DK, (h + 1) * HG_DK) for h in range(HG_HEADS)]

    def dk(sl, lo_row, hi_row):
        n = _HG_RANGES.index((lo_row, hi_row))
        return ev[n:n + 1, sl]

    scores, inter, upds = [], [], []
    for h, sl in enumerate(head_cols):
        qd_h, ki_h = qd[:, sl], ki[:, sl]
        qd_b = qd_h.astype(BF16)
        ki_b = [sub(ki_h, j).astype(BF16) for j in range(nsub)]
        kend = [sub(ki_h, 0) * dk(sl, 0, s1), sub(ki_h, 1) * dk(sl, s1, s2),
                sub(ki_h, 2) * dk(sl, s2, s3), sub(ki_h, 3) * dk(sl, s3, s4)]
        kend_b = [kend[j].astype(BF16) for j in range(nsub - 1)]
        kmat = [
            [ki_b[0], zeros_sub, zeros_sub, zeros_sub],
            [kend_b[0], ki_b[1], zeros_sub, zeros_sub],
            [(kend[0] * dk(sl, s1, s2)).astype(BF16), kend_b[1], ki_b[2], zeros_sub],
            [(kend[0] * dk(sl, s1, s3)).astype(BF16), (kend[1] * dk(sl, s2, s3)).astype(BF16),
             kend_b[2], ki_b[3]],
        ]
        scores.append(lax.dot_general(qd_b, jnp.concatenate([p for row in kmat for p in row], axis=0),
                                      (((1,), (1,)), ((), ())), preferred_element_type=F32))
        qe = jnp.concatenate([sub(qd_h, 0), sub(qd_h, 1) * dk(sl, 0, s1), sub(qd_h, 2) * dk(sl, 0, s2),
                              sub(qd_h, 3) * dk(sl, 0, s3)], axis=0).astype(BF16)
        ke = jnp.concatenate([kend[0] * dk(sl, s1, s4), kend[1] * dk(sl, s2, s4),
                              kend[2] * dk(sl, s3, s4), kend[3]], axis=0).astype(BF16)
        inter.append(lax.dot_general(qe, st_ref[h].astype(BF16), (((1,), (1,)), ((), ())),
                                     preferred_element_type=F32))
        upds.append(lax.dot_general(iv[:, sl], ke, (((0,), (0,)), ((), ())),
                                    preferred_element_type=F32))
    outs = []
    for h, sl in enumerate(head_cols):
        a = jnp.where(keep, scores[h], 0.0).astype(BF16)
        outs.append(inter[h] + jnp.dot(a, jnp.concatenate([iv[:, sl]] * nsub, axis=0),
                                       preferred_element_type=F32))
        st_ref[h] = st_ref[h] * dk(sl, 0, s4) + upds[h]
    return [(_rms(outs[h], gain) * gate[:, sl]).astype(BF16) for h, sl in enumerate(head_cols)]


def _mixer_kernel(bucket_ref, table_ref, sinks_ref, sel_ref, lbl_ref, gain_ref, pre_ref, x_ref, w_ref,
                  attn_ref, rec_ref, kv_ref, sout_ref,
                  bias_ref, z_even, z_odd, xn_ref, k_all, v_all, st_ref, *, layer, units_per_seq):
    s = pl.program_id(0)
    unit = MIX_UNIT
    kv_w = N_KV * HEAD_DIM
    u_in_seq = (s + (units_per_seq - 1)) % units_per_seq

    @pl.when(s == 0)
    def _():
        _fill_bias(bucket_ref, table_ref, bias_ref)
        z_odd[...] = jnp.zeros(z_odd.shape, F32)
        k_all[0:WINDOW, :] = jnp.zeros((WINDOW, kv_w), BF16)
        v_all[0:WINDOW, :] = jnp.zeros((WINDOW, kv_w), BF16)

    @pl.when((s == 0) | (u_in_seq == 0))
    def _():
        st_ref[...] = jnp.zeros(st_ref.shape, F32)

    def step(z_write, z_read):
        xn_ref[...] = _rms(x_ref[...], pre_ref[...]).astype(BF16)

        def project(c):
            cols = slice(c * MIX_PROJ_TN, (c + 1) * MIX_PROJ_TN)
            z_write[:, cols] = jnp.dot(xn_ref[...], w_ref[:, cols], preferred_element_type=F32)

        k_cols = slice(Z_K * kv_w, (Z_K + 1) * kv_w)
        v_cols = slice(Z_V * kv_w, (Z_V + 1) * kv_w)
        lb = _lower_bound(lbl_ref, layer)

        def load_kv():
            k_all[WINDOW:, :] = z_read[:, k_cols].astype(BF16)
            v_all[WINDOW:, :] = z_read[:, v_cols].astype(BF16)
            kv_ref[:, 0:kv_w] = z_read[:, k_cols]
            kv_ref[:, kv_w:] = z_read[:, v_cols]

        def keep_history():
            k_all[0:WINDOW, :] = k_all[unit:unit + WINDOW, :]
            v_all[0:WINDOW, :] = v_all[unit:unit + WINDOW, :]

        def recur(c):
            rows = slice(c * CHUNK, (c + 1) * CHUNK)

            def zcols(col):
                return z_read[rows, col * D_HG:(col + 1) * D_HG]

            outs = _hgrn_chunk(zcols(Z_QH), zcols(Z_FH), zcols(Z_IH), zcols(Z_GH), sel_ref[...],
                               lb, gain_ref[...], st_ref)
            for h in range(HG_HEADS):
                rec_ref[rows, h * HG_DV:(h + 1) * HG_DV] = outs[h]

        mixers = ([load_kv]
                  + _attend_stages(z_read, attn_ref, k_all, v_all, bias_ref, sinks_ref, u_in_seq * unit)
                  + [keep_history]
                  + [functools.partial(recur, c) for c in range(unit // CHUNK)])
        _interleave([functools.partial(project, c) for c in range(D_IN // MIX_PROJ_TN)], mixers)

    @pl.when(s % 2 == 0)
    def _():
        step(z_even, z_odd)

    @pl.when(s % 2 == 1)
    def _():
        step(z_odd, z_even)

    @pl.when((s > 0) & (u_in_seq == units_per_seq - 1))
    def _():
        for h in range(HG_HEADS):
            sout_ref[0, h] = st_ref[h].T


def _mixer_front(x, pre, w_in, bucket, table, sinks, lb_logits, gain, *, n_seq, seq_len, layer):
    unit = MIX_UNIT
    ups = seq_len // unit
    n_units = n_seq * ups
    n_layers = lb_logits.shape[0]
    kv_w = N_KV * HEAD_DIM
    sel = _hgrn_selector()
    smem = pl.BlockSpec(memory_space=pltpu.SMEM)

    def const(shape):
        return pl.BlockSpec(shape, lambda s: (0,) * len(shape))

    def prev_unit(s):
        return (jnp.maximum(s - 1, 0), 0)

    return pl.pallas_call(
        functools.partial(_mixer_kernel, layer=layer, units_per_seq=ups),
        grid=(n_units + 1,),
        in_specs=[
            const((CHUNK, LK)), smem, smem, const(sel.shape), const((n_layers, D_HG)),
            const((1, HG_DV)), const((1, D_MODEL)),
            pl.BlockSpec((unit, D_MODEL), lambda s: (jnp.minimum(s, n_units - 1), 0)),
            pl.BlockSpec((D_MODEL, D_IN), lambda s: (0, 0), pipeline_mode=pl.Buffered(1)),
        ],
        out_specs=[
            pl.BlockSpec((unit, D_ATTN), prev_unit),
            pl.BlockSpec((unit, D_HG), prev_unit),
            pl.BlockSpec((unit, 2 * kv_w), prev_unit),
            pl.BlockSpec((1, HG_HEADS, HG_DK, HG_DV), lambda s: (jnp.maximum(s - 1, 0) // ups, 0, 0, 0)),
        ],
        out_shape=[
            jax.ShapeDtypeStruct((n_units * unit, D_ATTN), BF16),
            jax.ShapeDtypeStruct((n_units * unit, D_HG), BF16),
            jax.ShapeDtypeStruct((n_units * unit, 2 * kv_w), F32),
            jax.ShapeDtypeStruct((n_seq, HG_HEADS, HG_DK, HG_DV), F32),
        ],
        scratch_shapes=[
            pltpu.VMEM((N_KV, GROUP * CHUNK, LK), F32),
            pltpu.VMEM((unit, D_IN), F32),
            pltpu.VMEM((unit, D_IN), F32),
            pltpu.VMEM((unit, D_MODEL), BF16),
            pltpu.VMEM((WINDOW + unit, kv_w), BF16),
            pltpu.VMEM((WINDOW + unit, kv_w), BF16),
            pltpu.VMEM((HG_HEADS, HG_DV, HG_DK), F32),
        ],
        compiler_params=_cparams(("arbitrary",)),
        name="mixer_front",
    )(bucket, table, sinks, sel, lb_logits, gain, pre, x, w_in)


def _mixer_cached_kernel(bucket_ref, table_ref, sinks_ref, sel_ref, lbl_ref, gain_ref, pre_ref, x_ref, w_ref,
                         kp_ref, vp_ref, s0_ref,
                         attn_ref, rec_ref, kv_ref, sout_ref,
                         bias_ref, z_even, z_odd, xn_ref, k_win, v_win, st_ref, *, layer):
    s = pl.program_id(0)
    unit = MIX_UNIT
    kv_w = N_KV * HEAD_DIM

    @pl.when(s == 0)
    def _():
        _fill_bias(bucket_ref, table_ref, bias_ref)
        z_odd[...] = jnp.zeros(z_odd.shape, F32)

    def step(z_write, z_read):
        xn_ref[...] = _rms(x_ref[...], pre_ref[...]).astype(BF16)

        def project(c):
            cols = slice(c * MIX_PROJ_TN, (c + 1) * MIX_PROJ_TN)
            z_write[:, cols] = jnp.dot(xn_ref[...], w_ref[:, cols], preferred_element_type=F32)

        k_cols = slice(Z_K * kv_w, (Z_K + 1) * kv_w)
        v_cols = slice(Z_V * kv_w, (Z_V + 1) * kv_w)
        lb = _lower_bound(lbl_ref, layer)

        def load_kv():
            for c in range(unit // CHUNK):
                hist = slice(c * WINDOW, (c + 1) * WINDOW)
                rows = slice(c * CHUNK, (c + 1) * CHUNK)
                k_win[c * LK:c * LK + WINDOW, :] = kp_ref[hist, :].astype(BF16)
                v_win[c * LK:c * LK + WINDOW, :] = vp_ref[hist, :].astype(BF16)
                k_win[c * LK + WINDOW:(c + 1) * LK, :] = z_read[rows, k_cols].astype(BF16)
                v_win[c * LK + WINDOW:(c + 1) * LK, :] = z_read[rows, v_cols].astype(BF16)
            kv_ref[:, 0:kv_w] = z_read[:, k_cols]
            kv_ref[:, kv_w:] = z_read[:, v_cols]

        def recur(c):
            rows = slice(c * CHUNK, (c + 1) * CHUNK)
            for h in range(HG_HEADS):
                st_ref[h] = s0_ref[c, h].T

            def zcols(col):
                return z_read[rows, col * D_HG:(col + 1) * D_HG]

            outs = _hgrn_chunk(zcols(Z_QH), zcols(Z_FH), zcols(Z_IH), zcols(Z_GH), sel_ref[...],
                               lb, gain_ref[...], st_ref)
            for h in range(HG_HEADS):
                rec_ref[rows, h * HG_DV:(h + 1) * HG_DV] = outs[h]
                sout_ref[c, h] = st_ref[h].T

        mixers = ([load_kv]
                  + _attend_stages(z_read, attn_ref, k_win, v_win, bias_ref, sinks_ref, None, win_stride=LK)
                  + [functools.partial(recur, c) for c in range(unit // CHUNK)])
        _interleave([functools.partial(project, c) for c in range(D_IN // MIX_PROJ_TN)], mixers)

    @pl.when(s % 2 == 0)
    def _():
        step(z_even, z_odd)

    @pl.when(s % 2 == 1)
    def _():
        step(z_odd, z_even)


def _mixer_front_cached(x, pre, w_in, bucket, table, sinks, lb_logits, gain, s0, k_cache, v_cache, *,
                        n_seq, layer):
    unit = MIX_UNIT
    per_unit = unit // CHUNK
    n_units = n_seq // per_unit
    n_layers = lb_logits.shape[0]
    kv_w = N_KV * HEAD_DIM
    sel = _hgrn_selector()
    smem = pl.BlockSpec(memory_space=pltpu.SMEM)

    def const(shape):
        return pl.BlockSpec(shape, lambda s: (0,) * len(shape))

    def prev_unit(s):
        return (jnp.maximum(s - 1, 0), 0)

    state_spec = pl.BlockSpec((per_unit, HG_HEADS, HG_DK, HG_DV), lambda s: (jnp.maximum(s - 1, 0), 0, 0, 0))
    return pl.pallas_call(
        functools.partial(_mixer_cached_kernel, layer=layer),
        grid=(n_units + 1,),
        in_specs=[
            const((CHUNK, LK)), smem, smem, const(sel.shape), const((n_layers, D_HG)),
            const((1, HG_DV)), const((1, D_MODEL)),
            pl.BlockSpec((unit, D_MODEL), lambda s: (jnp.minimum(s, n_units - 1), 0)),
            pl.BlockSpec((D_MODEL, D_IN), lambda s: (0, 0), pipeline_mode=pl.Buffered(1)),
            pl.BlockSpec((per_unit * WINDOW, kv_w), prev_unit),
            pl.BlockSpec((per_unit * WINDOW, kv_w), prev_unit),
            state_spec,
        ],
        out_specs=[
            pl.BlockSpec((unit, D_ATTN), prev_unit),
            pl.BlockSpec((unit, D_HG), prev_unit),
            pl.BlockSpec((unit, 2 * kv_w), prev_unit),
            state_spec,
        ],
        out_shape=[
            jax.ShapeDtypeStruct((n_units * unit, D_ATTN), BF16),
            jax.ShapeDtypeStruct((n_units * unit, D_HG), BF16),
            jax.ShapeDtypeStruct((n_units * unit, 2 * kv_w), F32),
            jax.ShapeDtypeStruct((n_seq, HG_HEADS, HG_DK, HG_DV), F32),
        ],
        scratch_shapes=[
            pltpu.VMEM((N_KV, GROUP * CHUNK, LK), F32),
            pltpu.VMEM((unit, D_IN), F32),
            pltpu.VMEM((unit, D_IN), F32),
            pltpu.VMEM((unit, D_MODEL), BF16),
            pltpu.VMEM((per_unit * LK, kv_w), BF16),
            pltpu.VMEM((per_unit * LK, kv_w), BF16),
            pltpu.VMEM((HG_HEADS, HG_DV, HG_DK), F32),
        ],
        compiler_params=_cparams(("arbitrary",)),
        name="mixer_front_cached",
    )(bucket, table, sinks, sel, lb_logits, gain, pre, x, w_in, k_cache, v_cache, s0)


def _outproj_kernel(x_ref, a_ref, r_ref, w_ref, post_ref, o_ref):
    nsub = x_ref.shape[0] // SUB_ROWS
    sub = [slice(r * SUB_ROWS, (r + 1) * SUB_ROWS) for r in range(nsub)]
    mix = [None] * nsub

    def project(r):
        mix[r] = (jnp.dot(a_ref[sub[r], :], w_ref[0:D_ATTN, :], preferred_element_type=F32)
                  + jnp.dot(r_ref[sub[r], :], w_ref[D_ATTN:, :], preferred_element_type=F32))

    def finish(r):
        o_ref[sub[r], :] = x_ref[sub[r], :] + _rms(mix[r], post_ref[...])

    _pipelined(nsub, None, project, finish)


def _outproj(x, attn, rec, w_out, post):
    t = x.shape[0]
    tm = min(ROW_TM, t)
    return pl.pallas_call(
        _outproj_kernel,
        grid=(t // tm,),
        in_specs=[
            pl.BlockSpec((tm, D_MODEL), lambda i: (i, 0)),
            pl.BlockSpec((tm, D_ATTN), lambda i: (i, 0)),
            pl.BlockSpec((tm, D_HG), lambda i: (i, 0)),
            pl.BlockSpec((D_MODEL, D_MODEL), lambda i: (0, 0), pipeline_mode=pl.Buffered(1)),
            pl.BlockSpec((1, D_MODEL), lambda i: (0, 0)),
        ],
        out_specs=pl.BlockSpec((tm, D_MODEL), lambda i: (i, 0)),
        out_shape=jax.ShapeDtypeStruct((t, D_MODEL), F32),
        compiler_params=_cparams(("parallel",)),
        name="mixer_out_proj",
    )(x, attn, rec, w_out, post)


def _ple_kernel(x_ref, p_ref, pre_ref, post_ref, wg_ref, wp_ref, o_ref):
    nsub = x_ref.shape[0] // SUB_ROWS
    sub = [slice(r * SUB_ROWS, (r + 1) * SUB_ROWS) for r in range(nsub)]
    hs, gates, projs = [None] * nsub, [None] * nsub, [None] * nsub

    def norm(r):
        hs[r] = _rms(x_ref[sub[r], :], pre_ref[...]).astype(BF16)

    def project(r):
        gates[r] = jnp.dot(hs[r], wg_ref[...], preferred_element_type=F32)
        projs[r] = jnp.dot(p_ref[sub[r], :].astype(BF16), wp_ref[...], preferred_element_type=F32)

    def finish(r):
        o_ref[sub[r], :] = x_ref[sub[r], :] + _rms(_sigmoid(gates[r]) * projs[r], post_ref[...])

    _pipelined(nsub, norm, project, finish)


def _ple(x, p_layers, layer, pre, post, w_gate, w_proj):
    t = x.shape[0]
    tm = min(ROW_TM, t)
    return pl.pallas_call(
        _ple_kernel,
        grid=(t // tm,),
        in_specs=[
            pl.BlockSpec((tm, D_MODEL), lambda i: (i, 0)),
            pl.BlockSpec((None, tm, PLE_DIM), lambda i: (layer, i, 0)),
            pl.BlockSpec((1, D_MODEL), lambda i: (0, 0)),
            pl.BlockSpec((1, D_MODEL), lambda i: (0, 0)),
            pl.BlockSpec((D_MODEL, D_MODEL), lambda i: (0, 0), pipeline_mode=pl.Buffered(1)),
            pl.BlockSpec((PLE_DIM, D_MODEL), lambda i: (0, 0)),
        ],
        out_specs=pl.BlockSpec((tm, D_MODEL), lambda i: (i, 0)),
        out_shape=jax.ShapeDtypeStruct((t, D_MODEL), F32),
        compiler_params=_cparams(("parallel",)),
        name="ple_embed",
    )(x, p_layers, pre, post, w_gate, w_proj)


def _t5_bucket(rel):
    half = NUM_BUCKETS // 2
    max_exact = half // 2
    n = jnp.abs(rel)
    nf = jnp.maximum(n, 1).astype(jnp.float32)
    large = max_exact + (jnp.log(nf / max_exact) / math.log(MAX_DISTANCE / max_exact)
                         * (half - max_exact)).astype(jnp.int32)
    large = jnp.minimum(large, half - 1)
    return jnp.where(rel > 0, half, 0) + jnp.where(n < max_exact, n, large)


def _ffn1(x, w):
    row = lambda v: v.reshape(1, -1)
    return _ffn(x, row(w['ffn1_pre']), row(w['ffn1_post']), w['ffn1_w_gate'], w['ffn1_w_up'], w['ffn1_w_down'])


def _back_half(x, p, layer, attn, rec, w):
    row = lambda v: v.reshape(1, -1)
    x = _outproj(x, attn, rec, w['w_out'], row(w['mix_post']))
    x = _ffn(x, row(w['ffn2_pre']), row(w['ffn2_post']), w['ffn2_w_gate'], w['ffn2_w_up'], w['ffn2_w_down'])
    return _ple(x, p, layer, row(w['ple_pre']), row(w['ple_post']), w['w_ple_gate'], w['w_ple_proj'])


def _prompt_layer(x, p, w, table, bucket, layer, *, n_seq, seq_len):
    x = _ffn1(x, w)
    attn, rec, kv, s_new = _mixer_front(
        x, w['mix_pre'].reshape(1, -1), w['w_in'], bucket, table, w['attn_sinks'], w['hgrn_lb_logits'],
        w['hgrn_norm'].reshape(1, -1), n_seq=n_seq, seq_len=seq_len, layer=layer)
    return _back_half(x, p, layer, attn, rec, w), kv, s_new


def _sample_layer(x, p, w, table, bucket, layer, *, n_seq, s0, k_cache, v_cache):
    kv_w = N_KV * HEAD_DIM
    x = _ffn1(x, w)
    attn, rec, kv, s_new = _mixer_front_cached(
        x, w['mix_pre'].reshape(1, -1), w['w_in'], bucket, table, w['attn_sinks'], w['hgrn_lb_logits'],
        w['hgrn_norm'].reshape(1, -1), s0, k_cache.reshape(n_seq * WINDOW, kv_w),
        v_cache.reshape(n_seq * WINDOW, kv_w), n_seq=n_seq, layer=layer)
    return _back_half(x, p, layer, attn, rec, w), kv, s_new


def kernel(x_prompt, x_sample, cache_attn_k, cache_attn_v, state_hgrn, p_prompt, p_sample,
           rel_bias_table, ffn1_pre, ffn1_post, ffn1_w_gate, ffn1_w_up, ffn1_w_down,
           mix_pre, mix_post, w_in, w_out, attn_sinks, hgrn_lb_logits, hgrn_norm,
           ffn2_pre, ffn2_post, ffn2_w_gate, ffn2_w_up, ffn2_w_down,
           ple_pre, ple_post, w_ple_gate, w_ple_proj):
    depth = w_in.shape[0]
    bp, sp, _ = x_prompt.shape
    bs, ss, _ = x_sample.shape
    wc = cache_attn_k.shape[2]
    assert wc == WINDOW and ss == CHUNK and sp % MIX_UNIT == 0 and sp >= WINDOW
    assert (bs * ss) % MIX_UNIT == 0

    rel = jnp.arange(LK)[None, :] - WINDOW - jnp.arange(CHUNK)[:, None]
    bucket = _t5_bucket(rel).astype(jnp.int32)

    kv_w = N_KV * HEAD_DIM
    k_lo, v_lo = D_ATTN, D_ATTN + kv_w
    yp = x_prompt.reshape(bp * sp, D_MODEL)
    ys = x_sample.reshape(bs * ss, D_MODEL)
    outs = [[] for _ in range(6)]
    for l in range(depth):
        w_in_l = w_in[l]
        w = {
            'ffn1_pre': ffn1_pre[l], 'ffn1_post': ffn1_post[l],
            'ffn1_w_gate': ffn1_w_gate[l].astype(BF16), 'ffn1_w_up': ffn1_w_up[l].astype(BF16),
            'ffn1_w_down': ffn1_w_down[l].astype(BF16),
            'mix_pre': mix_pre[l], 'mix_post': mix_post[l],
            'w_in': jnp.concatenate([w_in_l[:, :k_lo], w_in_l[:, v_lo + kv_w:], w_in_l[:, k_lo:v_lo + kv_w]],
                                    axis=1).astype(BF16),
            'w_out': w_out[l].astype(BF16), 'attn_sinks': attn_sinks[l],
            'hgrn_lb_logits': hgrn_lb_logits, 'hgrn_norm': hgrn_norm[l],
            'ffn2_pre': ffn2_pre[l], 'ffn2_post': ffn2_post[l],
            'ffn2_w_gate': ffn2_w_gate[l].astype(BF16), 'ffn2_w_up': ffn2_w_up[l].astype(BF16),
            'ffn2_w_down': ffn2_w_down[l].astype(BF16),
            'ple_pre': ple_pre[l], 'ple_post': ple_post[l],
            'w_ple_gate': w_ple_gate[l].astype(BF16), 'w_ple_proj': w_ple_proj[l].astype(BF16),
        }
        yp, kv_p, st_p = _prompt_layer(yp, p_prompt.reshape(depth, bp * sp, PLE_DIM), w, rel_bias_table, bucket, l,
                                       n_seq=bp, seq_len=sp)
        ys, kv_s, st_s = _sample_layer(ys, p_sample.reshape(depth, bs * ss, PLE_DIM), w, rel_bias_table, bucket, l,
                                       n_seq=bs, s0=state_hgrn[l], k_cache=cache_attn_k[l],
                                       v_cache=cache_attn_v[l])
        kv_p = kv_p.reshape(bp, sp, 2 * kv_w)[:, sp - WINDOW:].reshape(bp, WINDOW, 2, N_KV, HEAD_DIM)
        outs[0].append(kv_p[:, :, 0])
        outs[1].append(kv_p[:, :, 1])
        outs[2].append(st_p)
        kv_s = kv_s.reshape(bs, ss, 2, N_KV, HEAD_DIM)
        outs[3].append(jnp.concatenate([cache_attn_k[l], kv_s[:, :, 0]], axis=1)[:, ss:])
        outs[4].append(jnp.concatenate([cache_attn_v[l], kv_s[:, :, 1]], axis=1)[:, ss:])
        outs[5].append(st_s)
    return (yp.reshape(bp, sp, D_MODEL), ys.reshape(bs, ss, D_MODEL),
            jnp.stack(outs[0]), jnp.stack(outs[1]), jnp.stack(outs[2]),
            jnp.stack(outs[3]), jnp.stack(outs[4]), jnp.stack(outs[5]))
```

```python
import functools
import math

import jax
import jax.numpy as jnp
import numpy as np
from jax import lax
from jax.experimental import pallas as pl
from jax.experimental.pallas import tpu as pltpu

F32 = jnp.float32
BF16 = jnp.bfloat16

D_MODEL = 2048
CHUNK = 64
D_ATTN = 1024
HEAD_DIM = 128
N_HEADS = 8
N_KV = 2
GROUP = N_HEADS // N_KV
WINDOW = 128
LK = WINDOW + CHUNK
NUM_BUCKETS = 32
MAX_DISTANCE = 128
HG_DK = 128
HG_DV = 128
HG_HEADS = 8
D_HG = HG_HEADS * HG_DV
HG_SUB = 16
HG_SUB_SHIFT = HG_SUB.bit_length() - 1
CHUNK_SHIFT = CHUNK.bit_length() - 1
D_FF = 5632
PLE_DIM = 256
EPS = 1e-6
NEG_INF = -1e30
D_IN = D_ATTN + 2 * N_KV * HEAD_DIM + 4 * D_HG

Z_QH, Z_FH, Z_IH, Z_GH = 1, 2, 3, 4
Z_K, Z_V = 20, 21

VMEM_LIMIT_BYTES = 60 * 1024 * 1024

FFN_TM = 1024
FFN_TF = 512
ROW_TM = 1024
SUB_ROWS = 256
MIX_UNIT = 256
MIX_PROJ_TN = 512


def _rms(x, gain):
    y = x * lax.rsqrt(jnp.mean(x * x, axis=-1, keepdims=True) + EPS)
    return y * gain


def _sigmoid(x):
    return 1.0 / (1.0 + jnp.exp(-x))


def _cparams(sem):
    return pltpu.CompilerParams(dimension_semantics=sem, vmem_limit_bytes=VMEM_LIMIT_BYTES)


def _pipelined(n, before, matmul, after):
    if before is not None:
        before(0)
    for r in range(n):
        if before is not None and r + 1 < n:
            before(r + 1)
        matmul(r)
        if after is not None and r > 0:
            after(r - 1)
    if after is not None:
        after(n - 1)


def _ffn_kernel(x_ref, pre_ref, post_ref, wg_ref, wu_ref, wd_ref, o_ref, xn_ref):
    j = pl.program_id(1)
    last = pl.num_programs(1) - 1
    nsub = x_ref.shape[0] // SUB_ROWS
    sub = [slice(r * SUB_ROWS, (r + 1) * SUB_ROWS) for r in range(nsub)]

    def swiglu(rows):
        xn = xn_ref[rows, :]
        g = jnp.dot(xn, wg_ref[...], preferred_element_type=F32)
        u = jnp.dot(xn, wu_ref[...], preferred_element_type=F32)
        h = (g * _sigmoid(g) * u).astype(BF16)
        return jnp.dot(h, wd_ref[...], preferred_element_type=F32)

    @pl.when(j == 0)
    def _():
        def norm(r):
            xn_ref[sub[r], :] = _rms(x_ref[sub[r], :], pre_ref[...]).astype(BF16)

        def first(r):
            o_ref[sub[r], :] = swiglu(sub[r])

        _pipelined(nsub, norm, first, None)

    @pl.when((j > 0) & (j < last))
    def _():
        o_ref[...] += swiglu(slice(None))

    @pl.when(j == last)
    def _():
        ys = [None] * nsub

        def final(r):
            ys[r] = o_ref[sub[r], :] + swiglu(sub[r])

        def finish(r):
            o_ref[sub[r], :] = x_ref[sub[r], :] + 0.5 * _rms(ys[r], post_ref[...])

        _pipelined(nsub, None, final, finish)


def _ffn(x, pre, post, wg, wu, wd):
    t = x.shape[0]
    tm = min(FFN_TM, t)
    grid = (t // tm, D_FF // FFN_TF)
    assert grid[1] >= 3
    return pl.pallas_call(
        _ffn_kernel,
        grid=grid,
        in_specs=[
            pl.BlockSpec((tm, D_MODEL), lambda i, j: (i, 0)),
            pl.BlockSpec((1, D_MODEL), lambda i, j: (0, 0)),
            pl.BlockSpec((1, D_MODEL), lambda i, j: (0, 0)),
            pl.BlockSpec((D_MODEL, FFN_TF), lambda i, j: (0, j)),
            pl.BlockSpec((D_MODEL, FFN_TF), lambda i, j: (0, j)),
            pl.BlockSpec((FFN_TF, D_MODEL), lambda i, j: (j, 0)),
        ],
        out_specs=pl.BlockSpec((tm, D_MODEL), lambda i, j: (i, 0)),
        out_shape=jax.ShapeDtypeStruct((t, D_MODEL), F32),
        scratch_shapes=[pltpu.VMEM((tm, D_MODEL), BF16)],
        compiler_params=_cparams(("parallel", "arbitrary")),
        name="ffn_half_step",
    )(x, pre, post, wg, wu, wd)


def _fill_bias(bucket_ref, table_ref, bias_ref):
    bucket = bucket_ref[...]
    for h in range(N_HEADS):
        acc = jnp.zeros((CHUNK, LK), F32)
        for n in range(NUM_BUCKETS):
            acc = jnp.where(bucket == n, table_ref[n, h], acc)
        g, hh = divmod(h, GROUP)
        bias_ref[g, hh * CHUNK:(hh + 1) * CHUNK, :] = acc


def _sink_column(sinks_ref, g):
    row = lax.broadcasted_iota(jnp.int32, (GROUP * CHUNK, 1), 0)
    sink = jnp.zeros((GROUP * CHUNK, 1), F32)
    for hh in range(GROUP):
        sink = jnp.where((row >> CHUNK_SHIFT) == hh, sinks_ref[g * GROUP + hh], sink)
    return sink


def _attend_chunk(q_of_head, kk, vv, bias_g, sink, first_key_pos):
    qs = jnp.concatenate([q_of_head(hh) for hh in range(GROUP)], axis=0).astype(BF16)
    s = lax.dot_general(qs, kk, (((1,), (1,)), ((), ())), preferred_element_type=F32)
    s = s * (HEAD_DIM ** -0.5) + bias_g
    if first_key_pos is not None:
        col = lax.broadcasted_iota(jnp.int32, (GROUP * CHUNK, LK), 1)
        s = jnp.where(col + first_key_pos >= 0, s, NEG_INF)
    m = jnp.maximum(jnp.max(s, axis=-1, keepdims=True), sink)
    e = jnp.exp(s - m)
    p = e / (jnp.sum(e, axis=-1, keepdims=True) + jnp.exp(sink - m))
    return jnp.dot(p.astype(BF16), vv, preferred_element_type=F32)


def _attend_stages(q_ref, o_ref, k_all, v_all, bias_ref, sinks_ref, pos0, win_stride=CHUNK):
    def stage(c, g):
        rows = slice(c * CHUNK, (c + 1) * CHUNK)
        cols = slice(g * HEAD_DIM, (g + 1) * HEAD_DIM)
        win = slice(c * win_stride, c * win_stride + LK)
        masked = pos0 is not None and c * CHUNK < WINDOW

        def q_of_head(hh):
            h = g * GROUP + hh
            return q_ref[rows, h * HEAD_DIM:(h + 1) * HEAD_DIM]

        o = _attend_chunk(q_of_head, k_all[win, cols], v_all[win, cols], bias_ref[g], _sink_column(sinks_ref, g),
                          pos0 + (c * CHUNK - WINDOW) if masked else None)
        for hh in range(GROUP):
            h = g * GROUP + hh
            o_ref[rows, h * HEAD_DIM:(h + 1) * HEAD_DIM] = o[hh * CHUNK:(hh + 1) * CHUNK, :].astype(BF16)

    return [functools.partial(stage, c, g) for c in range(q_ref.shape[0] // CHUNK) for g in range(N_KV)]


def _interleave(a, b):
    ia = ib = 0
    while ia < len(a) or ib < len(b):
        if ib >= len(b) or (ia < len(a) and ia * len(b) <= ib * len(a)):
            a[ia]()
            ia += 1
        else:
            b[ib]()
            ib += 1


_HG_RANGES = [(0, 16), (0, 32), (0, 48), (0, 64), (16, 32), (32, 48), (48, 64), (16, 48), (16, 64), (32, 64)]
_HG_RANGE_ROWS = 16


def _hgrn_selector():
    t = np.arange(CHUNK)
    within = (t[:, None] // HG_SUB == t[None, :] // HG_SUB) & (t[None, :] <= t[:, None])
    ranges = np.zeros((_HG_RANGE_ROWS, CHUNK), bool)
    for n, (lo, hi) in enumerate(_HG_RANGES):
        ranges[n, lo:hi] = True
    once = np.concatenate([within, ranges], axis=0)
    return jnp.asarray(np.concatenate([once, once, once], axis=1), BF16)


def _lower_bound(lbl_ref, layer):
    logits = lbl_ref[...]
    ex = jnp.exp(logits - jnp.max(logits, axis=0, keepdims=True))
    return jnp.sum(ex[:layer + 1], axis=0, keepdims=True) / jnp.sum(ex, axis=0, keepdims=True)


def _hgrn_chunk(q, fl, iv, gate, sel, lb, gain, st_ref):
    nsub = CHUNK // HG_SUB
    s1, s2, s3, s4 = HG_SUB, 2 * HG_SUB, 3 * HG_SUB, 4 * HG_SUB
    t_i = lax.broadcasted_iota(jnp.int32, (CHUNK, nsub * CHUNK), 0)
    c_i = lax.broadcasted_iota(jnp.int32, (CHUNK, nsub * CHUNK), 1)
    keep = ((t_i >> HG_SUB_SHIFT) == (c_i >> CHUNK_SHIFT)) & ((c_i & (CHUNK - 1)) <= t_i)
    zeros_sub = jnp.zeros((HG_SUB, HG_DK), BF16)

    def sub(a, j):
        return a[j * HG_SUB:(j + 1) * HG_SUB]

    f = lb + (1.0 - lb) * _sigmoid(fl)
    lf = jnp.log(f)
    k = 1.0 - f
    hi = lf.astype(BF16)
    r1 = lf - hi.astype(F32)
    mid = r1.astype(BF16)
    lo = (r1 - mid.astype(F32)).astype(BF16)
    cr = jnp.dot(sel, jnp.concatenate([hi, mid, lo], axis=0), preferred_element_type=F32)
    cc = cr[:CHUNK]
    ev = jnp.exp(cr[CHUNK:])
    qd = q * jnp.exp(cc)
    ki = k * jnp.exp(-cc)
    iv = iv.astype(BF16)
    gate = gate * _sigmoid(gate)
    head_cols = [slice(h * HG_DK, (h + 1) * HG_DK) for h in range(HG_HEADS)]

    def dk(sl, lo_row, hi_row):
        n = _HG_RANGES.index((lo_row, hi_row))
        return ev[n:n + 1, sl]

    scores, inter, upds = [], [], []
    for h, sl in enumerate(head_cols):
        qd_h, ki_h = qd[:, sl], ki[:, sl]
        qd_b = qd_h.astype(BF16)
        ki_b = [sub(ki_h, j).astype(BF16) for j in range(nsub)]
        kend = [sub(ki_h, 0) * dk(sl, 0, s1), sub(ki_h, 1) * dk(sl, s1, s2),
                sub(ki_h, 2) * dk(sl, s2, s3), sub(ki_h, 3) * dk(sl, s3, s4)]
        kend_b = [kend[j].astype(BF16) for j in range(nsub - 1)]
        kmat = [
            [ki_b[0], zeros_sub, zeros_sub, zeros_sub],
            [kend_b[0], ki_b[1], zeros_sub, zeros_sub],
            [(kend[0] * dk(sl, s1, s2)).astype(BF16), kend_b[1], ki_b[2], zeros_sub],
            [(kend[0] * dk(sl, s1, s3)).astype(BF16), (kend[1] * dk(sl, s2, s3)).astype(BF16),
             kend_b[2], ki_b[3]],
        ]
        scores.append(lax.dot_general(qd_b, jnp.concatenate([p for row in kmat for p in row], axis=0),
                                      (((1,), (1,)), ((), ())), preferred_element_type=F32))
        qe = jnp.concatenate([sub(qd_h, 0), sub(qd_h, 1) * dk(sl, 0, s1), sub(qd_h, 2) * dk(sl, 0, s2),
                              sub(qd_h, 3) * dk(sl, 0, s3)], axis=0).astype(BF16)
        ke = jnp.concatenate([kend[0] * dk(sl, s1, s4), kend[1] * dk(sl, s2, s4),
                              kend[2] * dk(sl, s3, s4), kend[3]], axis=0).astype(BF16)
        inter.append(lax.dot_general(qe, st_ref[h].astype(BF16), (((1,), (1,)), ((), ())),
                                     preferred_element_type=F32))
        upds.append(lax.dot_general(iv[:, sl], ke, (((0,), (0,)), ((), ())),
                                    preferred_element_type=F32))
    outs = []
    for h, sl in enumerate(head_cols):
        a = jnp.where(keep, scores[h], 0.0).astype(BF16)
        outs.append(inter[h] + jnp.dot(a, jnp.concatenate([iv[:, sl]] * nsub, axis=0),
                                       preferred_element_type=F32))
        st_ref[h] = st_ref[h] * dk(sl, 0, s4) + upds[h]
    return [(_rms(outs[h], gain) * gate[:, sl]).astype(BF16) for h, sl in enumerate(head_cols)]


def _mixer_kernel(bucket_ref, table_ref, sinks_ref, sel_ref, lbl_ref, gain_ref, pre_ref, x_ref, w_ref,
                  attn_ref, rec_ref, kv_ref, sout_ref,
                  bias_ref, z_even, z_odd, xn_ref, k_all, v_all, st_ref, *, layer, units_per_seq):
    s = pl.program_id(0)
    unit = MIX_UNIT
    kv_w = N_KV * HEAD_DIM
    u_in_seq = (s + (units_per_seq - 1)) % units_per_seq

    @pl.when(s == 0)
    def _():
        _fill_bias(bucket_ref, table_ref, bias_ref)
        z_odd[...] = jnp.zeros(z_odd.shape, F32)
        k_all[0:WINDOW, :] = jnp.zeros((WINDOW, kv_w), BF16)
        v_all[0:WINDOW, :] = jnp.zeros((WINDOW, kv_w), BF16)

    @pl.when((s == 0) | (u_in_seq == 0))
    def _():
        st_ref[...] = jnp.zeros(st_ref.shape, F32)

    def step(z_write, z_read):
        xn_ref[...] = _rms(x_ref[...], pre_ref[...]).astype(BF16)

        def project(c):
            cols = slice(c * MIX_PROJ_TN, (c + 1) * MIX_PROJ_TN)
            z_write[:, cols] = jnp.dot(xn_ref[...], w_ref[:, cols], preferred_element_type=F32)

        k_cols = slice(Z_K * kv_w, (Z_K + 1) * kv_w)
        v_cols = slice(Z_V * kv_w, (Z_V + 1) * kv_w)
        lb = _lower_bound(lbl_ref, layer)

        def load_kv():
            k_all[WINDOW:, :] = z_read[:, k_cols].astype(BF16)
            v_all[WINDOW:, :] = z_read[:, v_cols].astype(BF16)
            kv_ref[:, 0:kv_w] = z_read[:, k_cols]
            kv_ref[:, kv_w:] = z_read[:, v_cols]

        def keep_history():
            k_all[0:WINDOW, :] = k_all[unit:unit + WINDOW, :]
            v_all[0:WINDOW, :] = v_all[unit:unit + WINDOW, :]

        def recur(c):
            rows = slice(c * CHUNK, (c + 1) * CHUNK)

            def zcols(col):
                return z_read[rows, col * D_HG:(col + 1) * D_HG]

            outs = _hgrn_chunk(zcols(Z_QH), zcols(Z_FH), zcols(Z_IH), zcols(Z_GH), sel_ref[...],
                               lb, gain_ref[...], st_ref)
            for h in range(HG_HEADS):
                rec_ref[rows, h * HG_DV:(h + 1) * HG_DV] = outs[h]

        mixers = ([load_kv]
                  + _attend_stages(z_read, attn_ref, k_all, v_all, bias_ref, sinks_ref, u_in_seq * unit)
                  + [keep_history]
                  + [functools.partial(recur, c) for c in range(unit // CHUNK)])
        _interleave([functools.partial(project, c) for c in range(D_IN // MIX_PROJ_TN)], mixers)

    @pl.when(s % 2 == 0)
    def _():
        step(z_even, z_odd)

    @pl.when(s % 2 == 1)
    def _():
        step(z_odd, z_even)

    @pl.when((s > 0) & (u_in_seq == units_per_seq - 1))
    def _():
        for h in range(HG_HEADS):
            sout_ref[0, h] = st_ref[h].T


def _mixer_front(x, pre, w_in, bucket, table, sinks, lb_logits, gain, *, n_seq, seq_len, layer):
    unit = MIX_UNIT
    ups = seq_len // unit
    n_units = n_seq * ups
    n_layers = lb_logits.shape[0]
    kv_w = N_KV * HEAD_DIM
    sel = _hgrn_selector()
    smem = pl.BlockSpec(memory_space=pltpu.SMEM)

    def const(shape):
        return pl.BlockSpec(shape, lambda s: (0,) * len(shape))

    def prev_unit(s):
        return (jnp.maximum(s - 1, 0), 0)

    return pl.pallas_call(
        functools.partial(_mixer_kernel, layer=layer, units_per_seq=ups),
        grid=(n_units + 1,),
        in_specs=[
            const((CHUNK, LK)), smem, smem, const(sel.shape), const((n_layers, D_HG)),
            const((1, HG_DV)), const((1, D_MODEL)),
            pl.BlockSpec((unit, D_MODEL), lambda s: (jnp.minimum(s, n_units - 1), 0)),
            pl.BlockSpec((D_MODEL, D_IN), lambda s: (0, 0), pipeline_mode=pl.Buffered(1)),
        ],
        out_specs=[
            pl.BlockSpec((unit, D_ATTN), prev_unit),
            pl.BlockSpec((unit, D_HG), prev_unit),
            pl.BlockSpec((unit, 2 * kv_w), prev_unit),
            pl.BlockSpec((1, HG_HEADS, HG_DK, HG_DV), lambda s: (jnp.maximum(s - 1, 0) // ups, 0, 0, 0)),
        ],
        out_shape=[
            jax.ShapeDtypeStruct((n_units * unit, D_ATTN), BF16),
            jax.ShapeDtypeStruct((n_units * unit, D_HG), BF16),
            jax.ShapeDtypeStruct((n_units * unit, 2 * kv_w), F32),
            jax.ShapeDtypeStruct((n_seq, HG_HEADS, HG_DK, HG_DV), F32),
        ],
        scratch_shapes=[
            pltpu.VMEM((N_KV, GROUP * CHUNK, LK), F32),
            pltpu.VMEM((unit, D_IN), F32),
            pltpu.VMEM((unit, D_IN), F32),
            pltpu.VMEM((unit, D_MODEL), BF16),
            pltpu.VMEM((WINDOW + unit, kv_w), BF16),
            pltpu.VMEM((WINDOW + unit, kv_w), BF16),
            pltpu.VMEM((HG_HEADS, HG_DV, HG_DK), F32),
        ],
        compiler_params=_cparams(("arbitrary",)),
        name="mixer_front",
    )(bucket, table, sinks, sel, lb_logits, gain, pre, x, w_in)


def _mixer_cached_kernel(bucket_ref, table_ref, sinks_ref, sel_ref, lbl_ref, gain_ref, pre_ref, x_ref, w_ref,
                         kp_ref, vp_ref, s0_ref,
                         attn_ref, rec_ref, kv_ref, sout_ref,
                         bias_ref, z_even, z_odd, xn_ref, k_win, v_win, st_ref, *, layer):
    s = pl.program_id(0)
    unit = MIX_UNIT
    kv_w = N_KV * HEAD_DIM

    @pl.when(s == 0)
    def _():
        _fill_bias(bucket_ref, table_ref, bias_ref)
        z_odd[...] = jnp.zeros(z_odd.shape, F32)

    def step(z_write, z_read):
        xn_ref[...] = _rms(x_ref[...], pre_ref[...]).astype(BF16)

        def project(c):
            cols = slice(c * MIX_PROJ_TN, (c + 1) * MIX_PROJ_TN)
            z_write[:, cols] = jnp.dot(xn_ref[...], w_ref[:, cols], preferred_element_type=F32)

        k_cols = slice(Z_K * kv_w, (Z_K + 1) * kv_w)
        v_cols = slice(Z_V * kv_w, (Z_V + 1) * kv_w)
        lb = _lower_bound(lbl_ref, layer)

        def load_kv():
            for c in range(unit // CHUNK):
                hist = slice(c * WINDOW, (c + 1) * WINDOW)
                rows = slice(c * CHUNK, (c + 1) * CHUNK)
                k_win[c * LK:c * LK + WINDOW, :] = kp_ref[hist, :].astype(BF16)
                v_win[c * LK:c * LK + WINDOW, :] = vp_ref[hist, :].astype(BF16)
                k_win[c * LK + WINDOW:(c + 1) * LK, :] = z_read[rows, k_cols].astype(BF16)
                v_win[c * LK + WINDOW:(c + 1) * LK, :] = z_read[rows, v_cols].astype(BF16)
            kv_ref[:, 0:kv_w] = z_read[:, k_cols]
            kv_ref[:, kv_w:] = z_read[:, v_cols]

        def recur(c):
            rows = slice(c * CHUNK, (c + 1) * CHUNK)
            for h in range(HG_HEADS):
                st_ref[h] = s0_ref[c, h].T

            def zcols(col):
                return z_read[rows, col * D_HG:(col + 1) * D_HG]

            outs = _hgrn_chunk(zcols(Z_QH), zcols(Z_FH), zcols(Z_IH), zcols(Z_GH), sel_ref[...],
                               lb, gain_ref[...], st_ref)
            for h in range(HG_HEADS):
                rec_ref[rows, h * HG_DV:(h + 1) * HG_DV] = outs[h]
                sout_ref[c, h] = st_ref[h].T

        mixers = ([load_kv]
                  + _attend_stages(z_read, attn_ref, k_win, v_win, bias_ref, sinks_ref, None, win_stride=LK)
                  + [functools.partial(recur, c) for c in range(unit // CHUNK)])
        _interleave([functools.partial(project, c) for c in range(D_IN // MIX_PROJ_TN)], mixers)

    @pl.when(s % 2 == 0)
    def _():
        step(z_even, z_odd)

    @pl.when(s % 2 == 1)
    def _():
        step(z_odd, z_even)


def _mixer_front_cached(x, pre, w_in, bucket, table, sinks, lb_logits, gain, s0, k_cache, v_cache, *,
                        n_seq, layer):
    unit = MIX_UNIT
    per_unit = unit // CHUNK
    n_units = n_seq // per_unit
    n_layers = lb_logits.shape[0]
    kv_w = N_KV * HEAD_DIM
    sel = _hgrn_selector()
    smem = pl.BlockSpec(memory_space=pltpu.SMEM)

    def const(shape):
        return pl.BlockSpec(shape, lambda s: (0,) * len(shape))

    def prev_unit(s):
        return (jnp.maximum(s - 1, 0), 0)

    state_spec = pl.BlockSpec((per_unit, HG_HEADS, HG_DK, HG_DV), lambda s: (jnp.maximum(s - 1, 0), 0, 0, 0))
    return pl.pallas_call(
        functools.partial(_mixer_cached_kernel, layer=layer),
        grid=(n_units + 1,),
        in_specs=[
            const((CHUNK, LK)), smem, smem, const(sel.shape), const((n_layers, D_HG)),
            const((1, HG_DV)), const((1, D_MODEL)),
            pl.BlockSpec((unit, D_MODEL), lambda s: (jnp.minimum(s, n_units - 1), 0)),
            pl.BlockSpec((D_MODEL, D_IN), lambda s: (0, 0), pipeline_mode=pl.Buffered(1)),
            pl.BlockSpec((per_unit * WINDOW, kv_w), prev_unit),
            pl.BlockSpec((per_unit * WINDOW, kv_w), prev_unit),
            pl.BlockSpec((None, per_unit, HG_HEADS, HG_DK, HG_DV),
                         lambda s: (layer, jnp.maximum(s - 1, 0), 0, 0, 0)),
        ],
        out_specs=[
            pl.BlockSpec((unit, D_ATTN), prev_unit),
            pl.BlockSpec((unit, D_HG), prev_unit),
            pl.BlockSpec((unit, 2 * kv_w), prev_unit),
            state_spec,
        ],
        out_shape=[
            jax.ShapeDtypeStruct((n_units * unit, D_ATTN), BF16),
            jax.ShapeDtypeStruct((n_units * unit, D_HG), BF16),
            jax.ShapeDtypeStruct((n_units * unit, 2 * kv_w), F32),
            jax.ShapeDtypeStruct((n_seq, HG_HEADS, HG_DK, HG_DV), F32),
        ],
        scratch_shapes=[
            pltpu.VMEM((N_KV, GROUP * CHUNK, LK), F32),
            pltpu.VMEM((unit, D_IN), F32),
            pltpu.VMEM((unit, D_IN), F32),
            pltpu.VMEM((unit, D_MODEL), BF16),
            pltpu.VMEM((per_unit * LK, kv_w), BF16),
            pltpu.VMEM((per_unit * LK, kv_w), BF16),
            pltpu.VMEM((HG_HEADS, HG_DV, HG_DK), F32),
        ],
        compiler_params=_cparams(("arbitrary",)),
        name="mixer_front_cached",
    )(bucket, table, sinks, sel, lb_logits, gain, pre, x, w_in, k_cache, v_cache, s0)


def _outproj_kernel(x_ref, a_ref, r_ref, w_ref, post_ref, o_ref):
    nsub = x_ref.shape[0] // SUB_ROWS
    sub = [slice(r * SUB_ROWS, (r + 1) * SUB_ROWS) for r in range(nsub)]
    mix = [None] * nsub

    def project(r):
        mix[r] = (jnp.dot(a_ref[sub[r], :], w_ref[0:D_ATTN, :], preferred_element_type=F32)
                  + jnp.dot(r_ref[sub[r], :], w_ref[D_ATTN:, :], preferred_element_type=F32))

    def finish(r):
        o_ref[sub[r], :] = x_ref[sub[r], :] + _rms(mix[r], post_ref[...])

    _pipelined(nsub, None, project, finish)


def _outproj(x, attn, rec, w_out, post):
    t = x.shape[0]
    tm = min(ROW_TM, t)
    return pl.pallas_call(
        _outproj_kernel,
        grid=(t // tm,),
        in_specs=[
            pl.BlockSpec((tm, D_MODEL), lambda i: (i, 0)),
            pl.BlockSpec((tm, D_ATTN), lambda i: (i, 0)),
            pl.BlockSpec((tm, D_HG), lambda i: (i, 0)),
            pl.BlockSpec((D_MODEL, D_MODEL), lambda i: (0, 0), pipeline_mode=pl.Buffered(1)),
            pl.BlockSpec((1, D_MODEL), lambda i: (0, 0)),
        ],
        out_specs=pl.BlockSpec((tm, D_MODEL), lambda i: (i, 0)),
        out_shape=jax.ShapeDtypeStruct((t, D_MODEL), F32),
        compiler_params=_cparams(("parallel",)),
        name="mixer_out_proj",
    )(x, attn, rec, w_out, post)


def _ple_kernel(x_ref, p_ref, pre_ref, post_ref, wg_ref, wp_ref, o_ref):
    nsub = x_ref.shape[0] // SUB_ROWS
    sub = [slice(r * SUB_ROWS, (r + 1) * SUB_ROWS) for r in range(nsub)]
    hs, gates, projs = [None] * nsub, [None] * nsub, [None] * nsub

    def norm(r):
        hs[r] = _rms(x_ref[sub[r], :], pre_ref[...]).astype(BF16)

    def project(r):
        gates[r] = jnp.dot(hs[r], wg_ref[...], preferred_element_type=F32)
        projs[r] = jnp.dot(p_ref[sub[r], :].astype(BF16), wp_ref[...], preferred_element_type=F32)

    def finish(r):
        o_ref[sub[r], :] = x_ref[sub[r], :] + _rms(_sigmoid(gates[r]) * projs[r], post_ref[...])

    _pipelined(nsub, norm, project, finish)


def _ple(x, p_layers, layer, pre, post, w_gate, w_proj):
    t = x.shape[0]
    tm = min(ROW_TM, t)
    return pl.pallas_call(
        _ple_kernel,
        grid=(t // tm,),
        in_specs=[
            pl.BlockSpec((tm, D_MODEL), lambda i: (i, 0)),
            pl.BlockSpec((None, tm, PLE_DIM), lambda i: (layer, i, 0)),
            pl.BlockSpec((1, D_MODEL), lambda i: (0, 0)),
            pl.BlockSpec((1, D_MODEL), lambda i: (0, 0)),
            pl.BlockSpec((D_MODEL, D_MODEL), lambda i: (0, 0), pipeline_mode=pl.Buffered(1)),
            pl.BlockSpec((PLE_DIM, D_MODEL), lambda i: (0, 0)),
        ],
        out_specs=pl.BlockSpec((tm, D_MODEL), lambda i: (i, 0)),
        out_shape=jax.ShapeDtypeStruct((t, D_MODEL), F32),
        compiler_params=_cparams(("parallel",)),
        name="ple_embed",
    )(x, p_layers, pre, post, w_gate, w_proj)


def _t5_bucket(rel):
    half = NUM_BUCKETS // 2
    max_exact = half // 2
    n = jnp.abs(rel)
    nf = jnp.maximum(n, 1).astype(jnp.float32)
    large = max_exact + (jnp.log(nf / max_exact) / math.log(MAX_DISTANCE / max_exact)
                         * (half - max_exact)).astype(jnp.int32)
    large = jnp.minimum(large, half - 1)
    return jnp.where(rel > 0, half, 0) + jnp.where(n < max_exact, n, large)


def _ffn1(x, w):
    row = lambda v: v.reshape(1, -1)
    return _ffn(x, row(w['ffn1_pre']), row(w['ffn1_post']), w['ffn1_w_gate'], w['ffn1_w_up'], w['ffn1_w_down'])


def _back_half(x, p, layer, attn, rec, w):
    row = lambda v: v.reshape(1, -1)
    x = _outproj(x, attn, rec, w['w_out'], row(w['mix_post']))
    x = _ffn(x, row(w['ffn2_pre']), row(w['ffn2_post']), w['ffn2_w_gate'], w['ffn2_w_up'], w['ffn2_w_down'])
    return _ple(x, p, layer, row(w['ple_pre']), row(w['ple_post']), w['w_ple_gate'], w['w_ple_proj'])


def _prompt_layer(x, p, w, table, bucket, layer, *, n_seq, seq_len):
    x = _ffn1(x, w)
    attn, rec, kv, s_new = _mixer_front(
        x, w['mix_pre'].reshape(1, -1), w['w_in'], bucket, table, w['attn_sinks'], w['hgrn_lb_logits'],
        w['hgrn_norm'].reshape(1, -1), n_seq=n_seq, seq_len=seq_len, layer=layer)
    return _back_half(x, p, layer, attn, rec, w), kv, s_new


def _sample_layer(x, p, w, table, bucket, layer, *, n_seq, s0, k_cache, v_cache):
    kv_w = N_KV * HEAD_DIM
    x = _ffn1(x, w)
    attn, rec, kv, s_new = _mixer_front_cached(
        x, w['mix_pre'].reshape(1, -1), w['w_in'], bucket, table, w['attn_sinks'], w['hgrn_lb_logits'],
        w['hgrn_norm'].reshape(1, -1), s0, k_cache.reshape(n_seq * WINDOW, kv_w),
        v_cache.reshape(n_seq * WINDOW, kv_w), n_seq=n_seq, layer=layer)
    return _back_half(x, p, layer, attn, rec, w), kv, s_new


def kernel(x_prompt, x_sample, cache_attn_k, cache_attn_v, state_hgrn, p_prompt, p_sample,
           rel_bias_table, ffn1_pre, ffn1_post, ffn1_w_gate, ffn1_w_up, ffn1_w_down,
           mix_pre, mix_post, w_in, w_out, attn_sinks, hgrn_lb_logits, hgrn_norm,
           ffn2_pre, ffn2_post, ffn2_w_gate, ffn2_w_up, ffn2_w_down,
           ple_pre, ple_post, w_ple_gate, w_ple_proj):
    depth = w_in.shape[0]
    bp, sp, _ = x_prompt.shape
    bs, ss, _ = x_sample.shape
    wc = cache_attn_k.shape[2]
    assert wc == WINDOW and ss == CHUNK and sp % MIX_UNIT == 0 and sp >= WINDOW
    assert (bs * ss) % MIX_UNIT == 0

    rel = jnp.arange(LK)[None, :] - WINDOW - jnp.arange(CHUNK)[:, None]
    bucket = _t5_bucket(rel).astype(jnp.int32)

    kv_w = N_KV * HEAD_DIM
    k_lo, v_lo = D_ATTN, D_ATTN + kv_w
    yp = x_prompt.reshape(bp * sp, D_MODEL)
    ys = x_sample.reshape(bs * ss, D_MODEL)
    outs = [[] for _ in range(6)]
    for l in range(depth):
        w_in_l = w_in[l]
        w = {
            'ffn1_pre': ffn1_pre[l], 'ffn1_post': ffn1_post[l],
            'ffn1_w_gate': ffn1_w_gate[l].astype(BF16), 'ffn1_w_up': ffn1_w_up[l].astype(BF16),
            'ffn1_w_down': ffn1_w_down[l].astype(BF16),
            'mix_pre': mix_pre[l], 'mix_post': mix_post[l],
            'w_in': jnp.concatenate([w_in_l[:, :k_lo], w_in_l[:, v_lo + kv_w:], w_in_l[:, k_lo:v_lo + kv_w]],
                                    axis=1).astype(BF16),
            'w_out': w_out[l].astype(BF16), 'attn_sinks': attn_sinks[l],
            'hgrn_lb_logits': hgrn_lb_logits, 'hgrn_norm': hgrn_norm[l],
            'ffn2_pre': ffn2_pre[l], 'ffn2_post': ffn2_post[l],
            'ffn2_w_gate': ffn2_w_gate[l].astype(BF16), 'ffn2_w_up': ffn2_w_up[l].astype(BF16),
            'ffn2_w_down': ffn2_w_down[l].astype(BF16),
            'ple_pre': ple_pre[l], 'ple_post': ple_post[l],
            'w_ple_gate': w_ple_gate[l].astype(BF16), 'w_ple_proj': w_ple_proj[l].astype(BF16),
        }
        yp, kv_p, st_p = _prompt_layer(yp, p_prompt.reshape(depth, bp * sp, PLE_DIM), w, rel_bias_table, bucket, l,
                                       n_seq=bp, seq_len=sp)
        ys, kv_s, st_s = _sample_layer(ys, p_sample.reshape(depth, bs * ss, PLE_DIM), w, rel_bias_table, bucket, l,
                                       n_seq=bs, s0=state_hgrn, k_cache=cache_attn_k[l],
                                       v_cache=cache_attn_v[l])
        kv_p = kv_p.reshape(bp, sp, 2 * kv_w)[:, sp - WINDOW:].reshape(bp, WINDOW, 2, N_KV, HEAD_DIM)
        outs[0].append(kv_p[:, :, 0])
        outs[1].append(kv_p[:, :, 1])
        outs[2].append(st_p)
        kv_s = kv_s.reshape(bs, ss, 2, N_KV, HEAD_DIM)
        outs[3].append(jnp.concatenate([cache_attn_k[l], kv_s[:, :, 0]], axis=1)[:, ss:])
        outs[4].append(jnp.concatenate([cache_attn_v[l], kv_s[:, :, 1]], axis=1)[:, ss:])
        outs[5].append(st_s)
    return (yp.reshape(bp, sp, D_MODEL), ys.reshape(bs, ss, D_MODEL),
            jnp.stack(outs[0]), jnp.stack(outs[1]), jnp.stack(outs[2]),
            jnp.stack(outs[3]), jnp.stack(outs[4]), jnp.stack(outs[5]))
```

```python
import functools
import math

import jax
import jax.numpy as jnp
import numpy as np
from jax import lax
from jax.experimental import pallas as pl
from jax.experimental.pallas import tpu as pltpu

F32 = jnp.float32
BF16 = jnp.bfloat16

D_MODEL = 2048
CHUNK = 64
D_ATTN = 1024
HEAD_DIM = 128
N_HEADS = 8
N_KV = 2
GROUP = N_HEADS // N_KV
WINDOW = 128
LK = WINDOW + CHUNK
NUM_BUCKETS = 32
MAX_DISTANCE = 128
HG_DK = 128
HG_DV = 128
HG_HEADS = 8
D_HG = HG_HEADS * HG_DV
HG_SUB = 16
HG_SUB_SHIFT = HG_SUB.bit_length() - 1
CHUNK_SHIFT = CHUNK.bit_length() - 1
D_FF = 5632
PLE_DIM = 256
EPS = 1e-6
NEG_INF = -1e30
D_IN = D_ATTN + 2 * N_KV * HEAD_DIM + 4 * D_HG

Z_QH, Z_FH, Z_IH, Z_GH = 1, 2, 3, 4
Z_K, Z_V = 20, 21

VMEM_LIMIT_BYTES = 60 * 1024 * 1024

FFN_TM = 1024
FFN_TF = 512
ROW_TM = 1024
SUB_ROWS = 256
MIX_UNIT = 256
MIX_PROJ_TN = 512


def _rms(x, gain):
    y = x * lax.rsqrt(jnp.mean(x * x, axis=-1, keepdims=True) + EPS)
    return y * gain


def _sigmoid(x):
    return 1.0 / (1.0 + jnp.exp(-x))


def _cparams(sem):
    return pltpu.CompilerParams(dimension_semantics=sem, vmem_limit_bytes=VMEM_LIMIT_BYTES)


def _pipelined(n, before, matmul, after):
    if before is not None:
        before(0)
    for r in range(n):
        if before is not None and r + 1 < n:
            before(r + 1)
        matmul(r)
        if after is not None and r > 0:
            after(r - 1)
    if after is not None:
        after(n - 1)


def _ffn_kernel(x_ref, pre_ref, post_ref, wg_ref, wu_ref, wd_ref, o_ref, xn_ref):
    j = pl.program_id(1)
    last = pl.num_programs(1) - 1
    nsub = x_ref.shape[0] // SUB_ROWS
    sub = [slice(r * SUB_ROWS, (r + 1) * SUB_ROWS) for r in range(nsub)]

    def swiglu(rows):
        xn = xn_ref[rows, :]
        g = jnp.dot(xn, wg_ref[...], preferred_element_type=F32)
        u = jnp.dot(xn, wu_ref[...], preferred_element_type=F32)
        h = (g * _sigmoid(g) * u).astype(BF16)
        return jnp.dot(h, wd_ref[...], preferred_element_type=F32)

    @pl.when(j == 0)
    def _():
        def norm(r):
            xn_ref[sub[r], :] = _rms(x_ref[sub[r], :], pre_ref[...]).astype(BF16)

        def first(r):
            o_ref[sub[r], :] = swiglu(sub[r])

        _pipelined(nsub, norm, first, None)

    @pl.when((j > 0) & (j < last))
    def _():
        o_ref[...] += swiglu(slice(None))

    @pl.when(j == last)
    def _():
        ys = [None] * nsub
        half_post = 0.5 * post_ref[...]

        def final(r):
            ys[r] = o_ref[sub[r], :] + swiglu(sub[r])

        def finish(r):
            o_ref[sub[r], :] = x_ref[sub[r], :] + _rms(ys[r], half_post)

        _pipelined(nsub, None, final, finish)


def _ffn(x, pre, post, wg, wu, wd):
    t = x.shape[0]
    tm = min(FFN_TM, t)
    grid = (t // tm, D_FF // FFN_TF)
    assert grid[1] >= 3
    return pl.pallas_call(
        _ffn_kernel,
        grid=grid,
        in_specs=[
            pl.BlockSpec((tm, D_MODEL), lambda i, j: (i, 0)),
            pl.BlockSpec((1, D_MODEL), lambda i, j: (0, 0)),
            pl.BlockSpec((1, D_MODEL), lambda i, j: (0, 0)),
            pl.BlockSpec((D_MODEL, FFN_TF), lambda i, j: (0, j)),
            pl.BlockSpec((D_MODEL, FFN_TF), lambda i, j: (0, j)),
            pl.BlockSpec((FFN_TF, D_MODEL), lambda i, j: (j, 0)),
        ],
        out_specs=pl.BlockSpec((tm, D_MODEL), lambda i, j: (i, 0)),
        out_shape=jax.ShapeDtypeStruct((t, D_MODEL), F32),
        scratch_shapes=[pltpu.VMEM((tm, D_MODEL), BF16)],
        compiler_params=_cparams(("parallel", "arbitrary")),
        name="ffn_half_step",
    )(x, pre, post, wg, wu, wd)


def _fill_bias(bucket_ref, table_ref, bias_ref):
    bucket = bucket_ref[...]
    for h in range(N_HEADS):
        acc = jnp.zeros((CHUNK, LK), F32)
        for n in range(NUM_BUCKETS):
            acc = jnp.where(bucket == n, table_ref[n, h], acc)
        g, hh = divmod(h, GROUP)
        bias_ref[g, hh * CHUNK:(hh + 1) * CHUNK, :] = acc


def _sink_column(sinks_ref, g):
    row = lax.broadcasted_iota(jnp.int32, (GROUP * CHUNK, 1), 0)
    sink = jnp.zeros((GROUP * CHUNK, 1), F32)
    for hh in range(GROUP):
        sink = jnp.where((row >> CHUNK_SHIFT) == hh, sinks_ref[g * GROUP + hh], sink)
    return sink


def _attend_chunk(q_of_head, kk, vv, bias_g, sink, first_key_pos):
    qs = jnp.concatenate([q_of_head(hh) for hh in range(GROUP)], axis=0).astype(BF16)
    s = lax.dot_general(qs, kk, (((1,), (1,)), ((), ())), preferred_element_type=F32)
    s = s * (HEAD_DIM ** -0.5) + bias_g
    if first_key_pos is not None:
        col = lax.broadcasted_iota(jnp.int32, (GROUP * CHUNK, LK), 1)
        s = jnp.where(col + first_key_pos >= 0, s, NEG_INF)
    m = jnp.maximum(jnp.max(s, axis=-1, keepdims=True), sink)
    e = jnp.exp(s - m)
    p = e / (jnp.sum(e, axis=-1, keepdims=True) + jnp.exp(sink - m))
    return jnp.dot(p.astype(BF16), vv, preferred_element_type=F32)


def _attend_stages(q_ref, o_ref, k_all, v_all, bias_ref, sinks_ref, pos0, win_stride=CHUNK):
    def stage(c, g):
        rows = slice(c * CHUNK, (c + 1) * CHUNK)
        cols = slice(g * HEAD_DIM, (g + 1) * HEAD_DIM)
        win = slice(c * win_stride, c * win_stride + LK)
        masked = pos0 is not None and c * CHUNK < WINDOW

        def q_of_head(hh):
            h = g * GROUP + hh
            return q_ref[rows, h * HEAD_DIM:(h + 1) * HEAD_DIM]

        o = _attend_chunk(q_of_head, k_all[win, cols], v_all[win, cols], bias_ref[g], _sink_column(sinks_ref, g),
                          pos0 + (c * CHUNK - WINDOW) if masked else None)
        for hh in range(GROUP):
            h = g * GROUP + hh
            o_ref[rows, h * HEAD_DIM:(h + 1) * HEAD_DIM] = o[hh * CHUNK:(hh + 1) * CHUNK, :].astype(BF16)

    return [functools.partial(stage, c, g) for c in range(q_ref.shape[0] // CHUNK) for g in range(N_KV)]


def _interleave(a, b):
    ia = ib = 0
    while ia < len(a) or ib < len(b):
        if ib >= len(b) or (ia < len(a) and ia * len(b) <= ib * len(a)):
            a[ia]()
            ia += 1
        else:
            b[ib]()
            ib += 1


_HG_RANGES = [(0, 16), (0, 32), (0, 48), (0, 64), (16, 32), (32, 48), (48, 64), (16, 48), (16, 64), (32, 64)]
_HG_RANGE_ROWS = 16


def _hgrn_selector():
    t = np.arange(CHUNK)
    within = (t[:, None] // HG_SUB == t[None, :] // HG_SUB) & (t[None, :] <= t[:, None])
    ranges = np.zeros((_HG_RANGE_ROWS, CHUNK), bool)
    for n, (lo, hi) in enumerate(_HG_RANGES):
        ranges[n, lo:hi] = True
    once = np.concatenate([within, ranges], axis=0)
    return jnp.asarray(np.concatenate([once, once, once], axis=1), BF16)


def _lower_bound(lbl_ref, layer):
    logits = lbl_ref[...]
    ex = jnp.exp(logits - jnp.max(logits, axis=0, keepdims=True))
    return jnp.sum(ex[:layer + 1], axis=0, keepdims=True) / jnp.sum(ex, axis=0, keepdims=True)


def _hgrn_chunk(q, fl, iv, gate, sel, lb, gain, st_ref):
    nsub = CHUNK // HG_SUB
    s1, s2, s3, s4 = HG_SUB, 2 * HG_SUB, 3 * HG_SUB, 4 * HG_SUB
    t_i = lax.broadcasted_iota(jnp.int32, (CHUNK, nsub * CHUNK), 0)
    c_i = lax.broadcasted_iota(jnp.int32, (CHUNK, nsub * CHUNK), 1)
    keep = ((t_i >> HG_SUB_SHIFT) == (c_i >> CHUNK_SHIFT)) & ((c_i & (CHUNK - 1)) <= t_i)
    zeros_sub = jnp.zeros((HG_SUB, HG_DK), BF16)

    def sub(a, j):
        return a[j * HG_SUB:(j + 1) * HG_SUB]

    f = lb + (1.0 - lb) * _sigmoid(fl)
    lf = jnp.log(f)
    k = 1.0 - f
    hi = lf.astype(BF16)
    r1 = lf - hi.astype(F32)
    mid = r1.astype(BF16)
    lo = (r1 - mid.astype(F32)).astype(BF16)
    cr = jnp.dot(sel, jnp.concatenate([hi, mid, lo], axis=0), preferred_element_type=F32)
    cc = cr[:CHUNK]
    ev = jnp.exp(cr[CHUNK:])
    qd = q * jnp.exp(cc)
    ki = k * jnp.exp(-cc)
    iv = iv.astype(BF16)
    gate = gate * _sigmoid(gate)
    head_cols = [slice(h * HG_DK, (h + 1) * HG_DK) for h in range(HG_HEADS)]

    def dk(sl, lo_row, hi_row):
        n = _HG_RANGES.index((lo_row, hi_row))
        return ev[n:n + 1, sl]

    scores, inter, upds = [], [], []
    for h, sl in enumerate(head_cols):
        qd_h, ki_h = qd[:, sl], ki[:, sl]
        qd_b = qd_h.astype(BF16)
        ki_b = [sub(ki_h, j).astype(BF16) for j in range(nsub)]
        kend = [sub(ki_h, 0) * dk(sl, 0, s1), sub(ki_h, 1) * dk(sl, s1, s2),
                sub(ki_h, 2) * dk(sl, s2, s3), sub(ki_h, 3) * dk(sl, s3, s4)]
        kend_b = [kend[j].astype(BF16) for j in range(nsub - 1)]
        kmat = [
            [ki_b[0], zeros_sub, zeros_sub, zeros_sub],
            [kend_b[0], ki_b[1], zeros_sub, zeros_sub],
            [(kend[0] * dk(sl, s1, s2)).astype(BF16), kend_b[1], ki_b[2], zeros_sub],
            [(kend[0] * dk(sl, s1, s3)).astype(BF16), (kend[1] * dk(sl, s2, s3)).astype(BF16),
             kend_b[2], ki_b[3]],
        ]
        scores.append(lax.dot_general(qd_b, jnp.concatenate([p for row in kmat for p in row], axis=0),
                                      (((1,), (1,)), ((), ())), preferred_element_type=F32))
        qe = jnp.concatenate([sub(qd_h, 0), sub(qd_h, 1) * dk(sl, 0, s1), sub(qd_h, 2) * dk(sl, 0, s2),
                              sub(qd_h, 3) * dk(sl, 0, s3)], axis=0).astype(BF16)
        ke = jnp.concatenate([kend[0] * dk(sl, s1, s4), kend[1] * dk(sl, s2, s4),
                              kend[2] * dk(sl, s3, s4), kend[3]], axis=0).astype(BF16)
        inter.append(lax.dot_general(qe, st_ref[h].astype(BF16), (((1,), (1,)), ((), ())),
                                     preferred_element_type=F32))
        upds.append(lax.dot_general(iv[:, sl], ke, (((0,), (0,)), ((), ())),
                                    preferred_element_type=F32))
    outs = []
    for h, sl in enumerate(head_cols):
        a = jnp.where(keep, scores[h], 0.0).astype(BF16)
        outs.append(inter[h] + jnp.dot(a, jnp.concatenate([iv[:, sl]] * nsub, axis=0),
                                       preferred_element_type=F32))
        st_ref[h] = st_ref[h] * dk(sl, 0, s4) + upds[h]
    return [(_rms(outs[h], gain) * gate[:, sl]).astype(BF16) for h, sl in enumerate(head_cols)]


def _mixer_kernel(bucket_ref, table_ref, sinks_ref, sel_ref, lbl_ref, gain_ref, pre_ref, x_ref, w_ref,
                  attn_ref, rec_ref, kv_ref, sout_ref,
                  bias_ref, z_even, z_odd, xn_ref, k_all, v_all, st_ref, *, layer, units_per_seq):
    s = pl.program_id(0)
    unit = MIX_UNIT
    kv_w = N_KV * HEAD_DIM
    u_in_seq = (s + (units_per_seq - 1)) % units_per_seq

    @pl.when(s == 0)
    def _():
        _fill_bias(bucket_ref, table_ref, bias_ref)
        z_odd[...] = jnp.zeros(z_odd.shape, F32)
        k_all[0:WINDOW, :] = jnp.zeros((WINDOW, kv_w), BF16)
        v_all[0:WINDOW, :] = jnp.zeros((WINDOW, kv_w), BF16)

    @pl.when((s == 0) | (u_in_seq == 0))
    def _():
        st_ref[...] = jnp.zeros(st_ref.shape, F32)

    def step(z_write, z_read):
        xn_ref[...] = _rms(x_ref[...], pre_ref[...]).astype(BF16)

        def project(c):
            cols = slice(c * MIX_PROJ_TN, (c + 1) * MIX_PROJ_TN)
            z_write[:, cols] = jnp.dot(xn_ref[...], w_ref[:, cols], preferred_element_type=F32)

        k_cols = slice(Z_K * kv_w, (Z_K + 1) * kv_w)
        v_cols = slice(Z_V * kv_w, (Z_V + 1) * kv_w)
        lb = _lower_bound(lbl_ref, layer)

        def load_kv():
            k_all[WINDOW:, :] = z_read[:, k_cols].astype(BF16)
            v_all[WINDOW:, :] = z_read[:, v_cols].astype(BF16)
            kv_ref[:, 0:kv_w] = z_read[:, k_cols]
            kv_ref[:, kv_w:] = z_read[:, v_cols]

        def keep_history():
            k_all[0:WINDOW, :] = k_all[unit:unit + WINDOW, :]
            v_all[0:WINDOW, :] = v_all[unit:unit + WINDOW, :]

        def recur(c):
            rows = slice(c * CHUNK, (c + 1) * CHUNK)

            def zcols(col):
                return z_read[rows, col * D_HG:(col + 1) * D_HG]

            outs = _hgrn_chunk(zcols(Z_QH), zcols(Z_FH), zcols(Z_IH), zcols(Z_GH), sel_ref[...],
                               lb, gain_ref[...], st_ref)
            for h in range(HG_HEADS):
                rec_ref[rows, h * HG_DV:(h + 1) * HG_DV] = outs[h]

        mixers = ([load_kv]
                  + _attend_stages(z_read, attn_ref, k_all, v_all, bias_ref, sinks_ref, u_in_seq * unit)
                  + [keep_history]
                  + [functools.partial(recur, c) for c in range(unit // CHUNK)])
        _interleave([functools.partial(project, c) for c in range(D_IN // MIX_PROJ_TN)], mixers)

    @pl.when(s % 2 == 0)
    def _():
        step(z_even, z_odd)

    @pl.when(s % 2 == 1)
    def _():
        step(z_odd, z_even)

    @pl.when((s > 0) & (u_in_seq == units_per_seq - 1))
    def _():
        for h in range(HG_HEADS):
            sout_ref[0, h] = st_ref[h].T


def _mixer_front(x, pre, w_in, bucket, table, sinks, lb_logits, gain, *, n_seq, seq_len, layer):
    unit = MIX_UNIT
    ups = seq_len // unit
    n_units = n_seq * ups
    n_layers = lb_logits.shape[0]
    kv_w = N_KV * HEAD_DIM
    sel = _hgrn_selector()
    smem = pl.BlockSpec(memory_space=pltpu.SMEM)

    def const(shape):
        return pl.BlockSpec(shape, lambda s: (0,) * len(shape))

    def prev_unit(s):
        return (jnp.maximum(s - 1, 0), 0)

    return pl.pallas_call(
        functools.partial(_mixer_kernel, layer=layer, units_per_seq=ups),
        grid=(n_units + 1,),
        in_specs=[
            const((CHUNK, LK)), smem, smem, const(sel.shape), const((n_layers, D_HG)),
            const((1, HG_DV)), const((1, D_MODEL)),
            pl.BlockSpec((unit, D_MODEL), lambda s: (jnp.minimum(s, n_units - 1), 0)),
            pl.BlockSpec((D_MODEL, D_IN), lambda s: (0, 0), pipeline_mode=pl.Buffered(1)),
        ],
        out_specs=[
            pl.BlockSpec((unit, D_ATTN), prev_unit),
            pl.BlockSpec((unit, D_HG), prev_unit),
            pl.BlockSpec((unit, 2 * kv_w), prev_unit),
            pl.BlockSpec((1, HG_HEADS, HG_DK, HG_DV), lambda s: (jnp.maximum(s - 1, 0) // ups, 0, 0, 0)),
        ],
        out_shape=[
            jax.ShapeDtypeStruct((n_units * unit, D_ATTN), BF16),
            jax.ShapeDtypeStruct((n_units * unit, D_HG), BF16),
            jax.ShapeDtypeStruct((n_units * unit, 2 * kv_w), F32),
            jax.ShapeDtypeStruct((n_seq, HG_HEADS, HG_DK, HG_DV), F32),
        ],
        scratch_shapes=[
            pltpu.VMEM((N_KV, GROUP * CHUNK, LK), F32),
            pltpu.VMEM((unit, D_IN), F32),
            pltpu.VMEM((unit, D_IN), F32),
            pltpu.VMEM((unit, D_MODEL), BF16),
            pltpu.VMEM((WINDOW + unit, kv_w), BF16),
            pltpu.VMEM((WINDOW + unit, kv_w), BF16),
            pltpu.VMEM((HG_HEADS, HG_DV, HG_DK), F32),
        ],
        compiler_params=_cparams(("arbitrary",)),
        name="mixer_front",
    )(bucket, table, sinks, sel, lb_logits, gain, pre, x, w_in)


def _mixer_cached_kernel(bucket_ref, table_ref, sinks_ref, sel_ref, lbl_ref, gain_ref, pre_ref, x_ref, w_ref,
                         kp_ref, vp_ref, s0_ref,
                         attn_ref, rec_ref, kv_ref, sout_ref,
                         bias_ref, z_even, z_odd, xn_ref, k_win, v_win, st_ref, *, layer):
    s = pl.program_id(0)
    unit = MIX_UNIT
    kv_w = N_KV * HEAD_DIM

    @pl.when(s == 0)
    def _():
        _fill_bias(bucket_ref, table_ref, bias_ref)
        z_odd[...] = jnp.zeros(z_odd.shape, F32)

    def step(z_write, z_read):
        xn_ref[...] = _rms(x_ref[...], pre_ref[...]).astype(BF16)

        def project(c):
            cols = slice(c * MIX_PROJ_TN, (c + 1) * MIX_PROJ_TN)
            z_write[:, cols] = jnp.dot(xn_ref[...], w_ref[:, cols], preferred_element_type=F32)

        k_cols = slice(Z_K * kv_w, (Z_K + 1) * kv_w)
        v_cols = slice(Z_V * kv_w, (Z_V + 1) * kv_w)
        lb = _lower_bound(lbl_ref, layer)

        def load_kv():
            for c in range(unit // CHUNK):
                hist = slice(c * WINDOW, (c + 1) * WINDOW)
                rows = slice(c * CHUNK, (c + 1) * CHUNK)
                k_win[c * LK:c * LK + WINDOW, :] = kp_ref[hist, :].astype(BF16)
                v_win[c * LK:c * LK + WINDOW, :] = vp_ref[hist, :].astype(BF16)
                k_win[c * LK + WINDOW:(c + 1) * LK, :] = z_read[rows, k_cols].astype(BF16)
                v_win[c * LK + WINDOW:(c + 1) * LK, :] = z_read[rows, v_cols].astype(BF16)
            kv_ref[:, 0:kv_w] = z_read[:, k_cols]
            kv_ref[:, kv_w:] = z_read[:, v_cols]

        def recur(c):
            rows = slice(c * CHUNK, (c + 1) * CHUNK)
            for h in range(HG_HEADS):
                st_ref[h] = s0_ref[c, h].T

            def zcols(col):
                return z_read[rows, col * D_HG:(col + 1) * D_HG]

            outs = _hgrn_chunk(zcols(Z_QH), zcols(Z_FH), zcols(Z_IH), zcols(Z_GH), sel_ref[...],
                               lb, gain_ref[...], st_ref)
            for h in range(HG_HEADS):
                rec_ref[rows, h * HG_DV:(h + 1) * HG_DV] = outs[h]
                sout_ref[c, h] = st_ref[h].T

        mixers = ([load_kv]
                  + _attend_stages(z_read, attn_ref, k_win, v_win, bias_ref, sinks_ref, None, win_stride=LK)
                  + [functools.partial(recur, c) for c in range(unit // CHUNK)])
        _interleave([functools.partial(project, c) for c in range(D_IN // MIX_PROJ_TN)], mixers)

    @pl.when(s % 2 == 0)
    def _():
        step(z_even, z_odd)

    @pl.when(s % 2 == 1)
    def _():
        step(z_odd, z_even)


def _mixer_front_cached(x, pre, w_in, bucket, table, sinks, lb_logits, gain, s0, k_cache, v_cache, *,
                        n_seq, layer):
    unit = MIX_UNIT
    per_unit = unit // CHUNK
    n_units = n_seq // per_unit
    n_layers = lb_logits.shape[0]
    kv_w = N_KV * HEAD_DIM
    sel = _hgrn_selector()
    smem = pl.BlockSpec(memory_space=pltpu.SMEM)

    def const(shape):
        return pl.BlockSpec(shape, lambda s: (0,) * len(shape))

    def prev_unit(s):
        return (jnp.maximum(s - 1, 0), 0)

    state_spec = pl.BlockSpec((per_unit, HG_HEADS, HG_DK, HG_DV), lambda s: (jnp.maximum(s - 1, 0), 0, 0, 0))
    return pl.pallas_call(
        functools.partial(_mixer_cached_kernel, layer=layer),
        grid=(n_units + 1,),
        in_specs=[
            const((CHUNK, LK)), smem, smem, const(sel.shape), const((n_layers, D_HG)),
            const((1, HG_DV)), const((1, D_MODEL)),
            pl.BlockSpec((unit, D_MODEL), lambda s: (jnp.minimum(s, n_units - 1), 0)),
            pl.BlockSpec((D_MODEL, D_IN), lambda s: (0, 0), pipeline_mode=pl.Buffered(1)),
            pl.BlockSpec((per_unit * WINDOW, kv_w), prev_unit),
            pl.BlockSpec((per_unit * WINDOW, kv_w), prev_unit),
            state_spec,
        ],
        out_specs=[
            pl.BlockSpec((unit, D_ATTN), prev_unit),
            pl.BlockSpec((unit, D_HG), prev_unit),
            pl.BlockSpec((unit, 2 * kv_w), prev_unit),
            state_spec,
        ],
        out_shape=[
            jax.ShapeDtypeStruct((n_units * unit, D_ATTN), BF16),
            jax.ShapeDtypeStruct((n_units * unit, D_HG), BF16),
            jax.ShapeDtypeStruct((n_units * unit, 2 * kv_w), F32),
            jax.ShapeDtypeStruct((n_seq, HG_HEADS, HG_DK, HG_DV), F32),
        ],
        scratch_shapes=[
            pltpu.VMEM((N_KV, GROUP * CHUNK, LK), F32),
            pltpu.VMEM((unit, D_IN), F32),
            pltpu.VMEM((unit, D_IN), F32),
            pltpu.VMEM((unit, D_MODEL), BF16),
            pltpu.VMEM((per_unit * LK, kv_w), BF16),
            pltpu.VMEM((per_unit * LK, kv_w), BF16),
            pltpu.VMEM((HG_HEADS, HG_DV, HG_DK), F32),
        ],
        compiler_params=_cparams(("arbitrary",)),
        name="mixer_front_cached",
    )(bucket, table, sinks, sel, lb_logits, gain, pre, x, w_in, k_cache, v_cache, s0)


def _outproj_kernel(x_ref, a_ref, r_ref, w_ref, post_ref, o_ref):
    nsub = x_ref.shape[0] // SUB_ROWS
    sub = [slice(r * SUB_ROWS, (r + 1) * SUB_ROWS) for r in range(nsub)]
    mix = [None] * nsub

    def project(r):
        mix[r] = (jnp.dot(a_ref[sub[r], :], w_ref[0:D_ATTN, :], preferred_element_type=F32)
                  + jnp.dot(r_ref[sub[r], :], w_ref[D_ATTN:, :], preferred_element_type=F32))

    def finish(r):
        o_ref[sub[r], :] = x_ref[sub[r], :] + _rms(mix[r], post_ref[...])

    _pipelined(nsub, None, project, finish)


def _outproj(x, attn, rec, w_out, post):
    t = x.shape[0]
    tm = min(ROW_TM, t)
    return pl.pallas_call(
        _outproj_kernel,
        grid=(t // tm,),
        in_specs=[
            pl.BlockSpec((tm, D_MODEL), lambda i: (i, 0)),
            pl.BlockSpec((tm, D_ATTN), lambda i: (i, 0)),
            pl.BlockSpec((tm, D_HG), lambda i: (i, 0)),
            pl.BlockSpec((D_MODEL, D_MODEL), lambda i: (0, 0), pipeline_mode=pl.Buffered(1)),
            pl.BlockSpec((1, D_MODEL), lambda i: (0, 0)),
        ],
        out_specs=pl.BlockSpec((tm, D_MODEL), lambda i: (i, 0)),
        out_shape=jax.ShapeDtypeStruct((t, D_MODEL), F32),
        compiler_params=_cparams(("parallel",)),
        name="mixer_out_proj",
    )(x, attn, rec, w_out, post)


def _ple_kernel(x_ref, p_ref, pre_ref, post_ref, wg_ref, wp_ref, o_ref):
    nsub = x_ref.shape[0] // SUB_ROWS
    sub = [slice(r * SUB_ROWS, (r + 1) * SUB_ROWS) for r in range(nsub)]
    hs, gates, projs = [None] * nsub, [None] * nsub, [None] * nsub

    def norm(r):
        hs[r] = _rms(x_ref[sub[r], :], pre_ref[...]).astype(BF16)

    def project(r):
        gates[r] = jnp.dot(hs[r], wg_ref[...], preferred_element_type=F32)
        projs[r] = jnp.dot(p_ref[sub[r], :].astype(BF16), wp_ref[...], preferred_element_type=F32)

    def finish(r):
        o_ref[sub[r], :] = x_ref[sub[r], :] + _rms(_sigmoid(gates[r]) * projs[r], post_ref[...])

    _pipelined(nsub, norm, project, finish)


def _ple(x, p, pre, post, w_gate, w_proj):
    t = x.shape[0]
    tm = min(ROW_TM, t)
    return pl.pallas_call(
        _ple_kernel,
        grid=(t // tm,),
        in_specs=[
            pl.BlockSpec((tm, D_MODEL), lambda i: (i, 0)),
            pl.BlockSpec((tm, PLE_DIM), lambda i: (i, 0)),
            pl.BlockSpec((1, D_MODEL), lambda i: (0, 0)),
            pl.BlockSpec((1, D_MODEL), lambda i: (0, 0)),
            pl.BlockSpec((D_MODEL, D_MODEL), lambda i: (0, 0), pipeline_mode=pl.Buffered(1)),
            pl.BlockSpec((PLE_DIM, D_MODEL), lambda i: (0, 0)),
        ],
        out_specs=pl.BlockSpec((tm, D_MODEL), lambda i: (i, 0)),
        out_shape=jax.ShapeDtypeStruct((t, D_MODEL), F32),
        compiler_params=_cparams(("parallel",)),
        name="ple_embed",
    )(x, p, pre, post, w_gate, w_proj)


def _t5_bucket(rel):
    half = NUM_BUCKETS // 2
    max_exact = half // 2
    n = jnp.abs(rel)
    nf = jnp.maximum(n, 1).astype(jnp.float32)
    large = max_exact + (jnp.log(nf / max_exact) / math.log(MAX_DISTANCE / max_exact)
                         * (half - max_exact)).astype(jnp.int32)
    large = jnp.minimum(large, half - 1)
    return jnp.where(rel > 0, half, 0) + jnp.where(n < max_exact, n, large)


def _ffn1(x, w):
    row = lambda v: v.reshape(1, -1)
    return _ffn(x, row(w['ffn1_pre']), row(w['ffn1_post']), w['ffn1_w_gate'], w['ffn1_w_up'], w['ffn1_w_down'])


def _back_half(x, p, attn, rec, w):
    row = lambda v: v.reshape(1, -1)
    x = _outproj(x, attn, rec, w['w_out'], row(w['mix_post']))
    x = _ffn(x, row(w['ffn2_pre']), row(w['ffn2_post']), w['ffn2_w_gate'], w['ffn2_w_up'], w['ffn2_w_down'])
    return _ple(x, p, row(w['ple_pre']), row(w['ple_post']), w['w_ple_gate'], w['w_ple_proj'])


def _prompt_layer(x, p, w, table, bucket, layer, *, n_seq, seq_len):
    x = _ffn1(x, w)
    attn, rec, kv, s_new = _mixer_front(
        x, w['mix_pre'].reshape(1, -1), w['w_in'], bucket, table, w['attn_sinks'], w['hgrn_lb_logits'],
        w['hgrn_norm'].reshape(1, -1), n_seq=n_seq, seq_len=seq_len, layer=layer)
    return _back_half(x, p, attn, rec, w), kv, s_new


def _sample_layer(x, p, w, table, bucket, layer, *, n_seq, s0, k_cache, v_cache):
    kv_w = N_KV * HEAD_DIM
    x = _ffn1(x, w)
    attn, rec, kv, s_new = _mixer_front_cached(
        x, w['mix_pre'].reshape(1, -1), w['w_in'], bucket, table, w['attn_sinks'], w['hgrn_lb_logits'],
        w['hgrn_norm'].reshape(1, -1), s0, k_cache.reshape(n_seq * WINDOW, kv_w),
        v_cache.reshape(n_seq * WINDOW, kv_w), n_seq=n_seq, layer=layer)
    return _back_half(x, p, attn, rec, w), kv, s_new


def kernel(x_prompt, x_sample, cache_attn_k, cache_attn_v, state_hgrn, p_prompt, p_sample,
           rel_bias_table, ffn1_pre, ffn1_post, ffn1_w_gate, ffn1_w_up, ffn1_w_down,
           mix_pre, mix_post, w_in, w_out, attn_sinks, hgrn_lb_logits, hgrn_norm,
           ffn2_pre, ffn2_post, ffn2_w_gate, ffn2_w_up, ffn2_w_down,
           ple_pre, ple_post, w_ple_gate, w_ple_proj):
    depth = w_in.shape[0]
    bp, sp, _ = x_prompt.shape
    bs, ss, _ = x_sample.shape
    wc = cache_attn_k.shape[2]
    assert wc == WINDOW and ss == CHUNK and sp % MIX_UNIT == 0 and sp >= WINDOW
    assert (bs * ss) % MIX_UNIT == 0

    rel = jnp.arange(LK)[None, :] - WINDOW - jnp.arange(CHUNK)[:, None]
    bucket = _t5_bucket(rel).astype(jnp.int32)

    kv_w = N_KV * HEAD_DIM
    k_lo, v_lo = D_ATTN, D_ATTN + kv_w
    yp = x_prompt.reshape(bp * sp, D_MODEL)
    ys = x_sample.reshape(bs * ss, D_MODEL)
    outs = [[] for _ in range(6)]
    for l in range(depth):
        w_in_l = w_in[l]
        w = {
            'ffn1_pre': ffn1_pre[l], 'ffn1_post': ffn1_post[l],
            'ffn1_w_gate': ffn1_w_gate[l].astype(BF16), 'ffn1_w_up': ffn1_w_up[l].astype(BF16),
            'ffn1_w_down': ffn1_w_down[l].astype(BF16),
            'mix_pre': mix_pre[l], 'mix_post': mix_post[l],
            'w_in': jnp.concatenate([w_in_l[:, :k_lo], w_in_l[:, v_lo + kv_w:], w_in_l[:, k_lo:v_lo + kv_w]],
                                    axis=1).astype(BF16),
            'w_out': w_out[l].astype(BF16), 'attn_sinks': attn_sinks[l],
            'hgrn_lb_logits': hgrn_lb_logits, 'hgrn_norm': hgrn_norm[l],
            'ffn2_pre': ffn2_pre[l], 'ffn2_post': ffn2_post[l],
            'ffn2_w_gate': ffn2_w_gate[l].astype(BF16), 'ffn2_w_up': ffn2_w_up[l].astype(BF16),
            'ffn2_w_down': ffn2_w_down[l].astype(BF16),
            'ple_pre': ple_pre[l], 'ple_post': ple_post[l],
            'w_ple_gate': w_ple_gate[l].astype(BF16), 'w_ple_proj': w_ple_proj[l].astype(BF16),
        }
        yp, kv_p, st_p = _prompt_layer(yp, p_prompt[l].reshape(bp * sp, PLE_DIM), w, rel_bias_table, bucket, l,
                                       n_seq=bp, seq_len=sp)
        ys, kv_s, st_s = _sample_layer(ys, p_sample[l].reshape(bs * ss, PLE_DIM), w, rel_bias_table, bucket, l,
                                       n_seq=bs, s0=state_hgrn[l], k_cache=cache_attn_k[l],
                                       v_cache=cache_attn_v[l])
        kv_p = kv_p.reshape(bp, sp, 2 * kv_w)[:, sp - WINDOW:].reshape(bp, WINDOW, 2, N_KV, HEAD_DIM)
        outs[0].append(kv_p[:, :, 0])
        outs[1].append(kv_p[:, :, 1])
        outs[2].append(st_p)
        kv_s = kv_s.reshape(bs, ss, 2, N_KV, HEAD_DIM)
        outs[3].append(jnp.concatenate([cache_attn_k[l], kv_s[:, :, 0]], axis=1)[:, ss:])
        outs[4].append(jnp.concatenate([cache_attn_v[l], kv_s[:, :, 1]], axis=1)[:, ss:])
        outs[5].append(st_s)
    return (yp.reshape(bp, sp, D_MODEL), ys.reshape(bs, ss, D_MODEL),
            jnp.stack(outs[0]), jnp.stack(outs[1]), jnp.stack(outs[2]),
            jnp.stack(outs[3]), jnp.stack(outs[4]), jnp.stack(outs[5]))
```
